```python
import math
import jax, jax.numpy as jnp
from jax import lax
import numpy as np

D_MODEL = 1024
BATCH = 2
SEQ = 8192
DEPTH = 4
DEC_BATCH = 32
DEC_SEQ = 4
PAST_LEN = 8192
PAGE_SIZE = 128

N_AB = (DEPTH + 1) // 2
N_CL = DEPTH // 2
EPS = 1e-6
ML_HEADS = 4
ML_DIM = D_MODEL // 8
ML_WIDTH = ML_HEADS * ML_DIM
CONV_W = 4
ML_CHUNK = 64
NSA_HEADS = 8
NSA_KV = 2
NSA_GROUP = NSA_HEADS // NSA_KV
NSA_DIM = D_MODEL // 16
CMP_LEN = 32
CMP_STRIDE = 16
CMP_PIECES = CMP_LEN // CMP_STRIDE
PHI_HIDDEN = D_MODEL // 16
SEL_BLOCK = 64
N_SEL = 16
WINDOW = 512
NSA_QB = 64
MIX_WIDTH = ML_WIDTH + NSA_HEADS * NSA_DIM
AB_SIZES = (ML_WIDTH, ML_WIDTH, ML_WIDTH, ML_WIDTH, ML_HEADS, ML_HEADS,
            NSA_HEADS * NSA_DIM, 6 * NSA_KV * NSA_DIM, 3 * NSA_HEADS)
AB_IN = sum(AB_SIZES)
GM_WIDTH = D_MODEL
GM_GROUPS = 4
GM_CHUNK = 128
FFN = 4 * D_MODEL

kernel_name = 'hybrid_mlstm_nsa_gmlp_step'


def rmsnorm(x, g):
    xf = x.astype(jnp.float32)
    y = xf * lax.rsqrt(jnp.mean(xf * xf, axis=-1, keepdims=True) + EPS)
    return (y * g.astype(jnp.float32)).astype(x.dtype)


def masked_softmax(s, mask):
    s = jnp.where(mask, s.astype(jnp.float32), -jnp.inf)
    m = jnp.max(s, axis=-1, keepdims=True)
    p = jnp.exp(s - jnp.where(jnp.isfinite(m), m, 0.0))
    d = jnp.sum(p, axis=-1, keepdims=True)
    return p / jnp.where(d > 0, d, 1.0)


def causal_conv(x, buf, w, b):
    T = x.shape[1]
    xp = jnp.concatenate([buf.astype(x.dtype), x], axis=1)
    y = b
    for j in range(CONV_W):
        y = y + xp[:, j:j + T] * w[j]
    return y, xp[:, T:]


def mlstm_scan(q, k, v, log_i, log_f, C0, n0, m0):
    B, T, H, D = q.shape
    L = math.gcd(ML_CHUNK, T)
    nc = T // L
    f32 = jnp.float32

    def chunks(a):
        return jnp.moveaxis(a.astype(f32).reshape((B, nc, L) + a.shape[2:]), 1, 0)

    causal = jnp.tril(jnp.ones((L, L), dtype=bool))[None, :, :, None]

    def step(carry, xs):
        C, n, m = carry
        qc, kc, vc, lic, lfc = xs
        b = jnp.cumsum(lfc, axis=1)
        a = m[:, None] + b
        dmat = jnp.where(causal, b[:, :, None] - b[:, None] + lic[:, None], -jnp.inf)
        mt = jnp.maximum(a, dmat.max(axis=2))
        w_inter = jnp.exp(a - mt)
        qk = jnp.einsum('bthd,bshd->btsh', qc, kc) * jnp.exp(dmat - mt[:, :, None])
        num = w_inter[..., None] * jnp.einsum('bthd,bhde->bthe', qc, C) + jnp.einsum('btsh,bshe->bthe', qk, vc)
        den = w_inter * jnp.einsum('bthd,bhd->bth', qc, n) + qk.sum(axis=2)
        h = num / jnp.maximum(jnp.abs(den), jnp.exp(-mt))[..., None]
        bl = b[:, -1]
        g = bl[:, None] - b + lic
        m_new = jnp.maximum(m + bl, g.max(axis=1))
        decay = jnp.exp(m + bl - m_new)
        ws = jnp.exp(g - m_new[:, None])
        C_new = decay[..., None, None] * C + jnp.einsum('bsh,bshd,bshe->bhde', ws, kc, vc)
        n_new = decay[..., None] * n + jnp.einsum('bsh,bshd->bhd', ws, kc)
        return (C_new, n_new, m_new), h

    (C, n, m), h = lax.scan(step, (C0.astype(f32), n0.astype(f32), m0.astype(f32)),
                            tuple(chunks(a) for a in (q, k, v, log_i, log_f)))
    return jnp.moveaxis(h, 0, 1).reshape(B, T, H, D), C, n, m


def compress_blocks(k, pe, w1, b1, w2, b2):
    B, TK, G, hd = k.shape
    n_sub = TK // CMP_STRIDE
    nc = n_sub - CMP_PIECES + 1
    sub = k[:, :n_sub * CMP_STRIDE].reshape(B, n_sub, CMP_STRIDE, G, hd)
    pre = b1
    piece_w = CMP_STRIDE * hd
    for r in range(CMP_PIECES):
        piece = sub + pe[r * CMP_STRIDE:(r + 1) * CMP_STRIDE][:, None, :]
        piece = jnp.swapaxes(piece, 2, 3).reshape(B, n_sub, G, piece_w)
        pre = pre + (piece @ w1[r * piece_w:(r + 1) * piece_w])[:, r:r + nc]
    return jax.nn.gelu(pre) @ w2 + b2


def to_blocks(k, nb):
    B, TK, G, hd = k.shape
    k = jnp.pad(k, ((0, 0), (0, nb * SEL_BLOCK - TK), (0, 0), (0, 0)))
    return k.reshape(B, nb, SEL_BLOCK, G, hd).transpose(0, 3, 1, 2, 4)


def nsa_attend(q, g, qpos, kc, vc, c_end, ovl, ks_bg, vs_bg, kw, vw, kwpos):
    B, Q = q.shape[:2]
    scale = NSA_DIM ** -0.5
    dt = vc.dtype
    s = jnp.einsum('bqgrd,bngd->bqgrn', q, kc) * scale
    p_cmp = masked_softmax(s, (c_end[None, :] <= qpos[:, None])[None, :, None, None, :])
    o_cmp = jnp.einsum('bqgrn,bngd->bqgrd', p_cmp.astype(dt), vc)
    nb = ks_bg.shape[2]
    imp = jnp.einsum('bqgn,nj->bqgj', p_cmp.sum(axis=3), ovl)
    blk = jnp.arange(nb)
    cur = qpos // SEL_BLOCK
    avail = blk[None, :] * SEL_BLOCK <= qpos[:, None]
    forced = (blk[None, :] == 0) | (blk[None, :] == cur[:, None]) | (blk[None, :] == cur[:, None] - 1)
    imp = jnp.where(avail[None, :, None], jnp.where(forced[None, :, None], jnp.inf, imp), -jnp.inf)
    _, idx = lax.top_k(imp, min(N_SEL, nb))
    n_sel = idx.shape[-1]
    bi = jnp.arange(B)[:, None, None, None]
    gi = jnp.arange(NSA_KV)[None, None, :, None]
    k_sel = ks_bg[bi, gi, idx].reshape(B, Q, NSA_KV, n_sel * SEL_BLOCK, NSA_DIM)
    v_sel = vs_bg[bi, gi, idx].reshape(B, Q, NSA_KV, n_sel * SEL_BLOCK, NSA_DIM)
    kpos = (idx[..., None] * SEL_BLOCK + jnp.arange(SEL_BLOCK)).reshape(B, Q, NSA_KV, 1, n_sel * SEL_BLOCK)
    s = jnp.einsum('bqgrd,bqgmd->bqgrm', q, k_sel) * scale
    p = masked_softmax(s, kpos <= qpos[None, :, None, None, None])
    o_slc = jnp.einsum('bqgrm,bqgmd->bqgrd', p.astype(dt), v_sel)
    s = jnp.einsum('bqgrd,blgd->bqgrl', q, kw) * scale
    wmask = (kwpos[None, :] <= qpos[:, None]) & (kwpos[None, :] >= qpos[:, None] - WINDOW) & (kwpos[None, :] >= 0)
    p = masked_softmax(s, wmask[None, :, None, None, :])
    o_win = jnp.einsum('bqgrl,blgd->bqgrd', p.astype(dt), vw)
    o = g[..., 0:1] * o_cmp + g[..., 1:2] * o_slc + g[..., 2:3] * o_win
    return o.reshape(B, Q, NSA_HEADS * NSA_DIM)


def ab_mixer(h, W, j, st, page_table):
    B, T, _ = h.shape
    dt = h.dtype
    splits = np.cumsum(AB_SIZES)[:-1].tolist()
    (q_pre, k_pre, v_m, o_pre, i_pre, f_pre, q_n, kv_n, g_n) = jnp.split(
        h @ W['ab_w_in'][j] + W['ab_b_in'][j], splits, axis=-1)
    if st is None:
        conv_buf = jnp.zeros((B, CONV_W - 1, 2 * ML_WIDTH), dt)
        C0 = jnp.zeros((B, ML_HEADS, ML_DIM, ML_DIM), jnp.float32)
        n0 = jnp.zeros((B, ML_HEADS, ML_DIM), jnp.float32)
        m0 = jnp.zeros((B, ML_HEADS), jnp.float32)
    else:
        conv_buf, C0, n0, m0, kv_cache, win_buf = st
    qk, conv_new = causal_conv(jnp.concatenate([q_pre, k_pre], axis=-1), conv_buf,
                               W['ml_conv_w'][j], W['ml_conv_b'][j])
    qm, km = jnp.split(jax.nn.silu(qk), 2, axis=-1)
    heads = lambda a: a.reshape(B, T, ML_HEADS, ML_DIM)
    log_f = jax.nn.log_sigmoid((f_pre + W['ml_f_bias'][j]).astype(jnp.float32))
    hm, C1, n1, m1 = mlstm_scan(heads(qm), heads(km) * (ML_DIM ** -0.5), heads(v_m), i_pre, log_f, C0, n0, m0)
    hm = rmsnorm(hm.astype(dt), W['ml_out_g'][j].reshape(ML_HEADS, ML_DIM)).reshape(B, T, ML_WIDTH)
    hm = hm * jax.nn.sigmoid(o_pre)
    qn = q_n.reshape(B, T, NSA_KV, NSA_GROUP, NSA_DIM)
    gn = jax.nn.sigmoid(g_n).reshape(B, T, NSA_KV, NSA_GROUP, 3)
    kv = kv_n.reshape(B, T, 6, NSA_KV, NSA_DIM)
    kv_rows, win_rows = kv[:, :, :4], kv[:, :, 4:]
    if st is None:
        past = 0
        full = kv_rows
    else:
        past = page_table.shape[1] * PAGE_SIZE
        gathered = kv_cache[page_table].reshape((B, past) + kv_rows.shape[2:]).astype(dt)
        full = jnp.concatenate([gathered, kv_rows], axis=1)
    tk = past + T
    kc = compress_blocks(full[:, :, 0], W['phi_pe'][j, 0], W['phi_w1'][j, 0], W['phi_b1'][j, 0],
                         W['phi_w2'][j, 0], W['phi_b2'][j, 0])
    vc = compress_blocks(full[:, :, 1], W['phi_pe'][j, 1], W['phi_w1'][j, 1], W['phi_b1'][j, 1],
                         W['phi_w2'][j, 1], W['phi_b2'][j, 1])
    nc = kc.shape[1]
    c_start = jnp.arange(nc) * CMP_STRIDE
    c_end = c_start + CMP_LEN - 1
    nb = -(-tk // SEL_BLOCK)
    b_start = jnp.arange(nb) * SEL_BLOCK
    ovl = ((c_start[:, None] < b_start[None, :] + SEL_BLOCK) &
           (c_start[:, None] + CMP_LEN > b_start[None, :])).astype(jnp.float32)
    ks_bg = to_blocks(full[:, :, 2], nb)
    vs_bg = to_blocks(full[:, :, 3], nb)
    if st is None:
        qb = math.gcd(NSA_QB, T)
        win_pad = jnp.pad(win_rows, ((0, 0), (WINDOW, 0), (0, 0), (0, 0), (0, 0)))

        def block(i):
            s0 = i * qb
            wb = lax.dynamic_slice_in_dim(win_pad, s0, WINDOW + qb, axis=1)
            return nsa_attend(lax.dynamic_slice_in_dim(qn, s0, qb, axis=1),
                              lax.dynamic_slice_in_dim(gn, s0, qb, axis=1),
                              s0 + jnp.arange(qb), kc, vc, c_end, ovl, ks_bg, vs_bg,
                              wb[:, :, 0], wb[:, :, 1], s0 - WINDOW + jnp.arange(WINDOW + qb))

        o_nsa = lax.map(block, jnp.arange(T // qb))
        o_nsa = jnp.moveaxis(o_nsa, 0, 1).reshape(B, T, NSA_HEADS * NSA_DIM)
        win_new = win_rows[:, -min(WINDOW, T):]
    else:
        wk = jnp.concatenate([win_buf.astype(dt), win_rows], axis=1)
        wb_len = win_buf.shape[1]
        kwpos = past - wb_len + jnp.arange(wb_len + T)
        o_nsa = nsa_attend(qn, gn, past + jnp.arange(T), kc, vc, c_end, ovl, ks_bg, vs_bg,
                           wk[:, :, 0], wk[:, :, 1], kwpos)
        win_new = wk[:, -wb_len:]
    y = jnp.concatenate([hm, o_nsa], axis=-1) @ W['ab_w_out'][j]
    return y, (conv_new, C1, n1, m1, kv_rows, win_new)


def cl_mixer(h, W, j):
    B, T, _ = h.shape
    z = jax.nn.gelu(h @ W['cl_w_in'][j] + W['cl_b_in'][j])
    u, v = jnp.split(z, 2, axis=-1)
    v = rmsnorm(v, W['cl_v_g'][j])
    L = min(GM_CHUNK, T)
    vr = v.reshape(B, T // L, L, GM_GROUPS, GM_WIDTH // GM_GROUPS)
    ws = jnp.where(jnp.tril(jnp.ones((L, L), dtype=bool))[None], W['cl_ws'][j][:, :L, :L], 0.0)
    s = jnp.einsum('gts,bcsgd->bctgd', ws, vr) + W['cl_bs'][j][:, :L].T[None, None, :, :, None]
    y = (u * s.reshape(B, T, GM_WIDTH)) @ W['cl_w_out'][j]
    return y, v


def ffn(h, w1, w2):
    return jnp.square(jax.nn.relu(h @ w1)) @ w2


def trunk(x, c, W, state, page_table):
    B = x.shape[0]
    sc = jax.nn.silu(c)
    ab_new, cl_new = [], []
    for l in range(DEPTH):
        mod = (sc @ W['ada_w'][l] + W['ada_b'][l]).reshape(B, 6, 1, D_MODEL)
        h = rmsnorm(x, W['norm_g'][l, 0]) * (1 + mod[:, 1]) + mod[:, 0]
        if l % 2 == 0:
            st = None if state is None else tuple(a[l // 2] for a in state)
            y, new = ab_mixer(h, W, l // 2, st, page_table)
            ab_new.append(new)
        else:
            y, new = cl_mixer(h, W, l // 2)
            cl_new.append(new)
        x = x + mod[:, 2] * y
        h = rmsnorm(x, W['norm_g'][l, 1]) * (1 + mod[:, 4]) + mod[:, 3]
        x = x + mod[:, 5] * ffn(h, W['ffn_w1'][l], W['ffn_w2'][l])
    return rmsnorm(x, W['final_g']), ab_new, cl_new


def setup_inputs(seed: int = 0) -> dict:
    key = jax.random.key(seed)
    ks = iter(jax.random.split(key, 48))
    D = D_MODEL

    def nrm(shape, s=1.0):
        return s * jax.random.normal(next(ks), shape, jnp.float32)

    n_pages = PAST_LEN // PAGE_SIZE
    n_pool = (5 * DEC_BATCH * n_pages) // 4
    win_buf = min(WINDOW, PAST_LEN)
    page_table = jax.random.permutation(next(ks), n_pool)[:DEC_BATCH * n_pages]
    page_table = page_table.reshape(DEC_BATCH, n_pages).astype(jnp.int32)
    return {
        'x_prompt': nrm((BATCH, SEQ, D)),
        'x_sample': nrm((DEC_BATCH, DEC_SEQ, D)),
        'c_prompt': nrm((BATCH, D)),
        'c_sample': nrm((DEC_BATCH, D)),
        'state_mlstm_conv': nrm((N_AB, DEC_BATCH, CONV_W - 1, 2 * ML_WIDTH)),
        'state_mlstm_C': nrm((N_AB, DEC_BATCH, ML_HEADS, ML_DIM, ML_DIM), 0.1),
        'state_mlstm_n': nrm((N_AB, DEC_BATCH, ML_HEADS, ML_DIM), 0.1),
        'state_mlstm_m': nrm((N_AB, DEC_BATCH, ML_HEADS)),
        'cache_nsa_kv': nrm((N_AB, n_pool, PAGE_SIZE, 4, NSA_KV, NSA_DIM)),
        'state_nsa_win': nrm((N_AB, DEC_BATCH, win_buf, 2, NSA_KV, NSA_DIM)),
        'page_table': page_table,
        'ada_w': nrm((DEPTH, D, 6 * D), 0.5 * D ** -0.5),
        'ada_b': nrm((DEPTH, 6 * D), 0.02),
        'norm_g': 1.0 + nrm((DEPTH, 2, D), 0.02),
        'ab_w_in': nrm((N_AB, D, AB_IN), D ** -0.5),
        'ab_b_in': nrm((N_AB, AB_IN), 0.02),
        'ml_conv_w': nrm((N_AB, CONV_W, 2 * ML_WIDTH), CONV_W ** -0.5),
        'ml_conv_b': nrm((N_AB, 2 * ML_WIDTH), 0.02),
        'ml_f_bias': jax.random.uniform(next(ks), (N_AB, ML_HEADS), jnp.float32, 3.0, 6.0),
        'ml_out_g': 1.0 + nrm((N_AB, ML_WIDTH), 0.02),
        'phi_pe': nrm((N_AB, 2, CMP_LEN, NSA_DIM), 0.1),
        'phi_w1': nrm((N_AB, 2, CMP_LEN * NSA_DIM, PHI_HIDDEN), (CMP_LEN * NSA_DIM) ** -0.5),
        'phi_b1': nrm((N_AB, 2, PHI_HIDDEN), 0.02),
        'phi_w2': nrm((N_AB, 2, PHI_HIDDEN, NSA_DIM), PHI_HIDDEN ** -0.5),
        'phi_b2': nrm((N_AB, 2, NSA_DIM), 0.02),
        'ab_w_out': nrm((N_AB, MIX_WIDTH, D), MIX_WIDTH ** -0.5),
        'cl_w_in': nrm((N_CL, D, 2 * GM_WIDTH), D ** -0.5),
        'cl_b_in': nrm((N_CL, 2 * GM_WIDTH), 0.02),
        'cl_v_g': 1.0 + nrm((N_CL, GM_WIDTH), 0.02),
        'cl_ws': nrm((N_CL, GM_GROUPS, GM_CHUNK, GM_CHUNK), GM_CHUNK ** -0.5),
        'cl_bs': 1.0 + nrm((N_CL, GM_GROUPS, GM_CHUNK), 0.1),
        'cl_w_out': nrm((N_CL, GM_WIDTH, D), GM_WIDTH ** -0.5),
        'ffn_w1': nrm((DEPTH, D, FFN), D ** -0.5),
        'ffn_w2': nrm((DEPTH, FFN, D), FFN ** -0.5),
        'final_g': 1.0 + nrm((D,), 0.02),
    }


def reference(x_prompt, x_sample, c_prompt, c_sample, state_mlstm_conv, state_mlstm_C, state_mlstm_n,
              state_mlstm_m, cache_nsa_kv, state_nsa_win, page_table, ada_w, ada_b, norm_g, ab_w_in, ab_b_in,
              ml_conv_w, ml_conv_b, ml_f_bias, ml_out_g, phi_pe, phi_w1, phi_b1, phi_w2, phi_b2, ab_w_out,
              cl_w_in, cl_b_in, cl_v_g, cl_ws, cl_bs, cl_w_out, ffn_w1, ffn_w2, final_g):
    W = dict(ada_w=ada_w, ada_b=ada_b, norm_g=norm_g, ab_w_in=ab_w_in, ab_b_in=ab_b_in,
             ml_conv_w=ml_conv_w, ml_conv_b=ml_conv_b, ml_f_bias=ml_f_bias, ml_out_g=ml_out_g,
             phi_pe=phi_pe, phi_w1=phi_w1, phi_b1=phi_b1, phi_w2=phi_w2, phi_b2=phi_b2, ab_w_out=ab_w_out,
             cl_w_in=cl_w_in, cl_b_in=cl_b_in, cl_v_g=cl_v_g, cl_ws=cl_ws, cl_bs=cl_bs, cl_w_out=cl_w_out,
             ffn_w1=ffn_w1, ffn_w2=ffn_w2, final_g=final_g)
    y_prompt, ab_p, _ = trunk(x_prompt, c_prompt, W, None, None)
    state = (state_mlstm_conv, state_mlstm_C, state_mlstm_n, state_mlstm_m, cache_nsa_kv, state_nsa_win)
    y_sample, ab_s, cl_s = trunk(x_sample, c_sample, W, state, page_table)
    p_conv, p_C, p_n, p_m, p_kv, p_win = [jnp.stack(a) for a in zip(*ab_p)]
    s_conv, s_C, s_n, s_m, s_kv, s_win = [jnp.stack(a) for a in zip(*ab_s)]
    s_gmlp_v = jnp.stack(cl_s)
    return (y_prompt, y_sample, p_conv, p_C, p_n, p_m, p_kv, p_win,
            s_conv, s_C, s_n, s_m, s_kv, s_win, s_gmlp_v)
```

```python
import functools

import numpy as np
import jax
import jax.numpy as jnp
from jax import lax
from jax.experimental import pallas as pl
from jax.experimental.pallas import tpu as pltpu

F32 = jnp.float32
BF16 = jnp.bfloat16

EPS = 1e-6
D_MODEL = 1024
ML_HEADS = 4
ML_DIM = 128
ML_WIDTH = ML_HEADS * ML_DIM
CONV_W = 4
NSA_HEADS = 8
NSA_KV = 2
NSA_GROUP = NSA_HEADS // NSA_KV
NSA_DIM = 64
NSA_WIDTH = NSA_HEADS * NSA_DIM
CMP_LEN = 32
CMP_STRIDE = 16
SEL_BLOCK = 64
N_SEL = 16
WINDOW = 512
GM_GROUPS = 4
GM_CHUNK = 128
PAGE_SIZE = 128

_SEG_QK = (0, 1024)
_SEG_V = (1024, 1536)
_SEG_O = (1536, 2048)
_SEG_QN = (2048, 2560)
_SEG_KV = (2560, 3328)
_SEG_SM = (3328, 3456)
_AB_COLS = 3456
_SM_I = 24
_SM_F = 28

NEG = -1e30
SEL_BIG = 2.0 ** 100
VMEM_LIMIT = 56 * 1024 * 1024


def _cparams(sem):
    return pltpu.CompilerParams(dimension_semantics=sem, vmem_limit_bytes=VMEM_LIMIT)


def _sigmoid(x):
    return 1.0 / (1.0 + jnp.exp(-x))


def _log_sigmoid(x):
    return jnp.minimum(x, 0.0) - jnp.log1p(jnp.exp(-jnp.abs(x)))


def _gelu(x):
    return 0.5 * x * (1.0 + jnp.tanh(0.7978845608028654 * (x + 0.044715 * (x * x * x))))


def _normmod(x, g, scale, shift):
    y = x * lax.rsqrt(jnp.mean(x * x, axis=-1, keepdims=True) + EPS) * g
    return y * (1.0 + scale) + shift


def _dot(a, b):
    return jnp.dot(a, b, preferred_element_type=F32)


def _dot_nt(a, b):
    return lax.dot_general(a, b, (((1,), (1,)), ((), ())), preferred_element_type=F32)


def _split3(x):
    hi = x.astype(BF16)
    r = x - hi.astype(F32)
    mid = r.astype(BF16)
    lo = (r - mid.astype(F32)).astype(BF16)
    return hi, mid, lo


def _ada_kernel(c_ref, w_ref, b_ref, o_ref):
    c = c_ref[...]
    sc = c * _sigmoid(c)
    sc_hi = sc.astype(BF16)
    sc_lo = (sc - sc_hi.astype(F32)).astype(BF16)
    w = w_ref[0]
    w_hi = w.astype(BF16)
    w_lo = (w - w_hi.astype(F32)).astype(BF16)
    acc = _dot(sc_hi, w_hi) + _dot(sc_lo, w_hi) + _dot(sc_hi, w_lo)
    o_ref[0] = acc + b_ref[0]


def _ada_mod(c_all, ada_w, ada_b):
    depth, d, n = ada_w.shape
    bp = c_all.shape[0]
    tn = 1536
    return pl.pallas_call(
        _ada_kernel,
        grid=(depth, n // tn),
        in_specs=[pl.BlockSpec((bp, d), lambda l, j: (0, 0)),
                  pl.BlockSpec((1, d, tn), lambda l, j: (l, 0, j)),
                  pl.BlockSpec((1, 1, tn), lambda l, j: (l, 0, j))],
        out_specs=pl.BlockSpec((1, bp, tn), lambda l, j: (l, 0, j)),
        out_shape=jax.ShapeDtypeStruct((depth, bp, n), F32),
        compiler_params=_cparams(("arbitrary", "arbitrary")),
        name="ada_mod",
    )(c_all, ada_w, ada_b.reshape(depth, 1, n))


def _mod_spec(mod4, tm, tiles_per_seq):
    r = mod4.shape[2]
    if r == 1:
        return pl.BlockSpec((1, 6, 1, D_MODEL), lambda i: (i // tiles_per_seq, 0, 0, 0))
    assert r == tm
    return pl.BlockSpec((1, 6, tm, D_MODEL), lambda i: (i, 0, 0, 0))


def _ab_in_kernel(x_ref, mod_ref, g_ref, w_ref, b_ref,
                  qk_ref, v_ref, o_ref, qn_ref, kvf_ref, kvb_ref, sm_ref):
    h = _normmod(x_ref[...], g_ref[...], mod_ref[0, 1], mod_ref[0, 0]).astype(BF16)

    def seg(ab):
        a, b = ab
        return _dot(h, w_ref[:, a:b]) + b_ref[:, a:b]

    qk_ref[...] = seg(_SEG_QK)
    v_ref[...] = seg(_SEG_V).astype(BF16)
    o_ref[...] = seg(_SEG_O)
    qn_ref[...] = (seg(_SEG_QN) * (NSA_DIM ** -0.5)).astype(BF16)
    kv = seg(_SEG_KV)
    kvf_ref[...] = kv
    kvb_ref[...] = kv.astype(BF16)
    sm_ref[...] = seg(_SEG_SM)


def _ab_in(x, mod4, tiles_per_seq, g, w, b, tm):
    n = x.shape[0]
    widths = [(1024, F32), (512, BF16), (512, F32), (512, BF16), (768, F32), (768, BF16), (128, F32)]
    return pl.pallas_call(
        _ab_in_kernel,
        grid=(n // tm,),
        in_specs=[pl.BlockSpec((tm, D_MODEL), lambda i: (i, 0)),
                  _mod_spec(mod4, tm, tiles_per_seq),
                  pl.BlockSpec((1, D_MODEL), lambda i: (0, 0)),
                  pl.BlockSpec((D_MODEL, _AB_COLS), lambda i: (0, 0)),
                  pl.BlockSpec((1, _AB_COLS), lambda i: (0, 0))],
        out_specs=[pl.BlockSpec((tm, wd), lambda i: (i, 0)) for wd, _ in widths],
        out_shape=[jax.ShapeDtypeStruct((n, wd), dt) for wd, dt in widths],
        compiler_params=_cparams(("arbitrary",)),
        name="ab_in",
    )(x, mod4, g, w, b)


def _mlstm_kernel(qk_ref, v_ref, o_ref, sm_ref, gr_ref, cw_ref, cb_ref, fbc_ref, fbr_ref, og_ref,
                  conv0_ref, c0_ref, n0_ref, m0_ref,
                  hm_ref, cout_ref, nout_ref, mout_ref,
                  xp_scr, c_scr, n_scr, m_scr, *, lc):
    c = pl.program_id(1)

    @pl.when(c == 0)
    def _():
        xp_scr[0:8, :] = conv0_ref[0]
        c_scr[...] = c0_ref[0]
        n_scr[...] = n0_ref[0]
        m_scr[...] = m0_ref[0]

    xp_scr[8:8 + lc, :] = qk_ref[0]
    y = cb_ref[...]
    for j in range(CONV_W):
        y = y + xp_scr[5 + j:5 + j + lc, :] * cw_ref[j:j + 1, :]
    xp_scr[0:8, :] = xp_scr[lc:lc + 8, :]
    act = y * _sigmoid(y)
    q_all = act[:, :ML_WIDTH]
    k_all = act[:, ML_WIDTH:] * (ML_DIM ** -0.5)

    sm = sm_ref[0]
    gr = gr_ref[0]
    lf_cols = _log_sigmoid(sm + fbc_ref[...])
    lf_rows = _log_sigmoid(gr + fbr_ref[...])
    v_all = v_ref[0]
    o_all = o_ref[0]

    row_i = lax.broadcasted_iota(jnp.int32, (lc, lc), 0)
    col_i = lax.broadcasted_iota(jnp.int32, (lc, lc), 1)
    tril = row_i >= col_i
    triu = row_i <= col_i

    outs = []
    for h in range(ML_HEADS):
        hs = slice(ML_DIM * h, ML_DIM * (h + 1))
        li_c = sm[:, _SM_I + h:_SM_I + h + 1]
        lf_c = lf_cols[:, _SM_F + h:_SM_F + h + 1]
        li_r = gr[h:h + 1, :]
        lf_r = lf_rows[4 + h:5 + h, :]
        b_c = jnp.sum(jnp.where(tril, lf_r, 0.0), axis=1, keepdims=True)
        b_r = jnp.sum(jnp.where(triu, lf_c, 0.0), axis=0, keepdims=True)
        a_r = li_r - b_r
        a_c = li_c - b_c
        m_h = m_scr[h:h + 1, 0:1]
        cm_c = jnp.maximum(m_h, jnp.max(jnp.where(tril, a_r, NEG), axis=1, keepdims=True))
        dm = jnp.exp(jnp.where(tril, a_r - cm_c, NEG))
        w_int = jnp.exp(m_h - cm_c)
        qh = q_all[:, hs]
        kh = k_all[:, hs]
        vh = v_all[:, hs]
        qb = qh.astype(BF16)
        kb = kh.astype(BF16)
        s = _dot_nt(qb, kb) * dm
        c_old = c_scr[h]
        n_old = n_scr[h:h + 1, :]
        num = w_int * _dot(qb, c_old.astype(BF16)) + _dot(s.astype(BF16), vh)
        den = w_int * jnp.sum(qh * n_old, axis=1, keepdims=True) + jnp.sum(s, axis=1, keepdims=True)
        mt = b_c + cm_c
        hh = num / jnp.maximum(jnp.abs(den), jnp.exp(-mt))
        hn = hh * lax.rsqrt(jnp.mean(hh * hh, axis=1, keepdims=True) + EPS) * og_ref[:, hs]
        outs.append(hn * _sigmoid(o_all[:, hs]))
        cm_last = jnp.maximum(m_h, jnp.max(a_r, axis=1, keepdims=True))
        bl = jnp.sum(lf_r, axis=1, keepdims=True)
        decay = jnp.exp(m_h - cm_last)
        ws_c = jnp.exp(a_c - cm_last)
        kt = kh.T.astype(BF16)
        c_scr[h] = decay * c_old + _dot(kt, (ws_c * vh.astype(F32)).astype(BF16))
        n_scr[h:h + 1, :] = decay * n_old + jnp.sum(ws_c * kh, axis=0, keepdims=True)
        m_scr[h:h + 1, :] = jnp.broadcast_to(bl + cm_last, (1, ML_DIM))

    hm_ref[0] = jnp.concatenate(outs, axis=1).astype(BF16)

    @pl.when(c == pl.num_programs(1) - 1)
    def _():
        cout_ref[0] = c_scr[...]
        nout_ref[0] = n_scr[...]
        mout_ref[0] = m_scr[...]


def _mlstm(qk, v, o, sm, gr, conv_w, conv_b, f_bias, out_g, conv0, c0, n0, m0, lc):
    bsz, t, _ = qk.shape
    nc = t // lc
    fbc = jnp.zeros((1, 128), F32).at[0, _SM_F:_SM_F + ML_HEADS].set(f_bias)
    fbr = jnp.zeros((8, 1), F32).at[4:8, 0].set(f_bias)
    conv0p = jnp.concatenate([jnp.zeros((bsz, 5, 2 * ML_WIDTH), F32), conv0], axis=1)
    m0p = jnp.broadcast_to(m0[:, :, None], (bsz, ML_HEADS, ML_DIM))
    tok = lambda wd: pl.BlockSpec((1, lc, wd), lambda b, c: (b, c, 0))
    full2 = lambda a: pl.BlockSpec(a.shape, lambda b, c: (0, 0))
    st3 = lambda a: pl.BlockSpec((1,) + a.shape[1:], lambda b, c: (b,) + (0,) * (a.ndim - 1))
    hm, c1, n1, m1 = pl.pallas_call(
        functools.partial(_mlstm_kernel, lc=lc),
        grid=(bsz, nc),
        in_specs=[tok(1024), tok(512), tok(512), tok(128),
                  pl.BlockSpec((1, 8, lc), lambda b, c: (b, 0, c)),
                  full2(conv_w), full2(conv_b), full2(fbc), full2(fbr), full2(out_g),
                  st3(conv0p), st3(c0), st3(n0), st3(m0p)],
        out_specs=[tok(512), st3(c0), st3(n0), st3(m0p)],
        out_shape=[jax.ShapeDtypeStruct((bsz, t, ML_WIDTH), BF16),
                   jax.ShapeDtypeStruct(c0.shape, F32),
                   jax.ShapeDtypeStruct(n0.shape, F32),
                   jax.ShapeDtypeStruct(m0p.shape, F32)],
        scratch_shapes=[pltpu.VMEM((lc + 8, 2 * ML_WIDTH), F32),
                        pltpu.VMEM((ML_HEADS, ML_DIM, ML_DIM), F32),
                        pltpu.VMEM((ML_HEADS, ML_DIM), F32),
                        pltpu.VMEM((ML_HEADS, ML_DIM), F32)],
        compiler_params=_cparams(("arbitrary", "arbitrary")),
        name="mlstm",
    )(qk, v, o, sm, gr, conv_w, conv_b, fbc, fbr, out_g, conv0p, c0, n0, m0p)
    return hm, c1, n1, m1[:, :, 0]


def _compress_kernel(x_ref, pe_ref, w1_ref, b1_ref, w2_ref, b2_ref, o_ref):
    x = x_ref[0, 0, 0]
    w1 = w1_ref[0]
    xa = (x + pe_ref[0, 0:1, :]).astype(BF16)
    xb = (x + pe_ref[0, 1:2, :]).astype(BF16)
    p0 = _dot(xa, w1[:, :NSA_DIM])
    p1 = _dot(xb, w1[:, NSA_DIM:])
    n_sub = x.shape[0]
    p1 = pltpu.roll(p1, n_sub - 1, 0)
    pre = b1_ref[0] + p0 + p1
    o_ref[0, 0, 0] = (_dot(_gelu(pre).astype(BF16), w2_ref[0]) + b2_ref[0]).astype(BF16)


def _compress(xc, pe, w1, b1, w2, b2):
    _, bsz, g, n_sub, kw = xc.shape
    kind = lambda blk: pl.BlockSpec(blk, lambda k, b, gg: (k,) + (0,) * (len(blk) - 1))
    return pl.pallas_call(
        _compress_kernel,
        grid=(2, bsz, g),
        in_specs=[pl.BlockSpec((1, 1, 1, n_sub, kw), lambda k, b, gg: (k, b, gg, 0, 0)),
                  kind((1, 2, kw)), kind((1, kw, 2 * NSA_DIM)), kind((1, 1, NSA_DIM)),
                  kind((1, NSA_DIM, NSA_DIM)), kind((1, 1, NSA_DIM))],
        out_specs=pl.BlockSpec((1, 1, 1, n_sub, NSA_DIM), lambda k, b, gg: (k, b, gg, 0, 0)),
        out_shape=jax.ShapeDtypeStruct((2, bsz, g, n_sub, NSA_DIM), BF16),
        compiler_params=_cparams(("arbitrary", "arbitrary", "arbitrary")),
        name="nsa_compress",
    )(xc, pe, w1, b1, w2, b2)


def _nsa_cmp_kernel(q_ref, kc_ref, vc_ref, sm_ref, ovl_ref, o_ref, sel_ref, *, tq, qpos0, n_pick, nc):
    i = pl.program_id(1)
    q = q_ref[0]
    gates = _sigmoid(sm_ref[0])
    n_pad = kc_ref.shape[3]
    nb = ovl_ref.shape[0]
    qpos_c = qpos0 + i * tq + lax.broadcasted_iota(jnp.int32, (tq, 1), 0)
    n_i = lax.broadcasted_iota(jnp.int32, (1, n_pad), 1)
    valid = (n_i * CMP_STRIDE + (CMP_LEN - 1) <= qpos_c) & (n_i < nc)
    blk = lax.broadcasted_iota(jnp.int32, (nb, tq), 0)
    qpos_r = qpos0 + i * tq + lax.broadcasted_iota(jnp.int32, (nb, tq), 1)
    cur = qpos_r // SEL_BLOCK
    avail = blk * SEL_BLOCK <= qpos_r
    forced = (blk == 0) | (blk == cur) | (blk == cur - 1)
    outs = []
    for g in range(NSA_KV):
        kc = kc_ref[0, 0, g]
        vc = vc_ref[0, 0, g]
        psum = jnp.zeros((tq, n_pad), F32)
        for r in range(NSA_GROUP):
            hd = g * NSA_GROUP + r
            s = _dot_nt(q[:, NSA_DIM * hd:NSA_DIM * (hd + 1)], kc)
            s = jnp.where(valid, s, NEG)
            m = jnp.max(s, axis=1, keepdims=True)
            p = jnp.where(valid, jnp.exp(s - m), 0.0)
            d = jnp.sum(p, axis=1, keepdims=True)
            p = p / jnp.where(d > 0, d, 1.0)
            psum = psum + p
            outs.append(_dot(p.astype(BF16), vc) * gates[:, 3 * hd:3 * hd + 1])
        imp = sum(_dot_nt(ovl_ref[...], part) for part in _split3(psum))
        val = jnp.where(avail, jnp.where(forced, 1e9, imp), -1.0)

        def pick(_, carry):
            val, sel = carry
            mx = jnp.max(val, axis=0, keepdims=True)
            first = jnp.min(jnp.where(val == mx, blk, nb), axis=0, keepdims=True)
            hit = blk == first
            return jnp.where(hit, -2.0, val), jnp.where(hit, 1.0, sel)

        _, sel = lax.fori_loop(0, n_pick, pick, (val, jnp.zeros((nb, tq), F32)))
        sel_ref[0, g] = ((sel - 1.0) * SEL_BIG).T.astype(BF16)
    o_ref[0] = jnp.concatenate(outs, axis=1)


def _nsa_cmp(qn, kvc, sm, ovl_t, tq, qpos0, n_pick, nc):
    bsz, t, _ = qn.shape
    n_pad = kvc.shape[3]
    nb = ovl_t.shape[0]
    return pl.pallas_call(
        functools.partial(_nsa_cmp_kernel, tq=tq, qpos0=qpos0, n_pick=n_pick, nc=nc),
        grid=(bsz, t // tq),
        in_specs=[pl.BlockSpec((1, tq, NSA_WIDTH), lambda b, i: (b, i, 0)),
                  pl.BlockSpec((1, 1, NSA_KV, n_pad, NSA_DIM), lambda b, i: (0, b, 0, 0, 0)),
                  pl.BlockSpec((1, 1, NSA_KV, n_pad, NSA_DIM), lambda b, i: (1, b, 0, 0, 0)),
                  pl.BlockSpec((1, tq, 128), lambda b, i: (b, i, 0)),
                  pl.BlockSpec((nb, n_pad), lambda b, i: (0, 0))],
        out_specs=[pl.BlockSpec((1, tq, NSA_WIDTH), lambda b, i: (b, i, 0)),
                   pl.BlockSpec((1, NSA_KV, tq, nb), lambda b, i: (b, 0, i, 0))],
        out_shape=[jax.ShapeDtypeStruct((bsz, t, NSA_WIDTH), F32),
                   jax.ShapeDtypeStruct((bsz, NSA_KV, t, nb), BF16)],
        compiler_params=_cparams(("arbitrary", "arbitrary")),
        name="nsa_cmp_select",
    )(qn, kvc, kvc, sm, ovl_t)


def _flash(q4, k_at, v_at, lo, hi, score_fix, m_scr, l_scr, acc_scr):
    m_scr[...] = jnp.full(m_scr.shape, -3e38, F32)
    l_scr[...] = jnp.zeros(l_scr.shape, F32)
    acc_scr[...] = jnp.zeros(acc_scr.shape, F32)

    def body(j, carry):
        s = score_fix(_dot_nt(q4, k_at(j)), j)
        m_prev = m_scr[...]
        m_new = jnp.maximum(m_prev, jnp.max(s, axis=1, keepdims=True))
        alpha = jnp.exp(m_prev - m_new)
        p = jnp.exp(s - m_new)
        l_scr[...] = alpha * l_scr[...] + jnp.sum(p, axis=1, keepdims=True)
        acc_scr[...] = alpha * acc_scr[...] + _dot(p.astype(BF16), v_at(j))
        m_scr[...] = m_new
        return carry

    lax.fori_loop(lo, hi, body, 0)
    return acc_scr[...] / l_scr[...]


def _nsa_sw_kernel(q_ref, ks_ref, vs_ref, kw_ref, vw_ref, sel_ref, et_ref, sm_ref, o_ref,
                   m_scr, l_scr, acc_scr, *, tq, tk, qpos0, wpos0):
    i = pl.program_id(1)
    q = q_ref[0]
    gates = _sigmoid(sm_ref[0])
    q_first = qpos0 + i * tq
    q_last = q_first + tq - 1
    qpos = q_first + lax.broadcasted_iota(jnp.int32, (tq, 1), 0)
    qpos4 = jnp.concatenate([qpos] * NSA_GROUP, axis=0)
    k_i = lax.broadcasted_iota(jnp.int32, (1, tk), 1)
    outs = []
    for g in range(NSA_KV):
        q4 = jnp.concatenate([q[:, NSA_DIM * (g * NSA_GROUP + r):NSA_DIM * (g * NSA_GROUP + r + 1)]
                              for r in range(NSA_GROUP)], axis=0)
        selb = sel_ref[0, g]

        def tile(ref, j):
            return ref[0, g, pl.ds(pl.multiple_of(j * tk, tk), tk), :]

        def slc_fix(s, j):
            bias = _dot(selb, et_ref[:, pl.ds(pl.multiple_of(j * tk, tk), tk)])
            s = (s.reshape(NSA_GROUP, tq, tk) + bias[None]).reshape(NSA_GROUP * tq, tk)
            return jnp.where(j * tk + k_i <= qpos4, s, NEG)

        def win_fix(s, j):
            kpos = wpos0 + j * tk + k_i
            return jnp.where((kpos <= qpos4) & (kpos >= qpos4 - WINDOW), s, NEG)

        o_s = _flash(q4, functools.partial(tile, ks_ref), functools.partial(tile, vs_ref),
                     0, q_last // tk + 1, slc_fix, m_scr, l_scr, acc_scr)
        w_lo = jnp.maximum(q_first - WINDOW - wpos0, 0) // tk
        o_w = _flash(q4, functools.partial(tile, kw_ref), functools.partial(tile, vw_ref),
                     w_lo, (q_last - wpos0) // tk + 1, win_fix, m_scr, l_scr, acc_scr)
        for r in range(NSA_GROUP):
            hd = g * NSA_GROUP + r
            rows = slice(r * tq, (r + 1) * tq)
            outs.append(o_s[rows] * gates[:, 3 * hd + 1:3 * hd + 2] + o_w[rows] * gates[:, 3 * hd + 2:3 * hd + 3])
    o_ref[0] = jnp.concatenate(outs, axis=1)


def _nsa_sw(qn, ks, vs, kw, vw, selb, et, sm, tq, tk, qpos0, wpos0):
    bsz, t, _ = qn.shape
    nb = et.shape[0]
    kvspec = lambda a: pl.BlockSpec((1,) + a.shape[1:], lambda b, i: (b, 0, 0, 0))
    return pl.pallas_call(
        functools.partial(_nsa_sw_kernel, tq=tq, tk=tk, qpos0=qpos0, wpos0=wpos0),
        grid=(bsz, t // tq),
        in_specs=[pl.BlockSpec((1, tq, NSA_WIDTH), lambda b, i: (b, i, 0)),
                  kvspec(ks), kvspec(vs), kvspec(kw), kvspec(vw),
                  pl.BlockSpec((1, NSA_KV, tq, nb), lambda b, i: (b, 0, i, 0)),
                  pl.BlockSpec(et.shape, lambda b, i: (0, 0)),
                  pl.BlockSpec((1, tq, 128), lambda b, i: (b, i, 0))],
        out_specs=pl.BlockSpec((1, tq, NSA_WIDTH), lambda b, i: (b, i, 0)),
        out_shape=jax.ShapeDtypeStruct((bsz, t, NSA_WIDTH), F32),
        scratch_shapes=[pltpu.VMEM((NSA_GROUP * tq, 1), F32),
                        pltpu.VMEM((NSA_GROUP * tq, 1), F32),
                        pltpu.VMEM((NSA_GROUP * tq, NSA_DIM), F32)],
        compiler_params=_cparams(("arbitrary", "arbitrary")),
        name="nsa_select_window",
    )(qn, ks, vs, kw, vw, selb, et, sm)


def _ab_out_kernel(x_ref, mod_ref, hm_ref, oc_ref, osw_ref, w_ref, o_ref):
    o_nsa = (oc_ref[...] + osw_ref[...]).astype(BF16)
    y = _dot(hm_ref[...], w_ref[:ML_WIDTH, :]) + _dot(o_nsa, w_ref[ML_WIDTH:, :])
    o_ref[...] = x_ref[...] + mod_ref[0, 2] * y


def _ab_out(x, mod4, tiles_per_seq, hm, o_cmp, o_sw, w, tm):
    n = x.shape[0]
    tok = lambda wd: pl.BlockSpec((tm, wd), lambda i: (i, 0))
    return pl.pallas_call(
        _ab_out_kernel,
        grid=(n // tm,),
        in_specs=[tok(D_MODEL), _mod_spec(mod4, tm, tiles_per_seq), tok(ML_WIDTH), tok(NSA_WIDTH),
                  tok(NSA_WIDTH), pl.BlockSpec(w.shape, lambda i: (0, 0))],
        out_specs=tok(D_MODEL),
        out_shape=jax.ShapeDtypeStruct((n, D_MODEL), F32),
        compiler_params=_cparams(("arbitrary",)),
        name="ab_out",
    )(x, mod4, hm, o_cmp, o_sw, w)


def _cl_kernel(x_ref, mod_ref, g_ref, wi_ref, bi_ref, vg_ref, ws_ref, bs_ref, wo_ref, o_ref, v_ref, *, tm):
    x = x_ref[...]
    h = _normmod(x, g_ref[...], mod_ref[0, 1], mod_ref[0, 0]).astype(BF16)
    z = _gelu(_dot(h, wi_ref[...]) + bi_ref[...])
    gw = z.shape[1] // 2
    u = z[:, :gw]
    v = z[:, gw:]
    v = v * lax.rsqrt(jnp.mean(v * v, axis=-1, keepdims=True) + EPS) * vg_ref[...]
    v_ref[...] = v
    vb = v.astype(BF16)
    lch = ws_ref.shape[1]
    tril = lax.broadcasted_iota(jnp.int32, (lch, lch), 0) >= lax.broadcasted_iota(jnp.int32, (lch, lch), 1)
    gd = gw // GM_GROUPS
    rows = []
    for c in range(tm // lch):
        cols = []
        for g in range(GM_GROUPS):
            wsg = jnp.where(tril, ws_ref[g], 0.0).astype(BF16)
            cols.append(_dot(wsg, vb[c * lch:(c + 1) * lch, g * gd:(g + 1) * gd]) + bs_ref[:, g:g + 1])
        rows.append(jnp.concatenate(cols, axis=1))
    s = jnp.concatenate(rows, axis=0) if len(rows) > 1 else rows[0]
    y = _dot((u * s).astype(BF16), wo_ref[...])
    o_ref[...] = x + mod_ref[0, 2] * y


def _cl_mixer(x, mod4, tiles_per_seq, g, wi, bi, vg, ws_eff, bs_eff, wo, tm):
    n = x.shape[0]
    gw = wo.shape[0]
    tok = lambda wd: pl.BlockSpec((tm, wd), lambda i: (i, 0))
    full = lambda a: pl.BlockSpec(a.shape, lambda i: (0,) * a.ndim)
    return pl.pallas_call(
        functools.partial(_cl_kernel, tm=tm),
        grid=(n // tm,),
        in_specs=[tok(D_MODEL), _mod_spec(mod4, tm, tiles_per_seq), full(g), full(wi), full(bi), full(vg),
                  full(ws_eff), full(bs_eff), full(wo)],
        out_specs=[tok(D_MODEL), tok(gw)],
        out_shape=[jax.ShapeDtypeStruct((n, D_MODEL), F32), jax.ShapeDtypeStruct((n, gw), F32)],
        compiler_params=_cparams(("arbitrary",)),
        name="gmlp_mixer",
    )(x, mod4, g, wi, bi, vg, ws_eff, bs_eff, wo)


def _ffn_kernel(x_ref, mod_ref, g_ref, w1_ref, w2_ref, fg_ref, o_ref, h_scr, acc_scr, *, final):
    j = pl.program_id(1)

    @pl.when(j == 0)
    def _():
        h_scr[...] = _normmod(x_ref[...], g_ref[...], mod_ref[0, 4], mod_ref[0, 3]).astype(BF16)
        acc_scr[...] = jnp.zeros(acc_scr.shape, F32)

    a = jnp.maximum(_dot(h_scr[...], w1_ref[...]), 0.0)
    acc_scr[...] += _dot((a * a).astype(BF16), w2_ref[...])

    @pl.when(j == pl.num_programs(1) - 1)
    def _():
        y = x_ref[...] + mod_ref[0, 5] * acc_scr[...]
        if final:
            y = y * lax.rsqrt(jnp.mean(y * y, axis=-1, keepdims=True) + EPS) * fg_ref[...]
        o_ref[...] = y


def _ffn(x, mod4, tiles_per_seq, g, w1, w2, final_g, final, tm, tf):
    n = x.shape[0]
    f = w1.shape[1]
    r = mod4.shape[2]
    if r == 1:
        mspec = pl.BlockSpec((1, 6, 1, D_MODEL), lambda i, j: (i // tiles_per_seq, 0, 0, 0))
    else:
        mspec = pl.BlockSpec((1, 6, tm, D_MODEL), lambda i, j: (i, 0, 0, 0))
    return pl.pallas_call(
        functools.partial(_ffn_kernel, final=final),
        grid=(n // tm, f // tf),
        in_specs=[pl.BlockSpec((tm, D_MODEL), lambda i, j: (i, 0)), mspec,
                  pl.BlockSpec((1, D_MODEL), lambda i, j: (0, 0)),
                  pl.BlockSpec((D_MODEL, tf), lambda i, j: (0, j)),
                  pl.BlockSpec((tf, D_MODEL), lambda i, j: (j, 0)),
                  pl.BlockSpec((1, D_MODEL), lambda i, j: (0, 0))],
        out_specs=pl.BlockSpec((tm, D_MODEL), lambda i, j: (i, 0)),
        out_shape=jax.ShapeDtypeStruct((n, D_MODEL), F32),
        scratch_shapes=[pltpu.VMEM((tm, D_MODEL), BF16), pltpu.VMEM((tm, D_MODEL), F32)],
        compiler_params=_cparams(("arbitrary", "arbitrary")),
        name="ffn",
    )(x, mod4, g, w1, w2, final_g)


def _gather_kernel(pt_ref, *refs):
    n_in = len(refs) - 1
    o_ref = refs[-1]
    for p in range(n_in):
        o_ref[0, p * PAGE_SIZE:(p + 1) * PAGE_SIZE, :] = refs[p][0]


def _gather_pages(cache, page_table, pages_per_step=8):
    _, ps, wd = cache.shape
    bsz, n_pages = page_table.shape
    steps = n_pages // pages_per_step

    def in_map(p):
        return lambda b, s, pt: (pt[b, s * pages_per_step + p], 0, 0)

    return pl.pallas_call(
        _gather_kernel,
        grid_spec=pltpu.PrefetchScalarGridSpec(
            num_scalar_prefetch=1,
            grid=(bsz, steps),
            in_specs=[pl.BlockSpec((1, ps, wd), in_map(p)) for p in range(pages_per_step)],
            out_specs=pl.BlockSpec((1, pages_per_step * ps, wd), lambda b, s, pt: (b, s, 0))),
        out_shape=jax.ShapeDtypeStruct((bsz, n_pages * ps, wd), F32),
        compiler_params=_cparams(("arbitrary", "arbitrary")),
        name="page_gather",
    )(page_table, *([cache] * pages_per_step))


def _selection_constants(n_keys, n_sub, tk_total):
    nb = n_keys // SEL_BLOCK
    i = np.arange(n_sub)[None, :] * CMP_STRIDE
    j = np.arange(nb)[:, None] * SEL_BLOCK
    ovl_t = ((i < j + SEL_BLOCK) & (i + CMP_LEN > j) & (np.arange(n_sub)[None, :] < n_sub - 1))
    pos = np.arange(tk_total)[None, :]
    et = (pos // SEL_BLOCK) == np.arange(nb)[:, None]
    return jnp.asarray(ovl_t, BF16), jnp.asarray(et, BF16)


def _split_kv(kvb, bsz, t):
    return kvb.reshape(bsz, t, 6, NSA_KV, NSA_DIM).transpose(2, 0, 3, 1, 4)


def _cmp_input(rows_f32, bsz, t):
    n_sub = t // CMP_STRIDE
    x = rows_f32[:, :n_sub * CMP_STRIDE].transpose(2, 0, 3, 1, 4)
    return x.reshape(2, bsz, NSA_KV, n_sub, CMP_STRIDE * NSA_DIM)


def _ab_layer(x, mod4, tps, W, l, bsz, t, tm, st, page_table, lc, tq, tk):
    n = bsz * t
    j = l // 2
    qk, v_m, o_pre, qn, kvf, kvb, sm = _ab_in(x, mod4, tps, W['norm_g0'][l], W['ab_w_in'][j], W['ab_b_in'][j], tm)
    qk3 = qk.reshape(bsz, t, 2 * ML_WIDTH)
    sm3 = sm.reshape(bsz, t, 128)
    gr = sm3[:, :, _SM_I:_SM_I + 8].transpose(0, 2, 1)
    v3 = v_m.reshape(bsz, t, ML_WIDTH)
    o3 = o_pre.reshape(bsz, t, ML_WIDTH)
    if st is None:
        conv0 = jnp.zeros((bsz, CONV_W - 1, 2 * ML_WIDTH), F32)
        c0 = jnp.zeros((bsz, ML_HEADS, ML_DIM, ML_DIM), F32)
        n0 = jnp.zeros((bsz, ML_HEADS, ML_DIM), F32)
        m0 = jnp.zeros((bsz, ML_HEADS), F32)
        tp = t
    else:
        conv0, c0, n0, m0, cache, win_buf = st
        tp = lc
        pad = tp - t
        qk3 = jnp.pad(qk3, ((0, 0), (0, pad), (0, 0)))
        v3 = jnp.pad(v3, ((0, 0), (0, pad), (0, 0)))
        o3 = jnp.pad(o3, ((0, 0), (0, pad), (0, 0)))
        gr = jnp.concatenate([jnp.pad(gr[:, :4], ((0, 0), (0, 0), (0, pad)), constant_values=NEG),
                              jnp.pad(gr[:, 4:], ((0, 0), (0, 0), (0, pad)), constant_values=-NEG)], axis=1)
        sm_pad = jnp.zeros((bsz, pad, 128), F32).at[:, :, _SM_I:_SM_I + 4].set(NEG).at[:, :, _SM_F:_SM_F + 4].set(-NEG)
        sm3m = jnp.concatenate([sm3, sm_pad], axis=1)
    if st is None:
        sm3m = sm3
    hm, c1, n1, m1 = _mlstm(qk3, v3, o3, sm3m, gr, W['ml_conv_w'][j], W['ml_conv_b'][j], W['ml_f_bias'][j],
                            W['ml_out_g'][j], conv0, c0, n0, m0, lc)
    hm = hm[:, :t].reshape(n, ML_WIDTH)
    conv_full = jnp.concatenate([conv0, qk.reshape(bsz, t, 2 * ML_WIDTH)], axis=1)
    conv_new = conv_full[:, -(CONV_W - 1):]
    kv_rows = kvf[:, :512].reshape(bsz, t, 4, NSA_KV, NSA_DIM)
    win_rows = kvf[:, 512:].reshape(bsz, t, 2, NSA_KV, NSA_DIM)
    parts = _split_kv(kvb, bsz, t)
    if st is None:
        past = 0
        cmp_rows = kv_rows[:, :, :2]
        k_s, v_s, k_w, v_w = parts[2], parts[3], parts[4], parts[5]
        tkeys = t
        wpos0 = 0
        tq_eff = tq
        qn3 = qn.reshape(bsz, t, NSA_WIDTH)
        sm3q = sm3
        win_new = win_rows[:, -min(WINDOW, t):]
        n_pick = N_SEL
    else:
        past = page_table.shape[1] * PAGE_SIZE
        gathered = _gather_pages(cache.reshape(cache.shape[0], PAGE_SIZE, 4 * NSA_KV * NSA_DIM), page_table)
        gathered = gathered.reshape(bsz, past, 4, NSA_KV, NSA_DIM)
        cmp_rows = gathered[:, :, :2]
        assert (past + t) // CMP_STRIDE == past // CMP_STRIDE
        tkeys = past
        kpad = tk
        gb = gathered[:, :, 2:].astype(BF16).transpose(2, 0, 3, 1, 4)
        zpad = jnp.zeros((bsz, NSA_KV, kpad - t, NSA_DIM), BF16)
        k_s = jnp.concatenate([gb[0], parts[2], zpad], axis=2)
        v_s = jnp.concatenate([gb[1], parts[3], zpad], axis=2)
        wb = win_buf.astype(BF16).transpose(2, 0, 3, 1, 4)
        wlen = win_buf.shape[1]
        wpos0 = past - wlen
        k_w = jnp.concatenate([wb[0], parts[4], zpad], axis=2)
        v_w = jnp.concatenate([wb[1], parts[5], zpad], axis=2)
        tq_eff = tq
        qn3 = jnp.pad(qn.reshape(bsz, t, NSA_WIDTH), ((0, 0), (0, tq - t), (0, 0)))
        sm3q = jnp.pad(sm3, ((0, 0), (0, tq - t), (0, 0)))
        win_new = jnp.concatenate([win_buf, win_rows], axis=1)[:, -wlen:]
        n_pick = N_SEL - 1
        assert t <= SEL_BLOCK and past % SEL_BLOCK == 0
    n_sub = tkeys // CMP_STRIDE
    xc = _cmp_input(cmp_rows, bsz, tkeys)
    kvc = _compress(xc, W['phi_pe'][j], W['phi_w1'][j], W['phi_b1'][j], W['phi_w2'][j], W['phi_b2'][j])
    ovl_t, et = _selection_constants(tkeys, n_sub, k_s.shape[2])
    o_cmp, selb = _nsa_cmp(qn3, kvc, sm3q, ovl_t, tq_eff, past, n_pick, n_sub - 1)
    o_sw = _nsa_sw(qn3, k_s, v_s, k_w, v_w, selb, et, sm3q, tq_eff, tk, past, wpos0)
    o_cmp = o_cmp[:, :t].reshape(n, NSA_WIDTH)
    o_sw = o_sw[:, :t].reshape(n, NSA_WIDTH)
    x = _ab_out(x, mod4, tps, hm, o_cmp, o_sw, W['ab_w_out'][j], tm)
    return x, (conv_new, c1, n1, m1, kv_rows, win_new)


def _trunk(x3, mods, W, state, page_table, cfg):
    bsz, t, _ = x3.shape
    n = bsz * t
    tm = cfg['tm']
    x = x3.reshape(n, D_MODEL)
    depth = mods.shape[0]
    ab_new, cl_new = [], []
    if t % tm == 0:
        tps = t // tm
        to_mod4 = lambda m: m.reshape(bsz, 6, 1, D_MODEL)
    else:
        assert n == tm
        tps = 1
        to_mod4 = lambda m: jnp.repeat(m.reshape(bsz, 6, D_MODEL), t, axis=0).reshape(n, 6, D_MODEL).transpose(1, 0, 2)[None]
    lch = min(GM_CHUNK, t)
    for l in range(depth):
        mod4 = to_mod4(mods[l])
        j = l // 2
        if l % 2 == 0:
            st = None if state is None else tuple(a[j] for a in state)
            x, new = _ab_layer(x, mod4, tps, W, l, bsz, t, tm, st, page_table, cfg['lc'], cfg['tq'], cfg['tk'])
            ab_new.append(new)
        else:
            ws = W['cl_ws'][j][:, :lch, :lch]
            bs = W['cl_bs'][j][:, :lch]
            if lch < GM_CHUNK:
                rep = GM_CHUNK // lch
                ws = jnp.einsum('ab,gts->gatbs', jnp.eye(rep, dtype=F32), ws).reshape(GM_GROUPS, GM_CHUNK, GM_CHUNK)
                bs = jnp.tile(bs, (1, rep))
            x, v = _cl_mixer(x, mod4, tps, W['norm_g0'][l], W['cl_w_in'][j], W['cl_b_in'][j], W['cl_v_g'][j],
                             ws, bs.T, W['cl_w_out'][j], tm)
            cl_new.append(v.reshape(bsz, t, -1))
        x = _ffn(x, mod4, max(t // cfg['tm_ffn'], 1), W['norm_g1'][l], W['ffn_w1'][l], W['ffn_w2'][l], W['final_g'],
                 l == depth - 1, cfg['tm_ffn'], cfg['tf'])
    return x.reshape(bsz, t, D_MODEL), ab_new, cl_new


def kernel(x_prompt, x_sample, c_prompt, c_sample, state_mlstm_conv, state_mlstm_C, state_mlstm_n,
           state_mlstm_m, cache_nsa_kv, state_nsa_win, page_table, ada_w, ada_b, norm_g, ab_w_in, ab_b_in,
           ml_conv_w, ml_conv_b, ml_f_bias, ml_out_g, phi_pe, phi_w1, phi_b1, phi_w2, phi_b2, ab_w_out,
           cl_w_in, cl_b_in, cl_v_g, cl_ws, cl_bs, cl_w_out, ffn_w1, ffn_w2, final_g):
    depth = ada_w.shape[0]
    n_ab = ab_w_in.shape[0]
    bp, bs_ = c_prompt.shape[0], c_sample.shape[0]
    rows = bp + bs_
    rows_pad = -(-rows // 8) * 8
    c_all = jnp.concatenate([c_prompt, c_sample, jnp.zeros((rows_pad - rows, D_MODEL), F32)], axis=0)
    mods = _ada_mod(c_all, ada_w, ada_b)
    mods_p = mods[:, :bp]
    mods_s = mods[:, bp:rows]
    w_in = jnp.concatenate([ab_w_in[:, :, 0:2048], ab_w_in[:, :, 2056:3336], ab_w_in[:, :, 3336:3360],
                            ab_w_in[:, :, 2048:2056], jnp.zeros((n_ab, D_MODEL, 96), F32)], axis=2).astype(BF16)
    b_in = jnp.concatenate([ab_b_in[:, 0:2048], ab_b_in[:, 2056:3336], ab_b_in[:, 3336:3360],
                            ab_b_in[:, 2048:2056], jnp.zeros((n_ab, 96), F32)], axis=1)[:, None, :]
    half = CMP_STRIDE * NSA_DIM
    W = dict(
        norm_g0=norm_g[:, 0][:, None, :], norm_g1=norm_g[:, 1][:, None, :],
        ab_w_in=w_in, ab_b_in=b_in,
        ml_conv_w=ml_conv_w, ml_conv_b=ml_conv_b[:, None, :], ml_f_bias=ml_f_bias, ml_out_g=ml_out_g[:, None, :],
        phi_pe=phi_pe.reshape(n_ab, 2, 2, half),
        phi_w1=jnp.concatenate([phi_w1[:, :, :half], phi_w1[:, :, half:]], axis=3).astype(BF16),
        phi_b1=phi_b1[:, :, None, :], phi_w2=phi_w2.astype(BF16), phi_b2=phi_b2[:, :, None, :],
        ab_w_out=ab_w_out.astype(BF16),
        cl_w_in=cl_w_in.astype(BF16), cl_b_in=cl_b_in[:, None, :], cl_v_g=cl_v_g[:, None, :],
        cl_ws=cl_ws, cl_bs=cl_bs, cl_w_out=cl_w_out.astype(BF16),
        ffn_w1=ffn_w1.astype(BF16), ffn_w2=ffn_w2.astype(BF16), final_g=final_g[None, :])
    cfg_p = dict(tm=256, tm_ffn=512, tf=512, lc=256, tq=128, tk=256)
    y_prompt, ab_p, _ = _trunk(x_prompt, mods_p, W, None, None, cfg_p)
    n_s = x_sample.shape[0] * x_sample.shape[1]
    cfg_s = dict(tm=n_s, tm_ffn=n_s, tf=512, lc=128, tq=128, tk=256)
    state = (state_mlstm_conv, state_mlstm_C, state_mlstm_n, state_mlstm_m, cache_nsa_kv, state_nsa_win)
    y_sample, ab_s, cl_s = _trunk(x_sample, mods_s, W, state, page_table, cfg_s)
    p_out = [jnp.stack(a) for a in zip(*ab_p)]
    s_out = [jnp.stack(a) for a in zip(*ab_s)]
    return (y_prompt, y_sample, *p_out, *s_out, jnp.stack(cl_s))
```

```python
import functools

import numpy as np
import jax
import jax.numpy as jnp
from jax import lax
from jax.experimental import pallas as pl
from jax.experimental.pallas import tpu as pltpu

F32 = jnp.float32
BF16 = jnp.bfloat16

EPS = 1e-6
D_MODEL = 1024
ML_HEADS = 4
ML_DIM = 128
ML_WIDTH = ML_HEADS * ML_DIM
CONV_W = 4
NSA_HEADS = 8
NSA_KV = 2
NSA_GROUP = NSA_HEADS // NSA_KV
NSA_DIM = 64
NSA_WIDTH = NSA_HEADS * NSA_DIM
CMP_LEN = 32
CMP_STRIDE = 16
SEL_BLOCK = 64
N_SEL = 16
WINDOW = 512
GM_GROUPS = 4
GM_CHUNK = 128
PAGE_SIZE = 128

_SEG_QK = (0, 1024)
_SEG_V = (1024, 1536)
_SEG_O = (1536, 2048)
_SEG_QN = (2048, 2560)
_SEG_KV = (2560, 3328)
_SEG_SM = (3328, 3456)
_AB_COLS = 3456
_SM_I = 24
_SM_F = 28

NEG = -1e30
SEL_BIG = 2.0 ** 100
LOG2E = 1.4426950408889634
VMEM_LIMIT = 56 * 1024 * 1024


def _cparams(sem):
    return pltpu.CompilerParams(dimension_semantics=sem, vmem_limit_bytes=VMEM_LIMIT)


def _sigmoid(x):
    return 1.0 / (1.0 + jnp.exp(-x))


def _log_sigmoid(x):
    return jnp.minimum(x, 0.0) - jnp.log1p(jnp.exp(-jnp.abs(x)))


def _gelu(x):
    return 0.5 * x * (1.0 + jnp.tanh(0.7978845608028654 * (x + 0.044715 * (x * x * x))))


def _normmod(x, g, scale, shift):
    y = x * lax.rsqrt(jnp.mean(x * x, axis=-1, keepdims=True) + EPS) * g
    return y * (1.0 + scale) + shift


def _dot(a, b):
    return jnp.dot(a, b, preferred_element_type=F32)


def _dot_nt(a, b):
    return lax.dot_general(a, b, (((1,), (1,)), ((), ())), preferred_element_type=F32)


def _split3(x):
    hi = x.astype(BF16)
    r = x - hi.astype(F32)
    mid = r.astype(BF16)
    lo = (r - mid.astype(F32)).astype(BF16)
    return hi, mid, lo


def _ada_kernel(c_ref, w_ref, b_ref, o_ref):
    c = c_ref[...]
    sc = c * _sigmoid(c)
    sc_hi = sc.astype(BF16)
    sc_lo = (sc - sc_hi.astype(F32)).astype(BF16)
    w = w_ref[0]
    w_hi = w.astype(BF16)
    w_lo = (w - w_hi.astype(F32)).astype(BF16)
    acc = _dot(sc_hi, w_hi) + _dot(sc_lo, w_hi) + _dot(sc_hi, w_lo)
    o_ref[0] = acc + b_ref[0]


def _ada_mod(c_all, ada_w, ada_b):
    depth, d, n = ada_w.shape
    bp = c_all.shape[0]
    tn = 1536
    return pl.pallas_call(
        _ada_kernel,
        grid=(depth, n // tn),
        in_specs=[pl.BlockSpec((bp, d), lambda l, j: (0, 0)),
                  pl.BlockSpec((1, d, tn), lambda l, j: (l, 0, j)),
                  pl.BlockSpec((1, 1, tn), lambda l, j: (l, 0, j))],
        out_specs=pl.BlockSpec((1, bp, tn), lambda l, j: (l, 0, j)),
        out_shape=jax.ShapeDtypeStruct((depth, bp, n), F32),
        compiler_params=_cparams(("arbitrary", "arbitrary")),
        name="ada_mod",
    )(c_all, ada_w, ada_b.reshape(depth, 1, n))


def _mod_spec(mod4, tm, tiles_per_seq):
    r = mod4.shape[2]
    if r == 1:
        return pl.BlockSpec((1, 6, 1, D_MODEL), lambda i: (i // tiles_per_seq, 0, 0, 0))
    assert r == tm
    return pl.BlockSpec((1, 6, tm, D_MODEL), lambda i: (i, 0, 0, 0))


def _ab_in_kernel(x_ref, mod_ref, g_ref, w_ref, b_ref,
                  qk_ref, v_ref, o_ref, qn_ref, kvf_ref, kvb_ref, sm_ref):
    h = _normmod(x_ref[...], g_ref[...], mod_ref[0, 1], mod_ref[0, 0]).astype(BF16)

    def seg(ab):
        a, b = ab
        return _dot(h, w_ref[:, a:b]) + b_ref[:, a:b]

    qk_ref[...] = seg(_SEG_QK)
    v_ref[...] = seg(_SEG_V).astype(BF16)
    o_ref[...] = seg(_SEG_O)
    qn_ref[...] = (seg(_SEG_QN) * (NSA_DIM ** -0.5 * LOG2E)).astype(BF16)
    kv = seg(_SEG_KV)
    kvf_ref[...] = kv
    kvb_ref[...] = kv.astype(BF16)
    sm_ref[...] = seg(_SEG_SM)


def _ab_in(x, mod4, tiles_per_seq, g, w, b, tm):
    n = x.shape[0]
    widths = [(1024, F32), (512, BF16), (512, F32), (512, BF16), (768, F32), (768, BF16), (128, F32)]
    return pl.pallas_call(
        _ab_in_kernel,
        grid=(n // tm,),
        in_specs=[pl.BlockSpec((tm, D_MODEL), lambda i: (i, 0)),
                  _mod_spec(mod4, tm, tiles_per_seq),
                  pl.BlockSpec((1, D_MODEL), lambda i: (0, 0)),
                  pl.BlockSpec((D_MODEL, _AB_COLS), lambda i: (0, 0)),
                  pl.BlockSpec((1, _AB_COLS), lambda i: (0, 0))],
        out_specs=[pl.BlockSpec((tm, wd), lambda i: (i, 0)) for wd, _ in widths],
        out_shape=[jax.ShapeDtypeStruct((n, wd), dt) for wd, dt in widths],
        compiler_params=_cparams(("arbitrary",)),
        name="ab_in",
    )(x, mod4, g, w, b)


def _mlstm_kernel(qk_ref, v_ref, o_ref, sm_ref, gr_ref, cw_ref, cb_ref, fbc_ref, fbr_ref, og_ref,
                  conv0_ref, c0_ref, n0_ref, m0_ref,
                  hm_ref, cout_ref, nout_ref, mout_ref,
                  xp_scr, c_scr, n_scr, m_scr, *, lc):
    c = pl.program_id(1)

    @pl.when(c == 0)
    def _():
        xp_scr[0:8, :] = conv0_ref[0]
        c_scr[...] = c0_ref[0]
        n_scr[...] = n0_ref[0]
        m_scr[...] = m0_ref[0]

    xp_scr[8:8 + lc, :] = qk_ref[0]
    y = cb_ref[...]
    for j in range(CONV_W):
        y = y + xp_scr[5 + j:5 + j + lc, :] * cw_ref[j:j + 1, :]
    xp_scr[0:8, :] = xp_scr[lc:lc + 8, :]
    act = y * _sigmoid(y)
    q_all = act[:, :ML_WIDTH]
    k_all = act[:, ML_WIDTH:] * (ML_DIM ** -0.5)

    sm = sm_ref[0]
    gr = gr_ref[0]
    lf_cols = _log_sigmoid(sm + fbc_ref[...])
    lf_rows = _log_sigmoid(gr + fbr_ref[...])
    v_all = v_ref[0]
    o_all = o_ref[0]

    row_i = lax.broadcasted_iota(jnp.int32, (lc, lc), 0)
    col_i = lax.broadcasted_iota(jnp.int32, (lc, lc), 1)
    tril = row_i >= col_i
    triu = row_i <= col_i

    outs = []
    for h in range(ML_HEADS):
        hs = slice(ML_DIM * h, ML_DIM * (h + 1))
        li_c = sm[:, _SM_I + h:_SM_I + h + 1]
        lf_c = lf_cols[:, _SM_F + h:_SM_F + h + 1]
        li_r = gr[h:h + 1, :]
        lf_r = lf_rows[4 + h:5 + h, :]
        b_c = jnp.sum(jnp.where(tril, lf_r, 0.0), axis=1, keepdims=True)
        b_r = jnp.sum(jnp.where(triu, lf_c, 0.0), axis=0, keepdims=True)
        a_r = li_r - b_r
        a_c = li_c - b_c
        m_h = m_scr[h:h + 1, 0:1]
        cm_c = jnp.maximum(m_h, jnp.max(jnp.where(tril, a_r, NEG), axis=1, keepdims=True))
        dm = jnp.exp(jnp.where(tril, a_r - cm_c, NEG))
        w_int = jnp.exp(m_h - cm_c)
        qh = q_all[:, hs]
        kh = k_all[:, hs]
        vh = v_all[:, hs]
        qb = qh.astype(BF16)
        kb = kh.astype(BF16)
        s = _dot_nt(qb, kb) * dm
        c_old = c_scr[h]
        n_old = n_scr[h:h + 1, :]
        num = w_int * _dot(qb, c_old.astype(BF16)) + _dot(s.astype(BF16), vh)
        den = w_int * jnp.sum(qh * n_old, axis=1, keepdims=True) + jnp.sum(s, axis=1, keepdims=True)
        mt = b_c + cm_c
        hh = num / jnp.maximum(jnp.abs(den), jnp.exp(-mt))
        hn = hh * lax.rsqrt(jnp.mean(hh * hh, axis=1, keepdims=True) + EPS) * og_ref[:, hs]
        outs.append(hn * _sigmoid(o_all[:, hs]))
        cm_last = jnp.maximum(m_h, jnp.max(a_r, axis=1, keepdims=True))
        bl = jnp.sum(lf_r, axis=1, keepdims=True)
        decay = jnp.exp(m_h - cm_last)
        ws_c = jnp.exp(a_c - cm_last)
        kt = kh.T.astype(BF16)
        c_scr[h] = decay * c_old + _dot(kt, (ws_c * vh.astype(F32)).astype(BF16))
        n_scr[h:h + 1, :] = decay * n_old + jnp.sum(ws_c * kh, axis=0, keepdims=True)
        m_scr[h:h + 1, :] = jnp.broadcast_to(bl + cm_last, (1, ML_DIM))

    hm_ref[0] = jnp.concatenate(outs, axis=1).astype(BF16)

    @pl.when(c == pl.num_programs(1) - 1)
    def _():
        cout_ref[0] = c_scr[...]
        nout_ref[0] = n_scr[...]
        mout_ref[0] = m_scr[...]


def _mlstm(qk, v, o, sm, gr, conv_w, conv_b, f_bias, out_g, conv0, c0, n0, m0, lc):
    bsz, t, _ = qk.shape
    nc = t // lc
    fbc = jnp.zeros((1, 128), F32).at[0, _SM_F:_SM_F + ML_HEADS].set(f_bias)
    fbr = jnp.zeros((8, 1), F32).at[4:8, 0].set(f_bias)
    conv0p = jnp.concatenate([jnp.zeros((bsz, 5, 2 * ML_WIDTH), F32), conv0], axis=1)
    m0p = jnp.broadcast_to(m0[:, :, None], (bsz, ML_HEADS, ML_DIM))
    tok = lambda wd: pl.BlockSpec((1, lc, wd), lambda b, c: (b, c, 0))
    full2 = lambda a: pl.BlockSpec(a.shape, lambda b, c: (0, 0))
    st3 = lambda a: pl.BlockSpec((1,) + a.shape[1:], lambda b, c: (b,) + (0,) * (a.ndim - 1))
    hm, c1, n1, m1 = pl.pallas_call(
        functools.partial(_mlstm_kernel, lc=lc),
        grid=(bsz, nc),
        in_specs=[tok(1024), tok(512), tok(512), tok(128),
                  pl.BlockSpec((1, 8, lc), lambda b, c: (b, 0, c)),
                  full2(conv_w), full2(conv_b), full2(fbc), full2(fbr), full2(out_g),
                  st3(conv0p), st3(c0), st3(n0), st3(m0p)],
        out_specs=[tok(512), st3(c0), st3(n0), st3(m0p)],
        out_shape=[jax.ShapeDtypeStruct((bsz, t, ML_WIDTH), BF16),
                   jax.ShapeDtypeStruct(c0.shape, F32),
                   jax.ShapeDtypeStruct(n0.shape, F32),
                   jax.ShapeDtypeStruct(m0p.shape, F32)],
        scratch_shapes=[pltpu.VMEM((lc + 8, 2 * ML_WIDTH), F32),
                        pltpu.VMEM((ML_HEADS, ML_DIM, ML_DIM), F32),
                        pltpu.VMEM((ML_HEADS, ML_DIM), F32),
                        pltpu.VMEM((ML_HEADS, ML_DIM), F32)],
        compiler_params=_cparams(("arbitrary", "arbitrary")),
        name="mlstm",
    )(qk, v, o, sm, gr, conv_w, conv_b, fbc, fbr, out_g, conv0p, c0, n0, m0p)
    return hm, c1, n1, m1[:, :, 0]


def _compress_kernel(x_ref, pe_ref, w1_ref, b1_ref, w2_ref, b2_ref, o_ref):
    x = x_ref[0, 0, 0]
    w1 = w1_ref[0]
    xa = (x + pe_ref[0, 0:1, :]).astype(BF16)
    xb = (x + pe_ref[0, 1:2, :]).astype(BF16)
    p0 = _dot(xa, w1[:, :NSA_DIM])
    p1 = _dot(xb, w1[:, NSA_DIM:])
    n_sub = x.shape[0]
    p1 = pltpu.roll(p1, n_sub - 1, 0)
    pre = b1_ref[0] + p0 + p1
    o_ref[0, 0, 0] = (_dot(_gelu(pre).astype(BF16), w2_ref[0]) + b2_ref[0]).astype(BF16)


def _compress(xc, pe, w1, b1, w2, b2):
    _, bsz, g, n_sub, kw = xc.shape
    kind = lambda blk: pl.BlockSpec(blk, lambda k, b, gg: (k,) + (0,) * (len(blk) - 1))
    return pl.pallas_call(
        _compress_kernel,
        grid=(2, bsz, g),
        in_specs=[pl.BlockSpec((1, 1, 1, n_sub, kw), lambda k, b, gg: (k, b, gg, 0, 0)),
                  kind((1, 2, kw)), kind((1, kw, 2 * NSA_DIM)), kind((1, 1, NSA_DIM)),
                  kind((1, NSA_DIM, NSA_DIM)), kind((1, 1, NSA_DIM))],
        out_specs=pl.BlockSpec((1, 1, 1, n_sub, NSA_DIM), lambda k, b, gg: (k, b, gg, 0, 0)),
        out_shape=jax.ShapeDtypeStruct((2, bsz, g, n_sub, NSA_DIM), BF16),
        compiler_params=_cparams(("arbitrary", "arbitrary", "arbitrary")),
        name="nsa_compress",
    )(xc, pe, w1, b1, w2, b2)


def _nsa_cmp_kernel(q_ref, kc_ref, vc_ref, sm_ref, ovl_ref, o_ref, sel_ref, *, tq, qpos0, n_pick, nc):
    i = pl.program_id(1)
    q = q_ref[0]
    gates = _sigmoid(sm_ref[0])
    n_pad = kc_ref.shape[3]
    nb = ovl_ref.shape[0]
    qpos_c = qpos0 + i * tq + lax.broadcasted_iota(jnp.int32, (tq, 1), 0)
    n_i = lax.broadcasted_iota(jnp.int32, (1, n_pad), 1)
    valid = (n_i * CMP_STRIDE + (CMP_LEN - 1) <= qpos_c) & (n_i < nc)
    blk = lax.broadcasted_iota(jnp.int32, (nb, tq), 0)
    qpos_r = qpos0 + i * tq + lax.broadcasted_iota(jnp.int32, (nb, tq), 1)
    cur = qpos_r // SEL_BLOCK
    avail = blk * SEL_BLOCK <= qpos_r
    forced = (blk == 0) | (blk == cur) | (blk == cur - 1)
    outs = []
    for g in range(NSA_KV):
        kc = kc_ref[0, 0, g]
        vc = vc_ref[0, 0, g]
        psum = jnp.zeros((tq, n_pad), F32)
        for r in range(NSA_GROUP):
            hd = g * NSA_GROUP + r
            s = _dot_nt(q[:, NSA_DIM * hd:NSA_DIM * (hd + 1)], kc)
            s = jnp.where(valid, s, NEG)
            m = jnp.max(s, axis=1, keepdims=True)
            p = jnp.where(valid, jnp.exp2(s - m), 0.0)
            d = jnp.sum(p, axis=1, keepdims=True)
            p = p / jnp.where(d > 0, d, 1.0)
            psum = psum + p
            outs.append(_dot(p.astype(BF16), vc) * gates[:, 3 * hd:3 * hd + 1])
        imp = sum(_dot_nt(ovl_ref[...], part) for part in _split3(psum))
        val = jnp.where(avail, jnp.where(forced, 1e9, imp), -1.0)

        def pick(_, carry):
            val, sel = carry
            mx = jnp.max(val, axis=0, keepdims=True)
            first = jnp.min(jnp.where(val == mx, blk, nb), axis=0, keepdims=True)
            hit = blk == first
            return jnp.where(hit, -2.0, val), jnp.where(hit, 1.0, sel)

        _, sel = lax.fori_loop(0, n_pick, pick, (val, jnp.zeros((nb, tq), F32)))
        sel_ref[0, g] = ((sel - 1.0) * SEL_BIG).T.astype(BF16)
    o_ref[0] = jnp.concatenate(outs, axis=1)


def _nsa_cmp(qn, kvc, sm, ovl_t, tq, qpos0, n_pick, nc):
    bsz, t, _ = qn.shape
    n_pad = kvc.shape[3]
    nb = ovl_t.shape[0]
    return pl.pallas_call(
        functools.partial(_nsa_cmp_kernel, tq=tq, qpos0=qpos0, n_pick=n_pick, nc=nc),
        grid=(bsz, t // tq),
        in_specs=[pl.BlockSpec((1, tq, NSA_WIDTH), lambda b, i: (b, i, 0)),
                  pl.BlockSpec((1, 1, NSA_KV, n_pad, NSA_DIM), lambda b, i: (0, b, 0, 0, 0)),
                  pl.BlockSpec((1, 1, NSA_KV, n_pad, NSA_DIM), lambda b, i: (1, b, 0, 0, 0)),
                  pl.BlockSpec((1, tq, 128), lambda b, i: (b, i, 0)),
                  pl.BlockSpec((nb, n_pad), lambda b, i: (0, 0))],
        out_specs=[pl.BlockSpec((1, tq, NSA_WIDTH), lambda b, i: (b, i, 0)),
                   pl.BlockSpec((1, NSA_KV, tq, nb), lambda b, i: (b, 0, i, 0))],
        out_shape=[jax.ShapeDtypeStruct((bsz, t, NSA_WIDTH), F32),
                   jax.ShapeDtypeStruct((bsz, NSA_KV, t, nb), BF16)],
        compiler_params=_cparams(("arbitrary", "arbitrary")),
        name="nsa_cmp_select",
    )(qn, kvc, kvc, sm, ovl_t)


def _flash_t(k_at, v_at, qas, lo, n_full, hi, mask_at, m_scr, l_scr, acc_scr, p_scr, tk):
    w = qas[0].shape[0]
    cw = min(w, 128)
    m_scr[...] = jnp.full(m_scr.shape, -3e38, F32)
    l_scr[...] = jnp.zeros(l_scr.shape, F32)
    acc_scr[...] = jnp.zeros(acc_scr.shape, F32)

    def make_body(masked):
        def body(j, carry):
            for g in range(NSA_KV):
                s_all = _dot_nt(k_at(g, j), qas[g])
                for c in range(w // cw):
                    cs = slice(c * cw, (c + 1) * cw)
                    s = s_all[:, cs]
                    if masked:
                        s = jnp.where(mask_at(j, cs), s, NEG)
                    m_prev = m_scr[g, :, cs]
                    m_new = jnp.maximum(m_prev, jnp.max(s, axis=0, keepdims=True))
                    alpha = jnp.exp2(m_prev - m_new)
                    p = jnp.exp2(s - m_new)
                    l_scr[g, :, cs] = alpha * l_scr[g, :, cs] + jnp.sum(p, axis=0, keepdims=True)
                    m_scr[g, :, cs] = m_new
                    p_scr[g, 0:tk, cs] = p.astype(BF16)
                    acc_scr[g, :, cs] = acc_scr[g, :, cs] * alpha
                acc_scr[g] += _dot(v_at(g, j), p_scr[g, 0:tk, :])
            return carry
        return body

    lax.fori_loop(lo, n_full, make_body(False), 0)
    lax.fori_loop(n_full, hi, make_body(True), 0)
    return [acc_scr[g] / l_scr[g] for g in range(NSA_KV)]


def _nsa_sw_kernel(q_ref, ka_ref, vst_ref, kw_ref, vwt_ref, sel_ref, gt_ref, o_ref,
                   qa_scr, m_scr, l_scr, acc_scr, p_scr, *, tq, tk, tkw, qpos0, wpos0, transpose_out):
    i = pl.program_id(1)
    w = NSA_GROUP * tq
    nb = sel_ref.shape[3]
    q = q_ref[0]
    gates_t = _sigmoid(gt_ref[0])
    q_first = qpos0 + i * tq
    q_last = q_first + tq - 1
    lane = lax.broadcasted_iota(jnp.int32, (1, w), 1)
    qpos = q_first + (lane & (tq - 1))
    qa_scr[:, :, nb + NSA_DIM:] = jnp.zeros((NSA_KV, w, qa_scr.shape[2] - nb - NSA_DIM), BF16)
    for g in range(NSA_KV):
        for r in range(NSA_GROUP):
            hd = g * NSA_GROUP + r
            rows = slice(r * tq, (r + 1) * tq)
            qa_scr[g, rows, 0:nb] = sel_ref[0, g]
            qa_scr[g, rows, nb:nb + NSA_DIM] = q[:, NSA_DIM * hd:NSA_DIM * (hd + 1)]
    qas = [qa_scr[g] for g in range(NSA_KV)]
    q4s = [qa[:, nb:nb + NSA_DIM] for qa in qas]

    def rows_at(ref, tile, g, j):
        return ref[0, g, pl.ds(pl.multiple_of(j * tile, tile), tile), :]

    def cols_at(ref, tile, g, j):
        return ref[0, g, :, pl.ds(pl.multiple_of(j * tile, tile), tile)]

    def slc_mask(j, cs):
        return j * tk + lax.broadcasted_iota(jnp.int32, (tk, 1), 0) <= qpos[:, cs]

    def win_mask(j, cs):
        kpos = wpos0 + j * tkw + lax.broadcasted_iota(jnp.int32, (tkw, 1), 0)
        return (kpos <= qpos[:, cs]) & (kpos >= qpos[:, cs] - WINDOW)

    o_s = _flash_t(functools.partial(rows_at, ka_ref, tk), functools.partial(cols_at, vst_ref, tk), qas,
                   0, (q_first + 1) // tk, q_last // tk + 1, slc_mask, m_scr, l_scr, acc_scr, p_scr, tk)
    w_lo = jnp.maximum(q_first - WINDOW - wpos0, 0) // tkw
    o_w = _flash_t(functools.partial(rows_at, kw_ref, tkw), functools.partial(cols_at, vwt_ref, tkw), q4s,
                   w_lo, w_lo, (q_last - wpos0) // tkw + 1, win_mask, m_scr, l_scr, acc_scr, p_scr, tkw)
    outs = []
    for hd in range(NSA_HEADS):
        g, r = divmod(hd, NSA_GROUP)
        cols = slice(r * tq, (r + 1) * tq)
        o_h = (o_s[g][:, cols] * gates_t[3 * hd + 1:3 * hd + 2, :]
               + o_w[g][:, cols] * gates_t[3 * hd + 2:3 * hd + 3, :])
        if transpose_out:
            outs.append(o_h.T)
        else:
            o_ref[0, NSA_DIM * hd:NSA_DIM * (hd + 1), :] = o_h
    if transpose_out:
        o_ref[0] = jnp.concatenate(outs, axis=1)


def _nsa_sw(qn, ka, vst, kw, vwt, selb, gt, tq, tk, tkw, qpos0, wpos0, transpose_out):
    bsz, t, _ = qn.shape
    nb = selb.shape[3]
    assert tq & (tq - 1) == 0
    w = NSA_GROUP * tq
    kvspec = lambda a: pl.BlockSpec((1,) + a.shape[1:], lambda b, i: (b, 0, 0, 0))
    if transpose_out:
        out_spec = pl.BlockSpec((1, tq, NSA_WIDTH), lambda b, i: (b, i, 0))
        out_shape = jax.ShapeDtypeStruct((bsz, t, NSA_WIDTH), F32)
    else:
        out_spec = pl.BlockSpec((1, NSA_WIDTH, tq), lambda b, i: (b, 0, i))
        out_shape = jax.ShapeDtypeStruct((bsz, NSA_WIDTH, t), F32)
    return pl.pallas_call(
        functools.partial(_nsa_sw_kernel, tq=tq, tk=tk, tkw=tkw, qpos0=qpos0, wpos0=wpos0,
                          transpose_out=transpose_out),
        grid=(bsz, t // tq),
        in_specs=[pl.BlockSpec((1, tq, NSA_WIDTH), lambda b, i: (b, i, 0)),
                  kvspec(ka), kvspec(vst), kvspec(kw), kvspec(vwt),
                  pl.BlockSpec((1, NSA_KV, tq, nb), lambda b, i: (b, 0, i, 0)),
                  pl.BlockSpec((1, 32, tq), lambda b, i: (b, 0, i))],
        out_specs=out_spec,
        out_shape=out_shape,
        scratch_shapes=[pltpu.VMEM((NSA_KV, w, ka.shape[3]), BF16),
                        pltpu.VMEM((NSA_KV, 1, w), F32),
                        pltpu.VMEM((NSA_KV, 1, w), F32),
                        pltpu.VMEM((NSA_KV, NSA_DIM, w), F32),
                        pltpu.VMEM((NSA_KV, max(tk, tkw), w), BF16)],
        compiler_params=_cparams(("arbitrary", "arbitrary")),
        name="nsa_select_window",
    )(qn, ka, vst, kw, vwt, selb, gt)


def _ab_out_kernel(x_ref, mod_ref, hm_ref, oc_ref, osw_ref, w_ref, o_ref):
    o_nsa = (oc_ref[...] + osw_ref[...]).astype(BF16)
    y = _dot(hm_ref[...], w_ref[:ML_WIDTH, :]) + _dot(o_nsa, w_ref[ML_WIDTH:, :])
    o_ref[...] = x_ref[...] + mod_ref[0, 2] * y


def _ab_out(x, mod4, tiles_per_seq, hm, o_cmp, o_sw, w, tm):
    n = x.shape[0]
    tok = lambda wd: pl.BlockSpec((tm, wd), lambda i: (i, 0))
    return pl.pallas_call(
        _ab_out_kernel,
        grid=(n // tm,),
        in_specs=[tok(D_MODEL), _mod_spec(mod4, tm, tiles_per_seq), tok(ML_WIDTH), tok(NSA_WIDTH),
                  tok(NSA_WIDTH), pl.BlockSpec(w.shape, lambda i: (0, 0))],
        out_specs=tok(D_MODEL),
        out_shape=jax.ShapeDtypeStruct((n, D_MODEL), F32),
        compiler_params=_cparams(("arbitrary",)),
        name="ab_out",
    )(x, mod4, hm, o_cmp, o_sw, w)


def _cl_kernel(x_ref, mod_ref, g_ref, wi_ref, bi_ref, vg_ref, ws_ref, bs_ref, wo_ref, o_ref, v_ref, *, tm):
    x = x_ref[...]
    h = _normmod(x, g_ref[...], mod_ref[0, 1], mod_ref[0, 0]).astype(BF16)
    z = _gelu(_dot(h, wi_ref[...]) + bi_ref[...])
    gw = z.shape[1] // 2
    u = z[:, :gw]
    v = z[:, gw:]
    v = v * lax.rsqrt(jnp.mean(v * v, axis=-1, keepdims=True) + EPS) * vg_ref[...]
    v_ref[...] = v
    vb = v.astype(BF16)
    lch = ws_ref.shape[1]
    tril = lax.broadcasted_iota(jnp.int32, (lch, lch), 0) >= lax.broadcasted_iota(jnp.int32, (lch, lch), 1)
    gd = gw // GM_GROUPS
    rows = []
    for c in range(tm // lch):
        cols = []
        for g in range(GM_GROUPS):
            wsg = jnp.where(tril, ws_ref[g], 0.0).astype(BF16)
            cols.append(_dot(wsg, vb[c * lch:(c + 1) * lch, g * gd:(g + 1) * gd]) + bs_ref[:, g:g + 1])
        rows.append(jnp.concatenate(cols, axis=1))
    s = jnp.concatenate(rows, axis=0) if len(rows) > 1 else rows[0]
    y = _dot((u * s).astype(BF16), wo_ref[...])
    o_ref[...] = x + mod_ref[0, 2] * y


def _cl_mixer(x, mod4, tiles_per_seq, g, wi, bi, vg, ws_eff, bs_eff, wo, tm):
    n = x.shape[0]
    gw = wo.shape[0]
    tok = lambda wd: pl.BlockSpec((tm, wd), lambda i: (i, 0))
    full = lambda a: pl.BlockSpec(a.shape, lambda i: (0,) * a.ndim)
    return pl.pallas_call(
        functools.partial(_cl_kernel, tm=tm),
        grid=(n // tm,),
        in_specs=[tok(D_MODEL), _mod_spec(mod4, tm, tiles_per_seq), full(g), full(wi), full(bi), full(vg),
                  full(ws_eff), full(bs_eff), full(wo)],
        out_specs=[tok(D_MODEL), tok(gw)],
        out_shape=[jax.ShapeDtypeStruct((n, D_MODEL), F32), jax.ShapeDtypeStruct((n, gw), F32)],
        compiler_params=_cparams(("arbitrary",)),
        name="gmlp_mixer",
    )(x, mod4, g, wi, bi, vg, ws_eff, bs_eff, wo)


def _ffn_kernel(x_ref, mod_ref, g_ref, w1_ref, w2_ref, fg_ref, o_ref, h_scr, acc_scr, *, final):
    j = pl.program_id(1)

    @pl.when(j == 0)
    def _():
        h_scr[...] = _normmod(x_ref[...], g_ref[...], mod_ref[0, 4], mod_ref[0, 3]).astype(BF16)
        acc_scr[...] = jnp.zeros(acc_scr.shape, F32)

    a = jnp.maximum(_dot(h_scr[...], w1_ref[...]), 0.0)
    acc_scr[...] += _dot((a * a).astype(BF16), w2_ref[...])

    @pl.when(j == pl.num_programs(1) - 1)
    def _():
        y = x_ref[...] + mod_ref[0, 5] * acc_scr[...]
        if final:
            y = y * lax.rsqrt(jnp.mean(y * y, axis=-1, keepdims=True) + EPS) * fg_ref[...]
        o_ref[...] = y


def _ffn(x, mod4, tiles_per_seq, g, w1, w2, final_g, final, tm, tf):
    n = x.shape[0]
    f = w1.shape[1]
    r = mod4.shape[2]
    if r == 1:
        mspec = pl.BlockSpec((1, 6, 1, D_MODEL), lambda i, j: (i // tiles_per_seq, 0, 0, 0))
    else:
        mspec = pl.BlockSpec((1, 6, tm, D_MODEL), lambda i, j: (i, 0, 0, 0))
    return pl.pallas_call(
        functools.partial(_ffn_kernel, final=final),
        grid=(n // tm, f // tf),
        in_specs=[pl.BlockSpec((tm, D_MODEL), lambda i, j: (i, 0)), mspec,
                  pl.BlockSpec((1, D_MODEL), lambda i, j: (0, 0)),
                  pl.BlockSpec((D_MODEL, tf), lambda i, j: (0, j)),
                  pl.BlockSpec((tf, D_MODEL), lambda i, j: (j, 0)),
                  pl.BlockSpec((1, D_MODEL), lambda i, j: (0, 0))],
        out_specs=pl.BlockSpec((tm, D_MODEL), lambda i, j: (i, 0)),
        out_shape=jax.ShapeDtypeStruct((n, D_MODEL), F32),
        scratch_shapes=[pltpu.VMEM((tm, D_MODEL), BF16), pltpu.VMEM((tm, D_MODEL), F32)],
        compiler_params=_cparams(("arbitrary", "arbitrary")),
        name="ffn",
    )(x, mod4, g, w1, w2, final_g)


def _gather_kernel(pt_ref, *refs):
    n_in = len(refs) - 1
    o_ref = refs[-1]
    for p in range(n_in):
        o_ref[0, p * PAGE_SIZE:(p + 1) * PAGE_SIZE, :] = refs[p][0]


def _gather_pages(cache, page_table, pages_per_step=8):
    _, ps, wd = cache.shape
    bsz, n_pages = page_table.shape
    steps = n_pages // pages_per_step

    def in_map(p):
        return lambda b, s, pt: (pt[b, s * pages_per_step + p], 0, 0)

    return pl.pallas_call(
        _gather_kernel,
        grid_spec=pltpu.PrefetchScalarGridSpec(
            num_scalar_prefetch=1,
            grid=(bsz, steps),
            in_specs=[pl.BlockSpec((1, ps, wd), in_map(p)) for p in range(pages_per_step)],
            out_specs=pl.BlockSpec((1, pages_per_step * ps, wd), lambda b, s, pt: (b, s, 0))),
        out_shape=jax.ShapeDtypeStruct((bsz, n_pages * ps, wd), F32),
        compiler_params=_cparams(("arbitrary", "arbitrary")),
        name="page_gather",
    )(page_table, *([cache] * pages_per_step))


def _selection_constants(n_keys, n_sub, tk_total):
    nb = n_keys // SEL_BLOCK
    i = np.arange(n_sub)[None, :] * CMP_STRIDE
    j = np.arange(nb)[:, None] * SEL_BLOCK
    ovl_t = ((i < j + SEL_BLOCK) & (i + CMP_LEN > j) & (np.arange(n_sub)[None, :] < n_sub - 1))
    pos = np.arange(tk_total)[None, :]
    et = (pos // SEL_BLOCK) == np.arange(nb)[:, None]
    return jnp.asarray(ovl_t, BF16), jnp.asarray(et, BF16)


def _split_kv(kvb, bsz, t):
    return kvb.reshape(bsz, t, 6, NSA_KV, NSA_DIM).transpose(2, 0, 3, 1, 4)


def _cmp_input(rows_f32, bsz, t):
    n_sub = t // CMP_STRIDE
    x = rows_f32[:, :n_sub * CMP_STRIDE].transpose(2, 0, 3, 1, 4)
    return x.reshape(2, bsz, NSA_KV, n_sub, CMP_STRIDE * NSA_DIM)


def _ab_layer(x, mod4, tps, W, l, bsz, t, tm, st, page_table, lc, tq, tk, tkw):
    n = bsz * t
    j = l // 2
    qk, v_m, o_pre, qn, kvf, kvb, sm = _ab_in(x, mod4, tps, W['norm_g0'][l], W['ab_w_in'][j], W['ab_b_in'][j], tm)
    qk3 = qk.reshape(bsz, t, 2 * ML_WIDTH)
    sm3 = sm.reshape(bsz, t, 128)
    gr = sm3[:, :, _SM_I:_SM_I + 8].transpose(0, 2, 1)
    v3 = v_m.reshape(bsz, t, ML_WIDTH)
    o3 = o_pre.reshape(bsz, t, ML_WIDTH)
    if st is None:
        conv0 = jnp.zeros((bsz, CONV_W - 1, 2 * ML_WIDTH), F32)
        c0 = jnp.zeros((bsz, ML_HEADS, ML_DIM, ML_DIM), F32)
        n0 = jnp.zeros((bsz, ML_HEADS, ML_DIM), F32)
        m0 = jnp.zeros((bsz, ML_HEADS), F32)
        tp = t
    else:
        conv0, c0, n0, m0, cache, win_buf = st
        tp = lc
        pad = tp - t
        qk3 = jnp.pad(qk3, ((0, 0), (0, pad), (0, 0)))
        v3 = jnp.pad(v3, ((0, 0), (0, pad), (0, 0)))
        o3 = jnp.pad(o3, ((0, 0), (0, pad), (0, 0)))
        gr = jnp.concatenate([jnp.pad(gr[:, :4], ((0, 0), (0, 0), (0, pad)), constant_values=NEG),
                              jnp.pad(gr[:, 4:], ((0, 0), (0, 0), (0, pad)), constant_values=-NEG)], axis=1)
        sm_pad = jnp.zeros((bsz, pad, 128), F32).at[:, :, _SM_I:_SM_I + 4].set(NEG).at[:, :, _SM_F:_SM_F + 4].set(-NEG)
        sm3m = jnp.concatenate([sm3, sm_pad], axis=1)
    if st is None:
        sm3m = sm3
    hm, c1, n1, m1 = _mlstm(qk3, v3, o3, sm3m, gr, W['ml_conv_w'][j], W['ml_conv_b'][j], W['ml_f_bias'][j],
                            W['ml_out_g'][j], conv0, c0, n0, m0, lc)
    hm = hm[:, :t].reshape(n, ML_WIDTH)
    conv_full = jnp.concatenate([conv0, qk.reshape(bsz, t, 2 * ML_WIDTH)], axis=1)
    conv_new = conv_full[:, -(CONV_W - 1):]
    kv_rows = kvf[:, :512].reshape(bsz, t, 4, NSA_KV, NSA_DIM)
    win_rows = kvf[:, 512:].reshape(bsz, t, 2, NSA_KV, NSA_DIM)
    parts = _split_kv(kvb, bsz, t)
    if st is None:
        past = 0
        cmp_rows = kv_rows[:, :, :2]
        k_s, v_s, k_w, v_w = parts[2], parts[3], parts[4], parts[5]
        tkeys = t
        wpos0 = 0
        tq_eff = tq
        qn3 = qn.reshape(bsz, t, NSA_WIDTH)
        sm3q = sm3
        win_new = win_rows[:, -min(WINDOW, t):]
        n_pick = N_SEL
    else:
        past = page_table.shape[1] * PAGE_SIZE
        gathered = _gather_pages(cache.reshape(cache.shape[0], PAGE_SIZE, 4 * NSA_KV * NSA_DIM), page_table)
        gathered = gathered.reshape(bsz, past, 4, NSA_KV, NSA_DIM)
        cmp_rows = gathered[:, :, :2]
        assert (past + t) // CMP_STRIDE == past // CMP_STRIDE
        tkeys = past
        kpad = tk
        gb = gathered[:, :, 2:].astype(BF16).transpose(2, 0, 3, 1, 4)
        zpad = jnp.zeros((bsz, NSA_KV, kpad - t, NSA_DIM), BF16)
        k_s = jnp.concatenate([gb[0], parts[2], zpad], axis=2)
        v_s = jnp.concatenate([gb[1], parts[3], zpad], axis=2)
        wb = win_buf.astype(BF16).transpose(2, 0, 3, 1, 4)
        wlen = win_buf.shape[1]
        wpos0 = past - wlen
        k_w = jnp.concatenate([wb[0], parts[4], zpad], axis=2)
        v_w = jnp.concatenate([wb[1], parts[5], zpad], axis=2)
        tq_eff = tq
        qn3 = jnp.pad(qn.reshape(bsz, t, NSA_WIDTH), ((0, 0), (0, tq - t), (0, 0)))
        sm3q = jnp.pad(sm3, ((0, 0), (0, tq - t), (0, 0)))
        win_new = jnp.concatenate([win_buf, win_rows], axis=1)[:, -wlen:]
        n_pick = N_SEL - 1
        assert t <= SEL_BLOCK and past % SEL_BLOCK == 0
    n_sub = tkeys // CMP_STRIDE
    xc = _cmp_input(cmp_rows, bsz, tkeys)
    kvc = _compress(xc, W['phi_pe'][j], W['phi_w1'][j], W['phi_b1'][j], W['phi_w2'][j], W['phi_b2'][j])
    ovl_t, et = _selection_constants(tkeys, n_sub, k_s.shape[2])
    o_cmp, selb = _nsa_cmp(qn3, kvc, sm3q, ovl_t, tq_eff, past, n_pick, n_sub - 1)
    tks = k_s.shape[2]
    nb = et.shape[0]
    ka_w = -(-(nb + NSA_DIM) // 128) * 128
    ka = jnp.concatenate([jnp.broadcast_to(et.T[None, None], (bsz, NSA_KV, tks, nb)), k_s,
                          jnp.zeros((bsz, NSA_KV, tks, ka_w - nb - NSA_DIM), BF16)], axis=3)
    gt = sm3q.transpose(0, 2, 1)
    if st is None:
        o_sw = _nsa_sw(qn3, ka, v_s.transpose(0, 1, 3, 2), k_w, v_w.transpose(0, 1, 3, 2), selb, gt,
                       tq_eff, tk, tkw, past, wpos0, True)
    else:
        tqs = 16
        o_sw = _nsa_sw(qn3[:, :tqs], ka, v_s.transpose(0, 1, 3, 2), k_w, v_w.transpose(0, 1, 3, 2),
                       selb[:, :, :tqs], gt[:, :, :tqs], tqs, tk, tkw, past, wpos0, False).transpose(0, 2, 1)
    o_cmp = o_cmp[:, :t].reshape(n, NSA_WIDTH)
    o_sw = o_sw[:, :t].reshape(n, NSA_WIDTH)
    x = _ab_out(x, mod4, tps, hm, o_cmp, o_sw, W['ab_w_out'][j], tm)
    return x, (conv_new, c1, n1, m1, kv_rows, win_new)


def _trunk(x3, mods, W, state, page_table, cfg):
    bsz, t, _ = x3.shape
    n = bsz * t
    tm = cfg['tm']
    x = x3.reshape(n, D_MODEL)
    depth = mods.shape[0]
    ab_new, cl_new = [], []
    if t % tm == 0:
        tps = t // tm
        to_mod4 = lambda m: m.reshape(bsz, 6, 1, D_MODEL)
    else:
        assert n == tm
        tps = 1
        to_mod4 = lambda m: jnp.repeat(m.reshape(bsz, 6, D_MODEL), t, axis=0).reshape(n, 6, D_MODEL).transpose(1, 0, 2)[None]
    lch = min(GM_CHUNK, t)
    for l in range(depth):
        mod4 = to_mod4(mods[l])
        j = l // 2
        if l % 2 == 0:
            st = None if state is None else tuple(a[j] for a in state)
            x, new = _ab_layer(x, mod4, tps, W, l, bsz, t, tm, st, page_table, cfg['lc'], cfg['tq'], cfg['tk'], cfg['tkw'])
            ab_new.append(new)
        else:
            ws = W['cl_ws'][j][:, :lch, :lch]
            bs = W['cl_bs'][j][:, :lch]
            if lch < GM_CHUNK:
                rep = GM_CHUNK // lch
                ws = jnp.einsum('ab,gts->gatbs', jnp.eye(rep, dtype=F32), ws).reshape(GM_GROUPS, GM_CHUNK, GM_CHUNK)
                bs = jnp.tile(bs, (1, rep))
            x, v = _cl_mixer(x, mod4, tps, W['norm_g0'][l], W['cl_w_in'][j], W['cl_b_in'][j], W['cl_v_g'][j],
                             ws, bs.T, W['cl_w_out'][j], tm)
            cl_new.append(v.reshape(bsz, t, -1))
        x = _ffn(x, mod4, max(t // cfg['tm_ffn'], 1), W['norm_g1'][l], W['ffn_w1'][l], W['ffn_w2'][l], W['final_g'],
                 l == depth - 1, cfg['tm_ffn'], cfg['tf'])
    return x.reshape(bsz, t, D_MODEL), ab_new, cl_new


def kernel(x_prompt, x_sample, c_prompt, c_sample, state_mlstm_conv, state_mlstm_C, state_mlstm_n,
           state_mlstm_m, cache_nsa_kv, state_nsa_win, page_table, ada_w, ada_b, norm_g, ab_w_in, ab_b_in,
           ml_conv_w, ml_conv_b, ml_f_bias, ml_out_g, phi_pe, phi_w1, phi_b1, phi_w2, phi_b2, ab_w_out,
           cl_w_in, cl_b_in, cl_v_g, cl_ws, cl_bs, cl_w_out, ffn_w1, ffn_w2, final_g):
    depth = ada_w.shape[0]
    n_ab = ab_w_in.shape[0]
    bp, bs_ = c_prompt.shape[0], c_sample.shape[0]
    rows = bp + bs_
    rows_pad = -(-rows // 8) * 8
    c_all = jnp.concatenate([c_prompt, c_sample, jnp.zeros((rows_pad - rows, D_MODEL), F32)], axis=0)
    mods = _ada_mod(c_all, ada_w, ada_b)
    mods_p = mods[:, :bp]
    mods_s = mods[:, bp:rows]
    w_in = jnp.concatenate([ab_w_in[:, :, 0:2048], ab_w_in[:, :, 2056:3336], ab_w_in[:, :, 3336:3360],
                            ab_w_in[:, :, 2048:2056], jnp.zeros((n_ab, D_MODEL, 96), F32)], axis=2).astype(BF16)
    b_in = jnp.concatenate([ab_b_in[:, 0:2048], ab_b_in[:, 2056:3336], ab_b_in[:, 3336:3360],
                            ab_b_in[:, 2048:2056], jnp.zeros((n_ab, 96), F32)], axis=1)[:, None, :]
    half = CMP_STRIDE * NSA_DIM
    W = dict(
        norm_g0=norm_g[:, 0][:, None, :], norm_g1=norm_g[:, 1][:, None, :],
        ab_w_in=w_in, ab_b_in=b_in,
        ml_conv_w=ml_conv_w, ml_conv_b=ml_conv_b[:, None, :], ml_f_bias=ml_f_bias, ml_out_g=ml_out_g[:, None, :],
        phi_pe=phi_pe.reshape(n_ab, 2, 2, half),
        phi_w1=jnp.concatenate([phi_w1[:, :, :half], phi_w1[:, :, half:]], axis=3).astype(BF16),
        phi_b1=phi_b1[:, :, None, :], phi_w2=phi_w2.astype(BF16), phi_b2=phi_b2[:, :, None, :],
        ab_w_out=ab_w_out.astype(BF16),
        cl_w_in=cl_w_in.astype(BF16), cl_b_in=cl_b_in[:, None, :], cl_v_g=cl_v_g[:, None, :],
        cl_ws=cl_ws, cl_bs=cl_bs, cl_w_out=cl_w_out.astype(BF16),
        ffn_w1=ffn_w1.astype(BF16), ffn_w2=ffn_w2.astype(BF16), final_g=final_g[None, :])
    cfg_p = dict(tm=256, tm_ffn=512, tf=512, lc=256, tq=128, tk=512, tkw=256)
    y_prompt, ab_p, _ = _trunk(x_prompt, mods_p, W, None, None, cfg_p)
    n_s = x_sample.shape[0] * x_sample.shape[1]
    cfg_s = dict(tm=n_s, tm_ffn=n_s, tf=512, lc=128, tq=128, tk=512, tkw=256)
    state = (state_mlstm_conv, state_mlstm_C, state_mlstm_n, state_mlstm_m, cache_nsa_kv, state_nsa_win)
    y_sample, ab_s, cl_s = _trunk(x_sample, mods_s, W, state, page_table, cfg_s)
    p_out = [jnp.stack(a) for a in zip(*ab_p)]
    s_out = [jnp.stack(a) for a in zip(*ab_s)]
    return (y_prompt, y_sample, *p_out, *s_out, jnp.stack(cl_s))
```

```python
import functools

import numpy as np
import jax
import jax.numpy as jnp
from jax import lax
from jax.experimental import pallas as pl
from jax.experimental.pallas import tpu as pltpu

F32 = jnp.float32
BF16 = jnp.bfloat16

EPS = 1e-6
D_MODEL = 1024
ML_HEADS = 4
ML_DIM = 128
ML_WIDTH = ML_HEADS * ML_DIM
CONV_W = 4
NSA_HEADS = 8
NSA_KV = 2
NSA_GROUP = NSA_HEADS // NSA_KV
NSA_DIM = 64
NSA_WIDTH = NSA_HEADS * NSA_DIM
CMP_LEN = 32
CMP_STRIDE = 16
SEL_BLOCK = 64
N_SEL = 16
WINDOW = 512
GM_GROUPS = 4
GM_CHUNK = 128
PAGE_SIZE = 128

_SEG_QK = (0, 1024)
_SEG_V = (1024, 1536)
_SEG_O = (1536, 2048)
_SEG_QN = (2048, 2560)
_SEG_KV = (2560, 3328)
_SEG_SM = (3328, 3456)
_AB_COLS = 3456
_SM_I = 24
_SM_F = 28

NEG = -1e30
SEL_BIG = 2.0 ** 100
LOG2E = 1.4426950408889634
VMEM_LIMIT = 56 * 1024 * 1024


def _cparams(sem):
    return pltpu.CompilerParams(dimension_semantics=sem, vmem_limit_bytes=VMEM_LIMIT)


def _sigmoid(x):
    return 1.0 / (1.0 + jnp.exp(-x))


def _log_sigmoid(x):
    return jnp.minimum(x, 0.0) - jnp.log1p(jnp.exp(-jnp.abs(x)))


def _gelu(x):
    return 0.5 * x * (1.0 + jnp.tanh(0.7978845608028654 * (x + 0.044715 * (x * x * x))))


def _normmod(x, g, scale, shift):
    y = x * lax.rsqrt(jnp.mean(x * x, axis=-1, keepdims=True) + EPS) * g
    return y * (1.0 + scale) + shift


def _dot(a, b):
    return jnp.dot(a, b, preferred_element_type=F32)


def _dot_nt(a, b):
    return lax.dot_general(a, b, (((1,), (1,)), ((), ())), preferred_element_type=F32)


def _split3(x):
    hi = x.astype(BF16)
    r = x - hi.astype(F32)
    mid = r.astype(BF16)
    lo = (r - mid.astype(F32)).astype(BF16)
    return hi, mid, lo


def _ada_kernel(c_ref, w_ref, b_ref, o_ref):
    c = c_ref[...]
    sc = c * _sigmoid(c)
    sc_hi = sc.astype(BF16)
    sc_lo = (sc - sc_hi.astype(F32)).astype(BF16)
    w = w_ref[0]
    w_hi = w.astype(BF16)
    w_lo = (w - w_hi.astype(F32)).astype(BF16)
    acc = _dot(sc_hi, w_hi) + _dot(sc_lo, w_hi) + _dot(sc_hi, w_lo)
    o_ref[0] = acc + b_ref[0]


def _ada_mod(c_all, ada_w, ada_b):
    depth, d, n = ada_w.shape
    bp = c_all.shape[0]
    tn = 1536
    return pl.pallas_call(
        _ada_kernel,
        grid=(depth, n // tn),
        in_specs=[pl.BlockSpec((bp, d), lambda l, j: (0, 0)),
                  pl.BlockSpec((1, d, tn), lambda l, j: (l, 0, j)),
                  pl.BlockSpec((1, 1, tn), lambda l, j: (l, 0, j))],
        out_specs=pl.BlockSpec((1, bp, tn), lambda l, j: (l, 0, j)),
        out_shape=jax.ShapeDtypeStruct((depth, bp, n), F32),
        compiler_params=_cparams(("arbitrary", "arbitrary")),
        name="ada_mod",
    )(c_all, ada_w, ada_b.reshape(depth, 1, n))


def _mod_spec(mod4, tm, tiles_per_seq):
    r = mod4.shape[2]
    if r == 1:
        return pl.BlockSpec((1, 6, 1, D_MODEL), lambda i: (i // tiles_per_seq, 0, 0, 0))
    assert r == tm
    return pl.BlockSpec((1, 6, tm, D_MODEL), lambda i: (i, 0, 0, 0))


def _ab_in_kernel(x_ref, mod_ref, g_ref, w_ref, b_ref, *rest, tm, tiles_per_seq, nb, seq_layouts):
    h = _normmod(x_ref[...], g_ref[...], mod_ref[0, 1], mod_ref[0, 0]).astype(BF16)

    def seg(ab):
        a, b = ab
        return _dot(h, w_ref[:, a:b]) + b_ref[:, a:b]

    if seq_layouts:
        wt_ref, bt_ref, qk_ref, v_ref, o_ref, qn_ref, kvf_ref, sm_ref, ka_ref, kw_ref, vt_ref, smt_ref = rest
    else:
        qk_ref, v_ref, o_ref, qn_ref, kvf_ref, sm_ref, kvb_ref = rest
    qk_ref[...] = seg(_SEG_QK)
    v_ref[...] = seg(_SEG_V).astype(BF16)
    o_ref[...] = seg(_SEG_O)
    qn_ref[...] = (seg(_SEG_QN) * (NSA_DIM ** -0.5 * LOG2E)).astype(BF16)
    kv = seg(_SEG_KV)
    kvf_ref[...] = kv
    sm_ref[...] = seg(_SEG_SM)
    if not seq_layouts:
        kvb_ref[...] = kv.astype(BF16)
        return
    pos = (pl.program_id(0) % tiles_per_seq) * tm + lax.broadcasted_iota(jnp.int32, (tm, nb), 0)
    onehot = jnp.where(pos // SEL_BLOCK == lax.broadcasted_iota(jnp.int32, (tm, nb), 1), 1.0, 0.0).astype(BF16)
    ka_w = ka_ref.shape[1] // NSA_KV
    pieces = []
    for g in range(NSA_KV):
        pieces += [onehot, kv[:, 256 + NSA_DIM * g:256 + NSA_DIM * (g + 1)].astype(BF16)]
        if ka_w > nb + NSA_DIM:
            pieces.append(jnp.zeros((tm, ka_w - nb - NSA_DIM), BF16))
    ka_ref[...] = jnp.concatenate(pieces, axis=1)
    kw_ref[...] = kv[:, 512:640].astype(BF16)
    zt = _dot_nt(wt_ref[...], h) + bt_ref[...]
    vt_ref[0] = zt[:256].astype(BF16)
    smt_ref[0] = zt[256:]


def _ab_in(x, mod4, tiles_per_seq, g, w, b, tm, seq_shape=None):
    n = x.shape[0]
    widths = [(1024, F32), (512, BF16), (512, F32), (512, BF16), (768, F32), (128, F32)]
    ins = [x, mod4, g, w, b]
    in_specs = [pl.BlockSpec((tm, D_MODEL), lambda i: (i, 0)),
                _mod_spec(mod4, tm, tiles_per_seq),
                pl.BlockSpec((1, D_MODEL), lambda i: (0, 0)),
                pl.BlockSpec((D_MODEL, _AB_COLS), lambda i: (0, 0)),
                pl.BlockSpec((1, _AB_COLS), lambda i: (0, 0))]
    nb = 0
    if seq_shape is None:
        widths.append((768, BF16))
    else:
        bsz, t = seq_shape
        nb = t // SEL_BLOCK
        ka_w = -(-(nb + NSA_DIM) // 128) * 128
        widths += [(NSA_KV * ka_w, BF16), (128, BF16)]
        a, c = _SEG_KV[0], _SEG_SM[0]
        cols = jnp.concatenate([w[:, a + 384:a + 512], w[:, a + 640:a + 768], w[:, c:c + 128]], axis=1)
        bcols = jnp.concatenate([b[:, a + 384:a + 512], b[:, a + 640:a + 768], b[:, c:c + 128]], axis=1)
        ins += [cols.T, bcols.T]
        in_specs += [pl.BlockSpec((384, D_MODEL), lambda i: (0, 0)), pl.BlockSpec((384, 1), lambda i: (0, 0))]
    out_specs = [pl.BlockSpec((tm, wd), lambda i: (i, 0)) for wd, _ in widths]
    out_shape = [jax.ShapeDtypeStruct((n, wd), dt) for wd, dt in widths]
    if seq_shape is not None:
        seq_map = lambda i: (i // tiles_per_seq, 0, i % tiles_per_seq)
        out_specs += [pl.BlockSpec((1, 256, tm), seq_map), pl.BlockSpec((1, 128, tm), seq_map)]
        out_shape += [jax.ShapeDtypeStruct((bsz, 256, t), BF16), jax.ShapeDtypeStruct((bsz, 128, t), F32)]
    return pl.pallas_call(
        functools.partial(_ab_in_kernel, tm=tm, tiles_per_seq=tiles_per_seq, nb=nb,
                          seq_layouts=seq_shape is not None),
        grid=(n // tm,),
        in_specs=in_specs,
        out_specs=out_specs,
        out_shape=out_shape,
        compiler_params=_cparams(("arbitrary",)),
        name="ab_in",
    )(*ins)


def _mlstm_kernel(qk_ref, v_ref, o_ref, sm_ref, gr_ref, cw_ref, cb_ref, fbc_ref, fbr_ref, og_ref,
                  conv0_ref, c0_ref, n0_ref, m0_ref,
                  hm_ref, cout_ref, nout_ref, mout_ref,
                  xp_scr, c_scr, n_scr, m_scr, *, lc):
    c = pl.program_id(1)

    @pl.when(c == 0)
    def _():
        xp_scr[0:8, :] = conv0_ref[0]
        c_scr[...] = c0_ref[0]
        n_scr[...] = n0_ref[0]
        m_scr[...] = m0_ref[0]

    xp_scr[8:8 + lc, :] = qk_ref[0]
    y = cb_ref[...]
    for j in range(CONV_W):
        y = y + xp_scr[5 + j:5 + j + lc, :] * cw_ref[j:j + 1, :]
    xp_scr[0:8, :] = xp_scr[lc:lc + 8, :]
    act = y * _sigmoid(y)
    q_all = act[:, :ML_WIDTH]
    k_all = act[:, ML_WIDTH:] * (ML_DIM ** -0.5)

    sm = sm_ref[0]
    gr = gr_ref[0]
    lf_cols = _log_sigmoid(sm + fbc_ref[...])
    lf_rows = _log_sigmoid(gr + fbr_ref[...])
    v_all = v_ref[0]
    o_all = o_ref[0]

    row_i = lax.broadcasted_iota(jnp.int32, (lc, lc), 0)
    col_i = lax.broadcasted_iota(jnp.int32, (lc, lc), 1)
    tril = row_i >= col_i
    triu = row_i <= col_i

    outs = []
    for h in range(ML_HEADS):
        hs = slice(ML_DIM * h, ML_DIM * (h + 1))
        li_c = sm[:, _SM_I + h:_SM_I + h + 1]
        lf_c = lf_cols[:, _SM_F + h:_SM_F + h + 1]
        li_r = gr[h:h + 1, :]
        lf_r = lf_rows[4 + h:5 + h, :]
        b_c = jnp.sum(jnp.where(tril, lf_r, 0.0), axis=1, keepdims=True)
        b_r = jnp.sum(jnp.where(triu, lf_c, 0.0), axis=0, keepdims=True)
        a_r = li_r - b_r
        a_c = li_c - b_c
        m_h = m_scr[h:h + 1, 0:1]
        cm_c = jnp.maximum(m_h, jnp.max(jnp.where(tril, a_r, NEG), axis=1, keepdims=True))
        dm = jnp.exp(jnp.where(tril, a_r - cm_c, NEG))
        w_int = jnp.exp(m_h - cm_c)
        qh = q_all[:, hs]
        kh = k_all[:, hs]
        vh = v_all[:, hs]
        qb = qh.astype(BF16)
        kb = kh.astype(BF16)
        s = _dot_nt(qb, kb) * dm
        c_old = c_scr[h]
        n_old = n_scr[h:h + 1, :]
        num = w_int * _dot(qb, c_old.astype(BF16)) + _dot(s.astype(BF16), vh)
        den = w_int * jnp.sum(qh * n_old, axis=1, keepdims=True) + jnp.sum(s, axis=1, keepdims=True)
        mt = b_c + cm_c
        hh = num / jnp.maximum(jnp.abs(den), jnp.exp(-mt))
        hn = hh * lax.rsqrt(jnp.mean(hh * hh, axis=1, keepdims=True) + EPS) * og_ref[:, hs]
        outs.append(hn * _sigmoid(o_all[:, hs]))
        cm_last = jnp.maximum(m_h, jnp.max(a_r, axis=1, keepdims=True))
        bl = jnp.sum(lf_r, axis=1, keepdims=True)
        decay = jnp.exp(m_h - cm_last)
        ws_c = jnp.exp(a_c - cm_last)
        kt = kh.T.astype(BF16)
        c_scr[h] = decay * c_old + _dot(kt, (ws_c * vh.astype(F32)).astype(BF16))
        n_scr[h:h + 1, :] = decay * n_old + jnp.sum(ws_c * kh, axis=0, keepdims=True)
        m_scr[h:h + 1, :] = jnp.broadcast_to(bl + cm_last, (1, ML_DIM))

    hm_ref[0] = jnp.concatenate(outs, axis=1).astype(BF16)

    @pl.when(c == pl.num_programs(1) - 1)
    def _():
        cout_ref[0] = c_scr[...]
        nout_ref[0] = n_scr[...]
        mout_ref[0] = m_scr[...]


def _mlstm(qk, v, o, sm, gr, conv_w, conv_b, f_bias, out_g, conv0, c0, n0, m0, lc, gr_block=0):
    bsz, t, _ = qk.shape
    nc = t // lc
    fbc = jnp.zeros((1, 128), F32).at[0, _SM_F:_SM_F + ML_HEADS].set(f_bias)
    fbr = jnp.zeros((8, 1), F32).at[4:8, 0].set(f_bias)
    conv0p = jnp.concatenate([jnp.zeros((bsz, 5, 2 * ML_WIDTH), F32), conv0], axis=1)
    m0p = jnp.broadcast_to(m0[:, :, None], (bsz, ML_HEADS, ML_DIM))
    tok = lambda wd: pl.BlockSpec((1, lc, wd), lambda b, c: (b, c, 0))
    full2 = lambda a: pl.BlockSpec(a.shape, lambda b, c: (0, 0))
    st3 = lambda a: pl.BlockSpec((1,) + a.shape[1:], lambda b, c: (b,) + (0,) * (a.ndim - 1))
    hm, c1, n1, m1 = pl.pallas_call(
        functools.partial(_mlstm_kernel, lc=lc),
        grid=(bsz, nc),
        in_specs=[tok(1024), tok(512), tok(512), tok(128),
                  pl.BlockSpec((1, 8, lc), lambda b, c: (b, gr_block, c)),
                  full2(conv_w), full2(conv_b), full2(fbc), full2(fbr), full2(out_g),
                  st3(conv0p), st3(c0), st3(n0), st3(m0p)],
        out_specs=[tok(512), st3(c0), st3(n0), st3(m0p)],
        out_shape=[jax.ShapeDtypeStruct((bsz, t, ML_WIDTH), BF16),
                   jax.ShapeDtypeStruct(c0.shape, F32),
                   jax.ShapeDtypeStruct(n0.shape, F32),
                   jax.ShapeDtypeStruct(m0p.shape, F32)],
        scratch_shapes=[pltpu.VMEM((lc + 8, 2 * ML_WIDTH), F32),
                        pltpu.VMEM((ML_HEADS, ML_DIM, ML_DIM), F32),
                        pltpu.VMEM((ML_HEADS, ML_DIM), F32),
                        pltpu.VMEM((ML_HEADS, ML_DIM), F32)],
        compiler_params=_cparams(("arbitrary", "arbitrary")),
        name="mlstm",
    )(qk, v, o, sm, gr, conv_w, conv_b, fbc, fbr, out_g, conv0p, c0, n0, m0p)
    return hm, c1, n1, m1[:, :, 0]


def _compress_math(x_at, wc_at, w1cat, pe, b1, w2, b2, n_sub):
    acc = None
    for p in range(CMP_STRIDE):
        d = _dot(x_at(p).astype(BF16), wc_at(p))
        acc = d if acc is None else acc + d
    pe_hi = pe.astype(BF16)
    pe_lo = (pe - pe_hi.astype(F32)).astype(BF16)
    pe_c = _dot(pe_hi, w1cat) + _dot(pe_lo, w1cat)
    const = b1 + pe_c[0:1, :NSA_DIM] + pe_c[1:2, NSA_DIM:]
    outs = []
    for g in range(NSA_KV):
        p0 = acc[:, 128 * g:128 * g + NSA_DIM]
        p1 = pltpu.roll(acc[:, 128 * g + NSA_DIM:128 * (g + 1)], n_sub - 1, 0)
        outs.append(_dot(_gelu(const + p0 + p1).astype(BF16), w2) + b2)
    return outs


def _compress_kernel(x_ref, wc_ref, w1_ref, pe_ref, b1_ref, w2_ref, b2_ref, o_ref):
    n_sub = o_ref.shape[3]
    outs = _compress_math(lambda p: x_ref[0, pl.ds(p, n_sub, stride=CMP_STRIDE), :], lambda p: wc_ref[0, p],
                          w1_ref[0], pe_ref[0], b1_ref[0], w2_ref[0], b2_ref[0], n_sub)
    for g in range(NSA_KV):
        o_ref[0, 0, g] = outs[g].astype(BF16)


def _compress(kvf3, cw):
    bsz, t, _ = kvf3.shape
    n_sub = t // CMP_STRIDE
    kind = lambda a: pl.BlockSpec((1,) + a.shape[1:], lambda k, b: (k,) + (0,) * (a.ndim - 1))
    ws = [cw['wc'], cw['w1cat'], cw['pe'], cw['b1'], cw['w2'], cw['b2']]
    return pl.pallas_call(
        _compress_kernel,
        grid=(2, bsz),
        in_specs=[pl.BlockSpec((1, t, 128), lambda k, b: (b, 0, k))] + [kind(a) for a in ws],
        out_specs=pl.BlockSpec((1, 1, NSA_KV, n_sub, NSA_DIM), lambda k, b: (k, b, 0, 0, 0)),
        out_shape=jax.ShapeDtypeStruct((2, bsz, NSA_KV, n_sub, NSA_DIM), BF16),
        compiler_params=_cparams(("arbitrary", "arbitrary")),
        name="nsa_compress",
    )(kvf3, *ws)


def _nsa_cmp_kernel(q_ref, kc_ref, vc_ref, sm_ref, ovl_ref, o_ref, sel_ref, *, tq, qpos0, n_pick, nc):
    i = pl.program_id(1)
    q = q_ref[0]
    gates = _sigmoid(sm_ref[0])
    n_pad = kc_ref.shape[3]
    nb = ovl_ref.shape[0]
    qpos_c = qpos0 + i * tq + lax.broadcasted_iota(jnp.int32, (tq, 1), 0)
    n_i = lax.broadcasted_iota(jnp.int32, (1, n_pad), 1)
    valid = (n_i * CMP_STRIDE + (CMP_LEN - 1) <= qpos_c) & (n_i < nc)
    blk = lax.broadcasted_iota(jnp.int32, (nb, tq), 0)
    qpos_r = qpos0 + i * tq + lax.broadcasted_iota(jnp.int32, (nb, tq), 1)
    cur = qpos_r // SEL_BLOCK
    avail = blk * SEL_BLOCK <= qpos_r
    forced = (blk == 0) | (blk == cur) | (blk == cur - 1)
    outs = []
    for g in range(NSA_KV):
        kc = kc_ref[0, 0, g]
        vc = vc_ref[0, 0, g]
        psum = jnp.zeros((tq, n_pad), F32)
        for r in range(NSA_GROUP):
            hd = g * NSA_GROUP + r
            s = _dot_nt(q[:, NSA_DIM * hd:NSA_DIM * (hd + 1)], kc)
            s = jnp.where(valid, s, NEG)
            m = jnp.max(s, axis=1, keepdims=True)
            p = jnp.where(valid, jnp.exp2(s - m), 0.0)
            d = jnp.sum(p, axis=1, keepdims=True)
            p = p / jnp.where(d > 0, d, 1.0)
            psum = psum + p
            outs.append(_dot(p.astype(BF16), vc) * gates[:, 3 * hd:3 * hd + 1])
        imp = sum(_dot_nt(ovl_ref[...], part) for part in _split3(psum))
        val = jnp.where(avail, jnp.where(forced, 1e9, imp), -1.0)

        def pick(_, carry):
            val, sel = carry
            mx = jnp.max(val, axis=0, keepdims=True)
            first = jnp.min(jnp.where(val == mx, blk, nb), axis=0, keepdims=True)
            hit = blk == first
            return jnp.where(hit, -2.0, val), jnp.where(hit, 1.0, sel)

        _, sel = lax.fori_loop(0, n_pick, pick, (val, jnp.zeros((nb, tq), F32)))
        sel_ref[0, g] = ((sel - 1.0) * SEL_BIG).T.astype(BF16)
    o_ref[0] = jnp.concatenate(outs, axis=1)


def _nsa_cmp(qn, kvc, sm, ovl_t, tq, qpos0, n_pick, nc):
    bsz, t, _ = qn.shape
    n_pad = kvc.shape[3]
    nb = ovl_t.shape[0]
    return pl.pallas_call(
        functools.partial(_nsa_cmp_kernel, tq=tq, qpos0=qpos0, n_pick=n_pick, nc=nc),
        grid=(bsz, t // tq),
        in_specs=[pl.BlockSpec((1, tq, NSA_WIDTH), lambda b, i: (b, i, 0)),
                  pl.BlockSpec((1, 1, NSA_KV, n_pad, NSA_DIM), lambda b, i: (0, b, 0, 0, 0)),
                  pl.BlockSpec((1, 1, NSA_KV, n_pad, NSA_DIM), lambda b, i: (1, b, 0, 0, 0)),
                  pl.BlockSpec((1, tq, 128), lambda b, i: (b, i, 0)),
                  pl.BlockSpec((nb, n_pad), lambda b, i: (0, 0))],
        out_specs=[pl.BlockSpec((1, tq, NSA_WIDTH), lambda b, i: (b, i, 0)),
                   pl.BlockSpec((1, NSA_KV, tq, nb), lambda b, i: (b, 0, i, 0))],
        out_shape=[jax.ShapeDtypeStruct((bsz, t, NSA_WIDTH), F32),
                   jax.ShapeDtypeStruct((bsz, NSA_KV, t, nb), BF16)],
        compiler_params=_cparams(("arbitrary", "arbitrary")),
        name="nsa_cmp_select",
    )(qn, kvc, kvc, sm, ovl_t)


def _flash_t(k_at, v_at, qas, lo, n_full, hi, mask_at, m_scr, l_scr, acc_scr, p_scr, tk):
    w = qas[0].shape[0]
    cw = min(w, 128)
    m_scr[...] = jnp.full(m_scr.shape, -3e38, F32)
    l_scr[...] = jnp.zeros(l_scr.shape, F32)
    acc_scr[...] = jnp.zeros(acc_scr.shape, F32)

    def make_body(masked):
        def body(j, carry):
            for g in range(NSA_KV):
                s_all = _dot_nt(k_at(g, j), qas[g])
                for c in range(w // cw):
                    cs = slice(c * cw, (c + 1) * cw)
                    s = s_all[:, cs]
                    if masked:
                        s = jnp.where(mask_at(j, cs), s, NEG)
                    m_prev = m_scr[g, :, cs]
                    m_new = jnp.maximum(m_prev, jnp.max(s, axis=0, keepdims=True))
                    alpha = jnp.exp2(m_prev - m_new)
                    p = jnp.exp2(s - m_new)
                    l_scr[g, :, cs] = alpha * l_scr[g, :, cs] + jnp.sum(p, axis=0, keepdims=True)
                    m_scr[g, :, cs] = m_new
                    p_scr[g, 0:tk, cs] = p.astype(BF16)
                    acc_scr[g, :, cs] = acc_scr[g, :, cs] * alpha
                acc_scr[g] += _dot(v_at(g, j), p_scr[g, 0:tk, :])
            return carry
        return body

    lax.fori_loop(lo, n_full, make_body(False), 0)
    lax.fori_loop(n_full, hi, make_body(True), 0)
    return [acc_scr[g] / l_scr[g] for g in range(NSA_KV)]


def _nsa_sw_kernel(q_ref, ka_ref, kw_ref, vt_ref, sel_ref, gt_ref, o_ref,
                   qa_scr, qw_scr, m_scr, l_scr, acc_scr, p_scr, *, tq, tk, tkw):
    i = pl.program_id(1)
    w = NSA_GROUP * tq
    nb = sel_ref.shape[3]
    ka_w = qa_scr.shape[2]
    q = q_ref[0]
    gates_t = _sigmoid(gt_ref[0])
    q_first = i * tq
    q_last = q_first + tq - 1
    lane = lax.broadcasted_iota(jnp.int32, (1, w), 1)
    qpos = q_first + (lane & (tq - 1))
    if ka_w > nb + NSA_DIM:
        qa_scr[:, :, nb + NSA_DIM:] = jnp.zeros((NSA_KV, w, ka_w - nb - NSA_DIM), BF16)
    qw_scr[...] = jnp.zeros(qw_scr.shape, BF16)
    for g in range(NSA_KV):
        for r in range(NSA_GROUP):
            hd = g * NSA_GROUP + r
            rows = slice(r * tq, (r + 1) * tq)
            q_h = q[:, NSA_DIM * hd:NSA_DIM * (hd + 1)]
            qa_scr[g, rows, 0:nb] = sel_ref[0, g]
            qa_scr[g, rows, nb:nb + NSA_DIM] = q_h
            qw_scr[g, rows, NSA_DIM * g:NSA_DIM * (g + 1)] = q_h
    qas = [qa_scr[g] for g in range(NSA_KV)]
    qws = [qw_scr[g] for g in range(NSA_KV)]

    def ka_at(g, j):
        return ka_ref[0, pl.ds(pl.multiple_of(j * tk, tk), tk), ka_w * g:ka_w * (g + 1)]

    def vs_at(g, j):
        return vt_ref[0, NSA_DIM * g:NSA_DIM * (g + 1), pl.ds(pl.multiple_of(j * tk, tk), tk)]

    def kw_at(g, j):
        return kw_ref[0, pl.ds(pl.multiple_of(j * tkw, tkw), tkw), :]

    def vw_at(g, j):
        return vt_ref[0, 128 + NSA_DIM * g:128 + NSA_DIM * (g + 1), pl.ds(pl.multiple_of(j * tkw, tkw), tkw)]

    def slc_mask(j, cs):
        return j * tk + lax.broadcasted_iota(jnp.int32, (tk, 1), 0) <= qpos[:, cs]

    def win_mask(j, cs):
        kpos = j * tkw + lax.broadcasted_iota(jnp.int32, (tkw, 1), 0)
        return (kpos <= qpos[:, cs]) & (kpos >= qpos[:, cs] - WINDOW)

    o_s = _flash_t(ka_at, vs_at, qas, 0, (q_first + 1) // tk, q_last // tk + 1, slc_mask,
                   m_scr, l_scr, acc_scr, p_scr, tk)
    w_lo = jnp.maximum(q_first - WINDOW, 0) // tkw
    o_w = _flash_t(kw_at, vw_at, qws, w_lo, w_lo, q_last // tkw + 1, win_mask,
                   m_scr, l_scr, acc_scr, p_scr, tkw)
    outs = []
    for hd in range(NSA_HEADS):
        g, r = divmod(hd, NSA_GROUP)
        cols = slice(r * tq, (r + 1) * tq)
        o_h = (o_s[g][:, cols] * gates_t[3 * hd + 1:3 * hd + 2, :]
               + o_w[g][:, cols] * gates_t[3 * hd + 2:3 * hd + 3, :])
        outs.append(o_h.T)
    o_ref[0] = jnp.concatenate(outs, axis=1)


def _nsa_sw(qn, ka, kw, vt, selb, smt, tq, tk, tkw):
    bsz, t, _ = qn.shape
    nb = selb.shape[3]
    assert tq & (tq - 1) == 0
    w = NSA_GROUP * tq
    ka_w = ka.shape[2] // NSA_KV
    seq = lambda a: pl.BlockSpec((1,) + a.shape[1:], lambda b, i: (b, 0, 0))
    return pl.pallas_call(
        functools.partial(_nsa_sw_kernel, tq=tq, tk=tk, tkw=tkw),
        grid=(bsz, t // tq),
        in_specs=[pl.BlockSpec((1, tq, NSA_WIDTH), lambda b, i: (b, i, 0)),
                  seq(ka), seq(kw), seq(vt),
                  pl.BlockSpec((1, NSA_KV, tq, nb), lambda b, i: (b, 0, i, 0)),
                  pl.BlockSpec((1, 32, tq), lambda b, i: (b, 0, i))],
        out_specs=pl.BlockSpec((1, tq, NSA_WIDTH), lambda b, i: (b, i, 0)),
        out_shape=jax.ShapeDtypeStruct((bsz, t, NSA_WIDTH), F32),
        scratch_shapes=[pltpu.VMEM((NSA_KV, w, ka_w), BF16),
                        pltpu.VMEM((NSA_KV, w, 128), BF16),
                        pltpu.VMEM((NSA_KV, 1, w), F32),
                        pltpu.VMEM((NSA_KV, 1, w), F32),
                        pltpu.VMEM((NSA_KV, NSA_DIM, w), F32),
                        pltpu.VMEM((NSA_KV, max(tk, tkw), w), BF16)],
        compiler_params=_cparams(("arbitrary", "arbitrary")),
        name="nsa_select_window",
    )(qn, ka, kw, vt, selb, smt)


def _ab_out_kernel(x_ref, mod_ref, hm_ref, oc_ref, osw_ref, w_ref, o_ref):
    o_nsa = (oc_ref[...] + osw_ref[...]).astype(BF16)
    y = _dot(hm_ref[...], w_ref[:ML_WIDTH, :]) + _dot(o_nsa, w_ref[ML_WIDTH:, :])
    o_ref[...] = x_ref[...] + mod_ref[0, 2] * y


def _ab_out(x, mod4, tiles_per_seq, hm, o_cmp, o_sw, w, tm):
    n = x.shape[0]
    tok = lambda wd: pl.BlockSpec((tm, wd), lambda i: (i, 0))
    return pl.pallas_call(
        _ab_out_kernel,
        grid=(n // tm,),
        in_specs=[tok(D_MODEL), _mod_spec(mod4, tm, tiles_per_seq), tok(ML_WIDTH), tok(NSA_WIDTH),
                  tok(NSA_WIDTH), pl.BlockSpec(w.shape, lambda i: (0, 0))],
        out_specs=tok(D_MODEL),
        out_shape=jax.ShapeDtypeStruct((n, D_MODEL), F32),
        compiler_params=_cparams(("arbitrary",)),
        name="ab_out",
    )(x, mod4, hm, o_cmp, o_sw, w)


def _cl_kernel(x_ref, mod_ref, g_ref, wi_ref, bi_ref, vg_ref, ws_ref, bs_ref, wo_ref, o_ref, v_ref, *, tm):
    x = x_ref[...]
    h = _normmod(x, g_ref[...], mod_ref[0, 1], mod_ref[0, 0]).astype(BF16)
    z = _gelu(_dot(h, wi_ref[...]) + bi_ref[...])
    gw = z.shape[1] // 2
    u = z[:, :gw]
    v = z[:, gw:]
    v = v * lax.rsqrt(jnp.mean(v * v, axis=-1, keepdims=True) + EPS) * vg_ref[...]
    v_ref[...] = v
    vb = v.astype(BF16)
    lch = ws_ref.shape[1]
    tril = lax.broadcasted_iota(jnp.int32, (lch, lch), 0) >= lax.broadcasted_iota(jnp.int32, (lch, lch), 1)
    gd = gw // GM_GROUPS
    rows = []
    for c in range(tm // lch):
        cols = []
        for g in range(GM_GROUPS):
            wsg = jnp.where(tril, ws_ref[g], 0.0).astype(BF16)
            cols.append(_dot(wsg, vb[c * lch:(c + 1) * lch, g * gd:(g + 1) * gd]) + bs_ref[:, g:g + 1])
        rows.append(jnp.concatenate(cols, axis=1))
    s = jnp.concatenate(rows, axis=0) if len(rows) > 1 else rows[0]
    y = _dot((u * s).astype(BF16), wo_ref[...])
    o_ref[...] = x + mod_ref[0, 2] * y


def _cl_mixer(x, mod4, tiles_per_seq, g, wi, bi, vg, ws_eff, bs_eff, wo, tm):
    n = x.shape[0]
    gw = wo.shape[0]
    tok = lambda wd: pl.BlockSpec((tm, wd), lambda i: (i, 0))
    full = lambda a: pl.BlockSpec(a.shape, lambda i: (0,) * a.ndim)
    return pl.pallas_call(
        functools.partial(_cl_kernel, tm=tm),
        grid=(n // tm,),
        in_specs=[tok(D_MODEL), _mod_spec(mod4, tm, tiles_per_seq), full(g), full(wi), full(bi), full(vg),
                  full(ws_eff), full(bs_eff), full(wo)],
        out_specs=[tok(D_MODEL), tok(gw)],
        out_shape=[jax.ShapeDtypeStruct((n, D_MODEL), F32), jax.ShapeDtypeStruct((n, gw), F32)],
        compiler_params=_cparams(("arbitrary",)),
        name="gmlp_mixer",
    )(x, mod4, g, wi, bi, vg, ws_eff, bs_eff, wo)


def _ffn_kernel(x_ref, mod_ref, g_ref, w1_ref, w2_ref, fg_ref, o_ref, h_scr, acc_scr, *, final):
    j = pl.program_id(1)

    @pl.when(j == 0)
    def _():
        h_scr[...] = _normmod(x_ref[...], g_ref[...], mod_ref[0, 4], mod_ref[0, 3]).astype(BF16)
        acc_scr[...] = jnp.zeros(acc_scr.shape, F32)

    a = jnp.maximum(_dot(h_scr[...], w1_ref[...]), 0.0)
    acc_scr[...] += _dot((a * a).astype(BF16), w2_ref[...])

    @pl.when(j == pl.num_programs(1) - 1)
    def _():
        y = x_ref[...] + mod_ref[0, 5] * acc_scr[...]
        if final:
            y = y * lax.rsqrt(jnp.mean(y * y, axis=-1, keepdims=True) + EPS) * fg_ref[...]
        o_ref[...] = y


def _ffn(x, mod4, tiles_per_seq, g, w1, w2, final_g, final, tm, tf):
    n = x.shape[0]
    f = w1.shape[1]
    r = mod4.shape[2]
    if r == 1:
        mspec = pl.BlockSpec((1, 6, 1, D_MODEL), lambda i, j: (i // tiles_per_seq, 0, 0, 0))
    else:
        mspec = pl.BlockSpec((1, 6, tm, D_MODEL), lambda i, j: (i, 0, 0, 0))
    return pl.pallas_call(
        functools.partial(_ffn_kernel, final=final),
        grid=(n // tm, f // tf),
        in_specs=[pl.BlockSpec((tm, D_MODEL), lambda i, j: (i, 0)), mspec,
                  pl.BlockSpec((1, D_MODEL), lambda i, j: (0, 0)),
                  pl.BlockSpec((D_MODEL, tf), lambda i, j: (0, j)),
                  pl.BlockSpec((tf, D_MODEL), lambda i, j: (j, 0)),
                  pl.BlockSpec((1, D_MODEL), lambda i, j: (0, 0))],
        out_specs=pl.BlockSpec((tm, D_MODEL), lambda i, j: (i, 0)),
        out_shape=jax.ShapeDtypeStruct((n, D_MODEL), F32),
        scratch_shapes=[pltpu.VMEM((tm, D_MODEL), BF16), pltpu.VMEM((tm, D_MODEL), F32)],
        compiler_params=_cparams(("arbitrary", "arbitrary")),
        name="ffn",
    )(x, mod4, g, w1, w2, final_g)


SQ = 16


def _page_specs(pages_per_step, row_block):
    def in_map(p):
        return lambda b, s, pt: (pt[b, s * pages_per_step + p], row_block, 0)
    return [pl.BlockSpec((1, 256, PAGE_SIZE), in_map(p)) for p in range(pages_per_step)]


def _decode_cmp_kernel(pt_ref, *refs, pages_per_step, past, n_pick):
    pages = refs[:pages_per_step]
    (q_ref, sm_ref, ovl_ref, wc_ref, w1_ref, pe_ref, b1_ref, w2_ref, b2_ref, o_ref, sel_ref, x_scr) = refs[pages_per_step:]
    s_idx = pl.program_id(1)
    for p in range(pages_per_step):
        row0 = pl.multiple_of((s_idx * pages_per_step + p) * PAGE_SIZE, PAGE_SIZE)
        for kind in range(2):
            x_scr[kind, pl.ds(row0, PAGE_SIZE), :] = pages[p][0, 128 * kind:128 * (kind + 1), :].T

    @pl.when(s_idx == pl.num_programs(1) - 1)
    def _():
        n_sub = past // CMP_STRIDE
        nc = n_sub - 1
        nb = ovl_ref.shape[1]
        kv = [_compress_math(lambda p: x_scr[kind, pl.ds(p, n_sub, stride=CMP_STRIDE), :],
                             lambda p: wc_ref[kind, p], w1_ref[kind], pe_ref[kind], b1_ref[kind],
                             w2_ref[kind], b2_ref[kind], n_sub) for kind in range(2)]
        q = q_ref[0]
        gates = _sigmoid(sm_ref[0])
        qpos = past + lax.broadcasted_iota(jnp.int32, (SQ, 1), 0)
        n_i = lax.broadcasted_iota(jnp.int32, (1, n_sub), 1)
        valid = (n_i * CMP_STRIDE + (CMP_LEN - 1) <= qpos) & (n_i < nc)
        blk = lax.broadcasted_iota(jnp.int32, (SQ, nb), 1)
        cur = qpos // SEL_BLOCK
        avail = blk * SEL_BLOCK <= qpos
        forced = (blk == 0) | (blk == cur) | (blk == cur - 1)
        outs = []
        for g in range(NSA_KV):
            kc = kv[0][g].astype(BF16)
            vc = kv[1][g].astype(BF16)
            psum = jnp.zeros((SQ, n_sub), F32)
            for r in range(NSA_GROUP):
                hd = g * NSA_GROUP + r
                s = _dot_nt(q[:, NSA_DIM * hd:NSA_DIM * (hd + 1)], kc)
                s = jnp.where(valid, s, NEG)
                m = jnp.max(s, axis=1, keepdims=True)
                p = jnp.where(valid, jnp.exp2(s - m), 0.0)
                d = jnp.sum(p, axis=1, keepdims=True)
                p = p / jnp.where(d > 0, d, 1.0)
                psum = psum + p
                outs.append(_dot(p.astype(BF16), vc) * gates[:, 3 * hd:3 * hd + 1])
            imp = sum(_dot(part, ovl_ref[...]) for part in _split3(psum))
            val = jnp.where(avail, jnp.where(forced, 1e9, imp), -1.0)

            def pick(_, carry):
                val, sel = carry
                mx = jnp.max(val, axis=1, keepdims=True)
                first = jnp.min(jnp.where(val == mx, blk, nb), axis=1, keepdims=True)
                hit = blk == first
                return jnp.where(hit, -2.0, val), jnp.where(hit, 1.0, sel)

            _, sel = lax.fori_loop(0, n_pick, pick, (val, jnp.zeros((SQ, nb), F32)))
            sel_ref[0, g] = (sel - 1.0) * SEL_BIG
        o_ref[0] = jnp.concatenate(outs, axis=1)


def _decode_cmp(cache_t, page_table, qn, sm, ovl, cw, pages_per_step, n_pick):
    bsz, n_pages = page_table.shape
    past = n_pages * PAGE_SIZE
    nb = ovl.shape[1]
    full = lambda a: pl.BlockSpec(a.shape, lambda b, s, pt: (0,) * a.ndim)
    seq = lambda a: pl.BlockSpec((1,) + a.shape[1:], lambda b, s, pt: (b,) + (0,) * (a.ndim - 1))
    ws = [cw['wc'], cw['w1cat'], cw['pe'], cw['b1'], cw['w2'], cw['b2']]
    return pl.pallas_call(
        functools.partial(_decode_cmp_kernel, pages_per_step=pages_per_step, past=past, n_pick=n_pick),
        grid_spec=pltpu.PrefetchScalarGridSpec(
            num_scalar_prefetch=1,
            grid=(bsz, n_pages // pages_per_step),
            in_specs=_page_specs(pages_per_step, 0) + [seq(qn), seq(sm), full(ovl)] + [full(a) for a in ws],
            out_specs=[pl.BlockSpec((1, SQ, NSA_WIDTH), lambda b, s, pt: (b, 0, 0)),
                       pl.BlockSpec((1, NSA_KV, SQ, nb), lambda b, s, pt: (b, 0, 0, 0))],
            scratch_shapes=[pltpu.VMEM((2, past, 128), F32)]),
        out_shape=[jax.ShapeDtypeStruct((bsz, SQ, NSA_WIDTH), F32),
                   jax.ShapeDtypeStruct((bsz, NSA_KV, SQ, nb), F32)],
        compiler_params=_cparams(("arbitrary", "arbitrary")),
        name="decode_compress_select",
    )(page_table, *([cache_t] * pages_per_step), qn, sm, ovl, *ws)


def _decode_sw_kernel(pt_ref, *refs, pages_per_step, past, n_new):
    pages = refs[:pages_per_step]
    (et_ref, q_ref, sel_ref, kvn_ref, win_ref, sm_ref, o_ref, qa_scr, m_scr, l_scr, acc_scr) = refs[pages_per_step:]
    s_idx = pl.program_id(1)
    nb = sel_ref.shape[3]
    rows_n = NSA_HEADS * SQ
    wlen = win_ref.shape[2]

    @pl.when(s_idx == 0)
    def _():
        q = q_ref[0]
        qa_scr[...] = jnp.zeros(qa_scr.shape, F32)
        for hd in range(NSA_HEADS):
            g = hd // NSA_GROUP
            rows = slice(hd * SQ, (hd + 1) * SQ)
            qa_scr[rows, NSA_DIM * g:NSA_DIM * (g + 1)] = q[:, NSA_DIM * hd:NSA_DIM * (hd + 1)].astype(F32)
            qa_scr[rows, 128:128 + nb] = sel_ref[0, g]
        m_scr[...] = jnp.full(m_scr.shape, -3e38, F32)
        l_scr[...] = jnp.zeros(l_scr.shape, F32)
        acc_scr[...] = jnp.zeros(acc_scr.shape, F32)

    qa = qa_scr[...].astype(BF16)

    def online(s, pv):
        m_prev = m_scr[...]
        m_new = jnp.maximum(m_prev, jnp.max(s, axis=1, keepdims=True))
        alpha = jnp.exp2(m_prev - m_new)
        p = jnp.exp2(s - m_new[:, 0:1])
        l_scr[...] = alpha * l_scr[...] + jnp.sum(p, axis=1, keepdims=True)
        acc_scr[...] = alpha * acc_scr[...] + pv(p.astype(BF16))
        m_scr[...] = m_new

    for pair in range(pages_per_step // 2):
        pa, pb = pages[2 * pair], pages[2 * pair + 1]
        kt = jnp.concatenate([pa[0, 0:128, :], pb[0, 0:128, :]], axis=1).astype(BF16)
        vt = jnp.concatenate([pa[0, 128:256, :], pb[0, 128:256, :]], axis=1).astype(BF16)
        off = pl.multiple_of((s_idx * pages_per_step + 2 * pair) * PAGE_SIZE, 2 * PAGE_SIZE)
        rhs = jnp.concatenate([kt, et_ref[:, pl.ds(off, 2 * PAGE_SIZE)]], axis=0)
        online(_dot(qa, rhs), lambda p: _dot_nt(p, vt))

    @pl.when(s_idx == pl.num_programs(1) - 1)
    def _():
        tok = lax.broadcasted_iota(jnp.int32, (rows_n, 1), 0) & (SQ - 1)
        new_i = lax.broadcasted_iota(jnp.int32, (1, SQ), 1)
        new_ok = (new_i <= tok) & (new_i < n_new)
        kvn = kvn_ref[0]
        q2 = qa[:, 0:128]
        online(jnp.where(new_ok, _dot_nt(q2, kvn[:, 256:384]), NEG), lambda p: _dot(p, kvn[:, 384:512]))
        o_s = acc_scr[...] / l_scr[...]
        s_w = _dot(q2, win_ref[0, 0:128, :].astype(BF16))
        w_i = lax.broadcasted_iota(jnp.int32, (1, wlen), 1)
        s_w = jnp.where(w_i >= tok + (wlen - WINDOW), s_w, NEG)
        s_n = jnp.where(new_ok, _dot_nt(q2, kvn[:, 512:640]), NEG)
        m_w = jnp.maximum(jnp.max(s_w, axis=1, keepdims=True), jnp.max(s_n, axis=1, keepdims=True))
        p_w = jnp.exp2(s_w - m_w)
        p_n = jnp.exp2(s_n - m_w)
        l_w = jnp.sum(p_w, axis=1, keepdims=True) + jnp.sum(p_n, axis=1, keepdims=True)
        o_w = (_dot_nt(p_w.astype(BF16), win_ref[0, 128:256, :].astype(BF16))
               + _dot(p_n.astype(BF16), kvn[:, 640:768])) / l_w
        gates = _sigmoid(sm_ref[0])
        g_s = jnp.concatenate([gates[:, 3 * hd + 1:3 * hd + 2] for hd in range(NSA_HEADS)], axis=0)
        g_w = jnp.concatenate([gates[:, 3 * hd + 2:3 * hd + 3] for hd in range(NSA_HEADS)], axis=0)
        o = o_s * g_s + o_w * g_w
        half = rows_n // 2
        o_ref[0, 0:half, :] = o[0:half, 0:NSA_DIM]
        o_ref[0, half:, :] = o[half:, NSA_DIM:]


def _decode_sw(cache_t, page_table, et, qn, selb, kvn, win_t, sm, pages_per_step, n_new):
    bsz, n_pages = page_table.shape
    past = n_pages * PAGE_SIZE
    nb = et.shape[0]
    rows_n = NSA_HEADS * SQ
    full = lambda a: pl.BlockSpec(a.shape, lambda b, s, pt: (0,) * a.ndim)
    seq = lambda a: pl.BlockSpec((1,) + a.shape[1:], lambda b, s, pt: (b,) + (0,) * (a.ndim - 1))
    return pl.pallas_call(
        functools.partial(_decode_sw_kernel, pages_per_step=pages_per_step, past=past, n_new=n_new),
        grid_spec=pltpu.PrefetchScalarGridSpec(
            num_scalar_prefetch=1,
            grid=(bsz, n_pages // pages_per_step),
            in_specs=_page_specs(pages_per_step, 1) + [full(et), seq(qn), seq(selb), seq(kvn), seq(win_t), seq(sm)],
            out_specs=pl.BlockSpec((1, rows_n, NSA_DIM), lambda b, s, pt: (b, 0, 0)),
            scratch_shapes=[pltpu.VMEM((rows_n, 128 + nb), F32),
                            pltpu.VMEM((rows_n, 128), F32),
                            pltpu.VMEM((rows_n, 128), F32),
                            pltpu.VMEM((rows_n, 128), F32)]),
        out_shape=jax.ShapeDtypeStruct((bsz, rows_n, NSA_DIM), F32),
        compiler_params=_cparams(("arbitrary", "arbitrary")),
        name="decode_select_window",
    )(page_table, *([cache_t] * pages_per_step), et, qn, selb, kvn, win_t, sm)


def _selection_constants(n_keys):
    nb = n_keys // SEL_BLOCK
    n_sub = n_keys // CMP_STRIDE
    i = np.arange(n_sub)[None, :] * CMP_STRIDE
    j = np.arange(nb)[:, None] * SEL_BLOCK
    ovl_t = ((i < j + SEL_BLOCK) & (i + CMP_LEN > j) & (np.arange(n_sub)[None, :] < n_sub - 1))
    et = (np.arange(n_keys)[None, :] // SEL_BLOCK) == np.arange(nb)[:, None]
    return jnp.asarray(ovl_t, BF16), jnp.asarray(et, BF16)


def _ab_layer_prompt(x, mod4, tps, W, l, bsz, t, cfg):
    n = bsz * t
    j = l // 2
    tm, lc = cfg['tm'], cfg['lc']
    qk, v_m, o_pre, qn, kvf, sm, ka, kw, vt, smt = _ab_in(x, mod4, tps, W['norm_g0'][l], W['ab_w_in'][j],
                                                          W['ab_b_in'][j], tm, seq_shape=(bsz, t))
    seq = lambda a: a.reshape(bsz, t, a.shape[1])
    conv0 = jnp.zeros((bsz, CONV_W - 1, 2 * ML_WIDTH), F32)
    c0 = jnp.zeros((bsz, ML_HEADS, ML_DIM, ML_DIM), F32)
    n0 = jnp.zeros((bsz, ML_HEADS, ML_DIM), F32)
    m0 = jnp.zeros((bsz, ML_HEADS), F32)
    hm, c1, n1, m1 = _mlstm(seq(qk), seq(v_m), seq(o_pre), seq(sm), smt, W['ml_conv_w'][j], W['ml_conv_b'][j],
                            W['ml_f_bias'][j], W['ml_out_g'][j], conv0, c0, n0, m0, lc, gr_block=_SM_I // 8)
    conv_new = seq(qk)[:, -(CONV_W - 1):]
    kv_rows = kvf[:, :512].reshape(bsz, t, 4, NSA_KV, NSA_DIM)
    win_rows = kvf[:, 512:].reshape(bsz, t, 2, NSA_KV, NSA_DIM)
    kvc = _compress(seq(kvf), W['cmp'][j])
    ovl_t, _ = _selection_constants(t)
    o_cmp, selb = _nsa_cmp(seq(qn), kvc, seq(sm), ovl_t, cfg['tq'], 0, N_SEL, t // CMP_STRIDE - 1)
    o_sw = _nsa_sw(seq(qn), seq(ka), seq(kw), vt, selb, smt, cfg['tq'], cfg['tk'], cfg['tkw'])
    x = _ab_out(x, mod4, tps, hm.reshape(n, ML_WIDTH), o_cmp.reshape(n, NSA_WIDTH), o_sw.reshape(n, NSA_WIDTH),
                W['ab_w_out'][j], tm)
    return x, (conv_new, c1, n1, m1, kv_rows, win_rows[:, -min(WINDOW, t):])


def _ab_layer_decode(x, mod4, W, l, bsz, t, cfg, st, page_table):
    n = bsz * t
    j = l // 2
    tm, lc = cfg['tm'], cfg['lc']
    conv0, c0, n0, m0, (cache_t, page0), win_buf, win_t = st
    page_table = page_table + page0
    qk, v_m, o_pre, qn, kvf, sm, kvb = _ab_in(x, mod4, 1, W['norm_g0'][l], W['ab_w_in'][j], W['ab_b_in'][j], tm)
    seq = lambda a: a.reshape(bsz, t, a.shape[1])
    pad_t = lambda a, tp: jnp.pad(a, ((0, 0), (0, tp - t), (0, 0)))
    sm3 = seq(sm)
    gr = sm3[:, :, _SM_I:_SM_I + 8].transpose(0, 2, 1)
    gr = jnp.concatenate([jnp.pad(gr[:, :4], ((0, 0), (0, 0), (0, lc - t)), constant_values=NEG),
                          jnp.pad(gr[:, 4:], ((0, 0), (0, 0), (0, lc - t)), constant_values=-NEG)], axis=1)
    sm_pad = jnp.zeros((bsz, lc - t, 128), F32).at[:, :, _SM_I:_SM_I + 4].set(NEG).at[:, :, _SM_F:_SM_F + 4].set(-NEG)
    hm, c1, n1, m1 = _mlstm(pad_t(seq(qk), lc), pad_t(seq(v_m), lc), pad_t(seq(o_pre), lc),
                            jnp.concatenate([sm3, sm_pad], axis=1), gr, W['ml_conv_w'][j], W['ml_conv_b'][j],
                            W['ml_f_bias'][j], W['ml_out_g'][j], conv0, c0, n0, m0, lc)
    hm = hm[:, :t].reshape(n, ML_WIDTH)
    conv_new = jnp.concatenate([conv0, seq(qk)], axis=1)[:, -(CONV_W - 1):]
    kv_rows = kvf[:, :512].reshape(bsz, t, 4, NSA_KV, NSA_DIM)
    win_rows = kvf[:, 512:].reshape(bsz, t, 2, NSA_KV, NSA_DIM)
    past = page_table.shape[1] * PAGE_SIZE
    assert (past + t) // CMP_STRIDE == past // CMP_STRIDE and t <= min(SEL_BLOCK, SQ) and past % SEL_BLOCK == 0
    ovl_t, et = _selection_constants(past)
    qn3, sm3q, kvn = pad_t(seq(qn), SQ), pad_t(sm3, SQ), pad_t(seq(kvb), SQ)
    pps = cfg['pages_per_step']
    o_cmp, selb = _decode_cmp(cache_t, page_table, qn3, sm3q, ovl_t.T, W['cmp'][j], pps, N_SEL - 1)
    o_sw = _decode_sw(cache_t, page_table, et, qn3, selb, kvn, win_t, sm3q, pps, t)
    o_cmp = o_cmp[:, :t].reshape(n, NSA_WIDTH)
    o_sw = o_sw.reshape(bsz, NSA_HEADS, SQ, NSA_DIM).transpose(0, 2, 1, 3)[:, :t].reshape(n, NSA_WIDTH)
    x = _ab_out(x, mod4, 1, hm, o_cmp, o_sw, W['ab_w_out'][j], tm)
    win_new = jnp.concatenate([win_buf, win_rows], axis=1)[:, -win_buf.shape[1]:]
    return x, (conv_new, c1, n1, m1, kv_rows, win_new)


def _trunk(x3, mods, W, state, page_table, cfg):
    bsz, t, _ = x3.shape
    n = bsz * t
    tm = cfg['tm']
    x = x3.reshape(n, D_MODEL)
    depth = mods.shape[0]
    ab_new, cl_new = [], []
    if t % tm == 0:
        tps = t // tm
        to_mod4 = lambda m: m.reshape(bsz, 6, 1, D_MODEL)
    else:
        assert n == tm
        tps = 1
        to_mod4 = lambda m: jnp.repeat(m.reshape(bsz, 6, D_MODEL), t, axis=0).reshape(n, 6, D_MODEL).transpose(1, 0, 2)[None]
    lch = min(GM_CHUNK, t)
    for l in range(depth):
        mod4 = to_mod4(mods[l])
        j = l // 2
        if l % 2 == 0:
            if state is None:
                x, new = _ab_layer_prompt(x, mod4, tps, W, l, bsz, t, cfg)
            else:
                x, new = _ab_layer_decode(x, mod4, W, l, bsz, t, cfg, tuple(a[j] for a in state), page_table)
            ab_new.append(new)
        else:
            ws = W['cl_ws'][j][:, :lch, :lch]
            bs = W['cl_bs'][j][:, :lch]
            if lch < GM_CHUNK:
                rep = GM_CHUNK // lch
                ws = jnp.einsum('ab,gts->gatbs', jnp.eye(rep, dtype=F32), ws).reshape(GM_GROUPS, GM_CHUNK, GM_CHUNK)
                bs = jnp.tile(bs, (1, rep))
            x, v = _cl_mixer(x, mod4, tps, W['norm_g0'][l], W['cl_w_in'][j], W['cl_b_in'][j], W['cl_v_g'][j],
                             ws, bs.T, W['cl_w_out'][j], tm)
            cl_new.append(v.reshape(bsz, t, -1))
        x = _ffn(x, mod4, max(t // cfg['tm_ffn'], 1), W['norm_g1'][l], W['ffn_w1'][l], W['ffn_w2'][l], W['final_g'],
                 l == depth - 1, cfg['tm_ffn'], cfg['tf'])
    return x.reshape(bsz, t, D_MODEL), ab_new, cl_new


def kernel(x_prompt, x_sample, c_prompt, c_sample, state_mlstm_conv, state_mlstm_C, state_mlstm_n,
           state_mlstm_m, cache_nsa_kv, state_nsa_win, page_table, ada_w, ada_b, norm_g, ab_w_in, ab_b_in,
           ml_conv_w, ml_conv_b, ml_f_bias, ml_out_g, phi_pe, phi_w1, phi_b1, phi_w2, phi_b2, ab_w_out,
           cl_w_in, cl_b_in, cl_v_g, cl_ws, cl_bs, cl_w_out, ffn_w1, ffn_w2, final_g):
    depth = ada_w.shape[0]
    n_ab = ab_w_in.shape[0]
    bp, bs_ = c_prompt.shape[0], c_sample.shape[0]
    rows = bp + bs_
    rows_pad = -(-rows // 8) * 8
    c_all = jnp.concatenate([c_prompt, c_sample, jnp.zeros((rows_pad - rows, D_MODEL), F32)], axis=0)
    mods = _ada_mod(c_all, ada_w, ada_b)
    mods_p = mods[:, :bp]
    mods_s = mods[:, bp:rows]
    w_in = jnp.concatenate([ab_w_in[:, :, 0:2048], ab_w_in[:, :, 2056:3336], ab_w_in[:, :, 3336:3360],
                            ab_w_in[:, :, 2048:2056], jnp.zeros((n_ab, D_MODEL, 96), F32)], axis=2).astype(BF16)
    b_in = jnp.concatenate([ab_b_in[:, 0:2048], ab_b_in[:, 2056:3336], ab_b_in[:, 3336:3360],
                            ab_b_in[:, 2048:2056], jnp.zeros((n_ab, 96), F32)], axis=1)[:, None, :]
    half = CMP_STRIDE * NSA_DIM
    blk = phi_w1.reshape(n_ab, 2, 2, CMP_STRIDE, NSA_DIM, NSA_DIM).transpose(0, 1, 3, 4, 2, 5)
    blk = blk.reshape(n_ab, 2, CMP_STRIDE, NSA_DIM, 2 * NSA_DIM)
    zero = jnp.zeros_like(blk)
    wc = jnp.concatenate([jnp.concatenate([blk, zero], axis=4), jnp.concatenate([zero, blk], axis=4)], axis=3)
    pe = jnp.pad(phi_pe.reshape(n_ab, 2, 2, half), ((0, 0), (0, 0), (0, 6), (0, 0)))
    cmp_w = [dict(wc=wc[j].astype(BF16),
                  w1cat=jnp.concatenate([phi_w1[j, :, :half], phi_w1[j, :, half:]], axis=2).astype(BF16),
                  pe=pe[j], b1=phi_b1[j][:, None, :], w2=phi_w2[j].astype(BF16), b2=phi_b2[j][:, None, :])
             for j in range(n_ab)]
    W = dict(
        norm_g0=norm_g[:, 0][:, None, :], norm_g1=norm_g[:, 1][:, None, :],
        ab_w_in=w_in, ab_b_in=b_in,
        ml_conv_w=ml_conv_w, ml_conv_b=ml_conv_b[:, None, :], ml_f_bias=ml_f_bias, ml_out_g=ml_out_g[:, None, :],
        cmp=cmp_w,
        ab_w_out=ab_w_out.astype(BF16),
        cl_w_in=cl_w_in.astype(BF16), cl_b_in=cl_b_in[:, None, :], cl_v_g=cl_v_g[:, None, :],
        cl_ws=cl_ws, cl_bs=cl_bs, cl_w_out=cl_w_out.astype(BF16),
        ffn_w1=ffn_w1.astype(BF16), ffn_w2=ffn_w2.astype(BF16), final_g=final_g[None, :])
    cfg_p = dict(tm=256, tm_ffn=512, tf=512, lc=256, tq=128, tk=512, tkw=256)
    y_prompt, ab_p, _ = _trunk(x_prompt, mods_p, W, None, None, cfg_p)
    n_s = x_sample.shape[0] * x_sample.shape[1]
    cfg_s = dict(tm=n_s, tm_ffn=n_s, tf=512, lc=128, pages_per_step=8)
    n_pool = cache_nsa_kv.shape[1]
    cache_t = cache_nsa_kv.transpose(0, 1, 3, 4, 5, 2).reshape(n_ab * n_pool, 4 * NSA_KV * NSA_DIM, PAGE_SIZE)
    cache_t = [(cache_t, j * n_pool) for j in range(n_ab)]
    wlen = state_nsa_win.shape[2]
    win_t = state_nsa_win.transpose(0, 1, 3, 4, 5, 2).reshape(n_ab, bs_, 2 * NSA_KV * NSA_DIM, wlen)
    state = (state_mlstm_conv, state_mlstm_C, state_mlstm_n, state_mlstm_m, cache_t, state_nsa_win, win_t)
    y_sample, ab_s, cl_s = _trunk(x_sample, mods_s, W, state, page_table, cfg_s)
    p_out = [jnp.stack(a) for a in zip(*ab_p)]
    s_out = [jnp.stack(a) for a in zip(*ab_s)]
    return (y_prompt, y_sample, *p_out, *s_out, jnp.stack(cl_s))
```

```python
import functools

import numpy as np
import jax
import jax.numpy as jnp
from jax import lax
from jax.experimental import pallas as pl
from jax.experimental.pallas import tpu as pltpu

F32 = jnp.float32
BF16 = jnp.bfloat16

EPS = 1e-6
D_MODEL = 1024
ML_HEADS = 4
ML_DIM = 128
ML_WIDTH = ML_HEADS * ML_DIM
CONV_W = 4
NSA_HEADS = 8
NSA_KV = 2
NSA_GROUP = NSA_HEADS // NSA_KV
NSA_DIM = 64
NSA_WIDTH = NSA_HEADS * NSA_DIM
CMP_LEN = 32
CMP_STRIDE = 16
SEL_BLOCK = 64
N_SEL = 16
WINDOW = 512
GM_GROUPS = 4
GM_CHUNK = 128
PAGE_SIZE = 128

_SEG_QK = (0, 1024)
_SEG_V = (1024, 1536)
_SEG_O = (1536, 2048)
_SEG_QN = (2048, 2560)
_SEG_KV = (2560, 3328)
_SEG_SM = (3328, 3456)
_AB_COLS = 3456
_SM_I = 24
_SM_F = 28

NEG = -1e30
SEL_BIG = 2.0 ** 100
LOG2E = 1.4426950408889634
VMEM_LIMIT = 56 * 1024 * 1024


def _cparams(sem):
    return pltpu.CompilerParams(dimension_semantics=sem, vmem_limit_bytes=VMEM_LIMIT)


def _sigmoid(x):
    return 1.0 / (1.0 + jnp.exp(-x))


def _log_sigmoid(x):
    return jnp.minimum(x, 0.0) - jnp.log1p(jnp.exp(-jnp.abs(x)))


def _gelu(x):
    return 0.5 * x * (1.0 + jnp.tanh(0.7978845608028654 * (x + 0.044715 * (x * x * x))))


def _normmod(x, g, scale, shift):
    y = x * lax.rsqrt(jnp.mean(x * x, axis=-1, keepdims=True) + EPS) * g
    return y * (1.0 + scale) + shift


def _dot(a, b):
    return jnp.dot(a, b, preferred_element_type=F32)


def _dot_nt(a, b):
    return lax.dot_general(a, b, (((1,), (1,)), ((), ())), preferred_element_type=F32)


def _split3(x):
    hi = x.astype(BF16)
    r = x - hi.astype(F32)
    mid = r.astype(BF16)
    lo = (r - mid.astype(F32)).astype(BF16)
    return hi, mid, lo


def _ada_kernel(c_ref, w_ref, b_ref, o_ref):
    c = c_ref[...]
    sc = c * _sigmoid(c)
    sc_hi = sc.astype(BF16)
    sc_lo = (sc - sc_hi.astype(F32)).astype(BF16)
    w = w_ref[0]
    w_hi = w.astype(BF16)
    w_lo = (w - w_hi.astype(F32)).astype(BF16)
    acc = _dot(sc_hi, w_hi) + _dot(sc_lo, w_hi) + _dot(sc_hi, w_lo)
    o_ref[0] = acc + b_ref[0]


def _ada_mod(c_all, ada_w, ada_b):
    depth, d, n = ada_w.shape
    bp = c_all.shape[0]
    tn = 1536
    return pl.pallas_call(
        _ada_kernel,
        grid=(depth, n // tn),
        in_specs=[pl.BlockSpec((bp, d), lambda l, j: (0, 0)),
                  pl.BlockSpec((1, d, tn), lambda l, j: (l, 0, j)),
                  pl.BlockSpec((1, 1, tn), lambda l, j: (l, 0, j))],
        out_specs=pl.BlockSpec((1, bp, tn), lambda l, j: (l, 0, j)),
        out_shape=jax.ShapeDtypeStruct((depth, bp, n), F32),
        compiler_params=_cparams(("arbitrary", "arbitrary")),
        name="ada_mod",
    )(c_all, ada_w, ada_b.reshape(depth, 1, n))


def _mod_spec(mod4, tm, tiles_per_seq):
    r = mod4.shape[2]
    if r == 1:
        return pl.BlockSpec((1, 6, 1, D_MODEL), lambda i: (i // tiles_per_seq, 0, 0, 0))
    assert r == tm
    return pl.BlockSpec((1, 6, tm, D_MODEL), lambda i: (i, 0, 0, 0))


def _ab_in_kernel(x_ref, mod_ref, g_ref, w_ref, b_ref, *rest, tm, tiles_per_seq, nb, seq_layouts):
    h = _normmod(x_ref[...], g_ref[...], mod_ref[0, 1], mod_ref[0, 0]).astype(BF16)

    def seg(ab):
        a, b = ab
        return _dot(h, w_ref[:, a:b]) + b_ref[:, a:b]

    if seq_layouts:
        wt_ref, bt_ref, qk_ref, v_ref, o_ref, qn_ref, kvf_ref, sm_ref, ka_ref, kw_ref, vt_ref, smt_ref = rest
    else:
        qk_ref, v_ref, o_ref, qn_ref, kvf_ref, sm_ref, kvb_ref = rest
    qk_ref[...] = seg(_SEG_QK)
    v_ref[...] = seg(_SEG_V).astype(BF16)
    o_ref[...] = seg(_SEG_O)
    qn_ref[...] = (seg(_SEG_QN) * (NSA_DIM ** -0.5 * LOG2E)).astype(BF16)
    kv = seg(_SEG_KV)
    kvf_ref[...] = kv
    sm_ref[...] = seg(_SEG_SM)
    if not seq_layouts:
        kvb_ref[...] = kv.astype(BF16)
        return
    pos = (pl.program_id(0) % tiles_per_seq) * tm + lax.broadcasted_iota(jnp.int32, (tm, nb), 0)
    onehot = jnp.where(pos // SEL_BLOCK == lax.broadcasted_iota(jnp.int32, (tm, nb), 1), 1.0, 0.0).astype(BF16)
    ka_w = ka_ref.shape[1] // NSA_KV
    pieces = []
    for g in range(NSA_KV):
        pieces += [onehot, kv[:, 256 + NSA_DIM * g:256 + NSA_DIM * (g + 1)].astype(BF16)]
        if ka_w > nb + NSA_DIM:
            pieces.append(jnp.zeros((tm, ka_w - nb - NSA_DIM), BF16))
    ka_ref[...] = jnp.concatenate(pieces, axis=1)
    kw_ref[...] = kv[:, 512:640].astype(BF16)
    zt = _dot_nt(wt_ref[...], h) + bt_ref[...]
    vt_ref[0] = zt[:256].astype(BF16)
    smt_ref[0] = zt[256:]


def _ab_in(x, mod4, tiles_per_seq, g, w, b, tm, seq_shape=None):
    n = x.shape[0]
    widths = [(1024, F32), (512, BF16), (512, F32), (512, BF16), (768, F32), (128, F32)]
    ins = [x, mod4, g, w, b]
    in_specs = [pl.BlockSpec((tm, D_MODEL), lambda i: (i, 0)),
                _mod_spec(mod4, tm, tiles_per_seq),
                pl.BlockSpec((1, D_MODEL), lambda i: (0, 0)),
                pl.BlockSpec((D_MODEL, _AB_COLS), lambda i: (0, 0)),
                pl.BlockSpec((1, _AB_COLS), lambda i: (0, 0))]
    nb = 0
    if seq_shape is None:
        widths.append((768, BF16))
    else:
        bsz, t = seq_shape
        nb = t // SEL_BLOCK
        ka_w = -(-(nb + NSA_DIM) // 128) * 128
        widths += [(NSA_KV * ka_w, BF16), (128, BF16)]
        a, c = _SEG_KV[0], _SEG_SM[0]
        cols = jnp.concatenate([w[:, a + 384:a + 512], w[:, a + 640:a + 768], w[:, c:c + 128]], axis=1)
        bcols = jnp.concatenate([b[:, a + 384:a + 512], b[:, a + 640:a + 768], b[:, c:c + 128]], axis=1)
        ins += [cols.T, bcols.T]
        in_specs += [pl.BlockSpec((384, D_MODEL), lambda i: (0, 0)), pl.BlockSpec((384, 1), lambda i: (0, 0))]
    out_specs = [pl.BlockSpec((tm, wd), lambda i: (i, 0)) for wd, _ in widths]
    out_shape = [jax.ShapeDtypeStruct((n, wd), dt) for wd, dt in widths]
    if seq_shape is not None:
        seq_map = lambda i: (i // tiles_per_seq, 0, i % tiles_per_seq)
        out_specs += [pl.BlockSpec((1, 256, tm), seq_map), pl.BlockSpec((1, 128, tm), seq_map)]
        out_shape += [jax.ShapeDtypeStruct((bsz, 256, t), BF16), jax.ShapeDtypeStruct((bsz, 128, t), F32)]
    return pl.pallas_call(
        functools.partial(_ab_in_kernel, tm=tm, tiles_per_seq=tiles_per_seq, nb=nb,
                          seq_layouts=seq_shape is not None),
        grid=(n // tm,),
        in_specs=in_specs,
        out_specs=out_specs,
        out_shape=out_shape,
        compiler_params=_cparams(("arbitrary",)),
        name="ab_in",
    )(*ins)


def _mlstm_kernel(qk_ref, v_ref, o_ref, sm_ref, gr_ref, cw_ref, cb_ref, fbc_ref, fbr_ref, og_ref,
                  conv0_ref, c0_ref, n0_ref, m0_ref,
                  hm_ref, cout_ref, nout_ref, mout_ref,
                  xp_scr, c_scr, n_scr, m_scr, *, lc):
    c = pl.program_id(1)

    @pl.when(c == 0)
    def _():
        xp_scr[0:8, :] = conv0_ref[0]
        c_scr[...] = c0_ref[0]
        n_scr[...] = n0_ref[0]
        m_scr[...] = m0_ref[0]

    xp_scr[8:8 + lc, :] = qk_ref[0]
    y = cb_ref[...]
    for j in range(CONV_W):
        y = y + xp_scr[5 + j:5 + j + lc, :] * cw_ref[j:j + 1, :]
    xp_scr[0:8, :] = xp_scr[lc:lc + 8, :]
    act = y * _sigmoid(y)
    q_all = act[:, :ML_WIDTH]
    k_all = act[:, ML_WIDTH:] * (ML_DIM ** -0.5)

    sm = sm_ref[0]
    gr = gr_ref[0]
    lf_cols = _log_sigmoid(sm + fbc_ref[...])
    lf_rows = _log_sigmoid(gr + fbr_ref[...])
    v_all = v_ref[0]
    o_all = o_ref[0]

    row_i = lax.broadcasted_iota(jnp.int32, (lc, lc), 0)
    col_i = lax.broadcasted_iota(jnp.int32, (lc, lc), 1)
    tril = row_i >= col_i
    triu = row_i <= col_i

    outs = []
    for h in range(ML_HEADS):
        hs = slice(ML_DIM * h, ML_DIM * (h + 1))
        li_c = sm[:, _SM_I + h:_SM_I + h + 1]
        lf_c = lf_cols[:, _SM_F + h:_SM_F + h + 1]
        li_r = gr[h:h + 1, :]
        lf_r = lf_rows[4 + h:5 + h, :]
        b_c = jnp.sum(jnp.where(tril, lf_r, 0.0), axis=1, keepdims=True)
        b_r = jnp.sum(jnp.where(triu, lf_c, 0.0), axis=0, keepdims=True)
        a_r = li_r - b_r
        a_c = li_c - b_c
        m_h = m_scr[h:h + 1, 0:1]
        cm_c = jnp.maximum(m_h, jnp.max(jnp.where(tril, a_r, NEG), axis=1, keepdims=True))
        dm = jnp.exp(jnp.where(tril, a_r - cm_c, NEG))
        w_int = jnp.exp(m_h - cm_c)
        qh = q_all[:, hs]
        kh = k_all[:, hs]
        vh = v_all[:, hs]
        qb = qh.astype(BF16)
        kb = kh.astype(BF16)
        s = _dot_nt(qb, kb) * dm
        c_old = c_scr[h]
        n_old = n_scr[h:h + 1, :]
        num = w_int * _dot(qb, c_old.astype(BF16)) + _dot(s.astype(BF16), vh)
        den = w_int * jnp.sum(qh * n_old, axis=1, keepdims=True) + jnp.sum(s, axis=1, keepdims=True)
        mt = b_c + cm_c
        hh = num / jnp.maximum(jnp.abs(den), jnp.exp(-mt))
        hn = hh * lax.rsqrt(jnp.mean(hh * hh, axis=1, keepdims=True) + EPS) * og_ref[:, hs]
        outs.append(hn * _sigmoid(o_all[:, hs]))
        cm_last = jnp.maximum(m_h, jnp.max(a_r, axis=1, keepdims=True))
        bl = jnp.sum(lf_r, axis=1, keepdims=True)
        decay = jnp.exp(m_h - cm_last)
        ws_c = jnp.exp(a_c - cm_last)
        kt = kh.T.astype(BF16)
        c_scr[h] = decay * c_old + _dot(kt, (ws_c * vh.astype(F32)).astype(BF16))
        n_scr[h:h + 1, :] = decay * n_old + jnp.sum(ws_c * kh, axis=0, keepdims=True)
        m_scr[h:h + 1, :] = jnp.broadcast_to(bl + cm_last, (1, ML_DIM))

    hm_ref[0] = jnp.concatenate(outs, axis=1).astype(BF16)

    @pl.when(c == pl.num_programs(1) - 1)
    def _():
        cout_ref[0] = c_scr[...]
        nout_ref[0] = n_scr[...]
        mout_ref[0] = m_scr[...]


def _mlstm(qk, v, o, sm, gr, conv_w, conv_b, f_bias, out_g, conv0, c0, n0, m0, lc, gr_block=0):
    bsz, t, _ = qk.shape
    nc = t // lc
    fbc = jnp.zeros((1, 128), F32).at[0, _SM_F:_SM_F + ML_HEADS].set(f_bias)
    fbr = jnp.zeros((8, 1), F32).at[4:8, 0].set(f_bias)
    conv0p = jnp.concatenate([jnp.zeros((bsz, 5, 2 * ML_WIDTH), F32), conv0], axis=1)
    m0p = jnp.broadcast_to(m0[:, :, None], (bsz, ML_HEADS, ML_DIM))
    tok = lambda wd: pl.BlockSpec((1, lc, wd), lambda b, c: (b, c, 0))
    full2 = lambda a: pl.BlockSpec(a.shape, lambda b, c: (0, 0))
    st3 = lambda a: pl.BlockSpec((1,) + a.shape[1:], lambda b, c: (b,) + (0,) * (a.ndim - 1))
    hm, c1, n1, m1 = pl.pallas_call(
        functools.partial(_mlstm_kernel, lc=lc),
        grid=(bsz, nc),
        in_specs=[tok(1024), tok(512), tok(512), tok(128),
                  pl.BlockSpec((1, 8, lc), lambda b, c: (b, gr_block, c)),
                  full2(conv_w), full2(conv_b), full2(fbc), full2(fbr), full2(out_g),
                  st3(conv0p), st3(c0), st3(n0), st3(m0p)],
        out_specs=[tok(512), st3(c0), st3(n0), st3(m0p)],
        out_shape=[jax.ShapeDtypeStruct((bsz, t, ML_WIDTH), BF16),
                   jax.ShapeDtypeStruct(c0.shape, F32),
                   jax.ShapeDtypeStruct(n0.shape, F32),
                   jax.ShapeDtypeStruct(m0p.shape, F32)],
        scratch_shapes=[pltpu.VMEM((lc + 8, 2 * ML_WIDTH), F32),
                        pltpu.VMEM((ML_HEADS, ML_DIM, ML_DIM), F32),
                        pltpu.VMEM((ML_HEADS, ML_DIM), F32),
                        pltpu.VMEM((ML_HEADS, ML_DIM), F32)],
        compiler_params=_cparams(("arbitrary", "arbitrary")),
        name="mlstm",
    )(qk, v, o, sm, gr, conv_w, conv_b, fbc, fbr, out_g, conv0p, c0, n0, m0p)
    return hm, c1, n1, m1[:, :, 0]


def _compress_math(x_at, wc_at, w1cat, pe, b1, w2, b2, n_sub):
    acc = None
    for p in range(CMP_STRIDE):
        d = _dot(x_at(p).astype(BF16), wc_at(p))
        acc = d if acc is None else acc + d
    pe_hi = pe.astype(BF16)
    pe_lo = (pe - pe_hi.astype(F32)).astype(BF16)
    pe_c = _dot(pe_hi, w1cat) + _dot(pe_lo, w1cat)
    const = b1 + pe_c[0:1, :NSA_DIM] + pe_c[1:2, NSA_DIM:]
    outs = []
    for g in range(NSA_KV):
        p0 = acc[:, 128 * g:128 * g + NSA_DIM]
        p1 = pltpu.roll(acc[:, 128 * g + NSA_DIM:128 * (g + 1)], n_sub - 1, 0)
        outs.append(_dot(_gelu(const + p0 + p1).astype(BF16), w2) + b2)
    return outs


def _compress_kernel(x_ref, wc_ref, w1_ref, pe_ref, b1_ref, w2_ref, b2_ref, o_ref, ot_ref):
    n_sub = o_ref.shape[3]
    outs = _compress_math(lambda p: x_ref[0, pl.ds(p, n_sub, stride=CMP_STRIDE), :], lambda p: wc_ref[0, p],
                          w1_ref[0], pe_ref[0], b1_ref[0], w2_ref[0], b2_ref[0], n_sub)
    for g in range(NSA_KV):
        o_ref[0, 0, g] = outs[g].astype(BF16)
        ot_ref[0, 0, g] = outs[g].T.astype(BF16)


def _compress(kvf3, cw):
    bsz, t, _ = kvf3.shape
    n_sub = t // CMP_STRIDE
    kind = lambda a: pl.BlockSpec((1,) + a.shape[1:], lambda k, b: (k,) + (0,) * (a.ndim - 1))
    ws = [cw['wc'], cw['w1cat'], cw['pe'], cw['b1'], cw['w2'], cw['b2']]
    return pl.pallas_call(
        _compress_kernel,
        grid=(2, bsz),
        in_specs=[pl.BlockSpec((1, t, 128), lambda k, b: (b, 0, k))] + [kind(a) for a in ws],
        out_specs=[pl.BlockSpec((1, 1, NSA_KV, n_sub, NSA_DIM), lambda k, b: (k, b, 0, 0, 0)),
                   pl.BlockSpec((1, 1, NSA_KV, NSA_DIM, n_sub), lambda k, b: (k, b, 0, 0, 0))],
        out_shape=[jax.ShapeDtypeStruct((2, bsz, NSA_KV, n_sub, NSA_DIM), BF16),
                   jax.ShapeDtypeStruct((2, bsz, NSA_KV, NSA_DIM, n_sub), BF16)],
        compiler_params=_cparams(("arbitrary", "arbitrary")),
        name="nsa_compress",
    )(kvf3, *ws)


def _select_blocks(imps, qpos_r, n_pick):
    nb, cols = imps[0].shape
    blk = lax.broadcasted_iota(jnp.int32, (nb, cols), 0)
    cur = qpos_r // SEL_BLOCK
    avail = blk * SEL_BLOCK <= qpos_r
    forced = (blk == 0) | (blk == cur) | (blk == cur - 1)
    vals = tuple(jnp.where(avail, jnp.where(forced, 1e9, imp), -1.0) for imp in imps)

    def pick(_, carry):
        out = []
        for val, sel in zip(carry[0], carry[1]):
            mx = jnp.max(val, axis=0, keepdims=True)
            first = jnp.min(jnp.where(val == mx, blk, nb), axis=0, keepdims=True)
            hit = blk == first
            out.append((jnp.where(hit, -2.0, val), jnp.where(hit, 1.0, sel)))
        return tuple(o[0] for o in out), tuple(o[1] for o in out)

    _, sels = lax.fori_loop(0, n_pick, pick, (vals, tuple(jnp.zeros((nb, cols), F32) for _ in imps)))
    return [((sel - 1.0) * SEL_BIG).T for sel in sels]


def _nsa_cmp_kernel(q_ref, kc_ref, vct_ref, gt_ref, ovl_ref, o_ref, sel_ref, *, tq, n_pick, nc):
    i = pl.program_id(1)
    q = q_ref[0]
    gates_t = _sigmoid(gt_ref[0])
    n_pad = kc_ref.shape[3]
    qpos_r = i * tq + lax.broadcasted_iota(jnp.int32, (1, tq), 1)
    n_i = lax.broadcasted_iota(jnp.int32, (n_pad, 1), 0)
    valid = (n_i * CMP_STRIDE + (CMP_LEN - 1) <= qpos_r) & (n_i < nc)
    bias = jnp.where(valid, 0.0, NEG)
    any_valid = qpos_r >= CMP_LEN - 1
    imps = []
    for g in range(NSA_KV):
        q4 = jnp.concatenate([q[:, NSA_DIM * (g * NSA_GROUP + r):NSA_DIM * (g * NSA_GROUP + r + 1)]
                              for r in range(NSA_GROUP)], axis=0)
        s_all = _dot_nt(kc_ref[0, 0, g], q4)
        vct = vct_ref[0, 0, g]
        psum = None
        for r in range(NSA_GROUP):
            hd = g * NSA_GROUP + r
            s = s_all[:, r * tq:(r + 1) * tq] + bias
            m = jnp.max(s, axis=0, keepdims=True)
            p = jnp.exp2(s - m)
            d = jnp.sum(p, axis=0, keepdims=True)
            p = p * jnp.where(any_valid, 1.0 / d, 0.0)
            psum = p if psum is None else psum + p
            o_ref[0, NSA_DIM * hd:NSA_DIM * (hd + 1), :] = _dot(vct, p.astype(BF16)) * gates_t[3 * hd:3 * hd + 1, :]
        imps.append(sum(_dot(ovl_ref[...], part) for part in _split3(psum)))
    for g, selb in enumerate(_select_blocks(imps, qpos_r, n_pick)):
        sel_ref[0, g] = selb.astype(BF16)


def _nsa_cmp(qn, kvc, kvct, smt, ovl_t, tq, n_pick, nc):
    bsz, t, _ = qn.shape
    n_pad = kvc.shape[3]
    nb = ovl_t.shape[0]
    return pl.pallas_call(
        functools.partial(_nsa_cmp_kernel, tq=tq, n_pick=n_pick, nc=nc),
        grid=(bsz, t // tq),
        in_specs=[pl.BlockSpec((1, tq, NSA_WIDTH), lambda b, i: (b, i, 0)),
                  pl.BlockSpec((1, 1, NSA_KV, n_pad, NSA_DIM), lambda b, i: (0, b, 0, 0, 0)),
                  pl.BlockSpec((1, 1, NSA_KV, NSA_DIM, n_pad), lambda b, i: (1, b, 0, 0, 0)),
                  pl.BlockSpec((1, 32, tq), lambda b, i: (b, 0, i)),
                  pl.BlockSpec((nb, n_pad), lambda b, i: (0, 0))],
        out_specs=[pl.BlockSpec((1, NSA_WIDTH, tq), lambda b, i: (b, 0, i)),
                   pl.BlockSpec((1, NSA_KV, tq, nb), lambda b, i: (b, 0, i, 0))],
        out_shape=[jax.ShapeDtypeStruct((bsz, NSA_WIDTH, t), F32),
                   jax.ShapeDtypeStruct((bsz, NSA_KV, t, nb), BF16)],
        compiler_params=_cparams(("arbitrary", "arbitrary")),
        name="nsa_cmp_select",
    )(qn, kvc, kvct, smt, ovl_t)


def _flash_t(k_at, v_at, q_scr, lo, n_full, hi, mask_at, m_scr, acc_scr, p_scr, s_scr, tk):
    w = q_scr.shape[1]
    cw = min(w, 128)
    m_scr[...] = jnp.full(m_scr.shape, -3e38, F32)
    acc_scr[...] = jnp.zeros(acc_scr.shape, F32)

    rb = min(tk, 128)
    nrb = tk // rb

    def fold(x, op):
        return op(x.reshape(rb // 8, 8, cw), axis=0) if rb > 8 else x

    def make_body(masked):
        def body(j, carry):
            for g in range(NSA_KV):
                s_scr[g, 0:tk, :] = _dot_nt(k_at(g, j), q_scr[g])
            for g in range(NSA_KV):
                for c in range(w // cw):
                    cs = slice(c * cw, (c + 1) * cw)
                    m8 = None
                    for i in range(nrb):
                        rows = slice(i * rb, (i + 1) * rb)
                        s = s_scr[g, rows, cs]
                        if masked:
                            s = jnp.where(mask_at(j, rows, cs), s, NEG)
                            s_scr[g, rows, cs] = s
                        f = fold(s, jnp.max)
                        m8 = f if m8 is None else jnp.maximum(m8, f)
                    m_prev = m_scr[g, :, cs]
                    m_new = jnp.maximum(m_prev, jnp.max(m8, axis=0, keepdims=True))
                    alpha = jnp.exp2(m_prev - m_new)
                    for i in range(nrb):
                        rows = slice(i * rb, (i + 1) * rb)
                        p_scr[g, rows, cs] = jnp.exp2(s_scr[g, rows, cs] - m_new).astype(BF16)
                    m_scr[g, :, cs] = m_new
                    acc_scr[g, :, cs] = acc_scr[g, :, cs] * alpha
                v1 = jnp.concatenate([v_at(g, j), jnp.ones((8, tk), BF16)], axis=0)
                acc_scr[g] += _dot(v1, p_scr[g, 0:tk, :])
            return carry
        return body

    lax.fori_loop(lo, n_full, make_body(False), 0)
    lax.fori_loop(n_full, hi, make_body(True), 0)
    return [acc_scr[g, 0:NSA_DIM] / acc_scr[g, NSA_DIM:NSA_DIM + 1] for g in range(NSA_KV)]


def _nsa_sw_kernel(q_ref, ka_ref, kw_ref, vt_ref, sel_ref, gt_ref, oc_ref, o_ref,
                   qa_scr, qw_scr, m_scr, acc_scr, p_scr, s_scr, *, tq, tk, tkw):
    i = pl.program_id(1)
    w = NSA_GROUP * tq
    nb = sel_ref.shape[3]
    ka_w = qa_scr.shape[2]
    q = q_ref[0]
    gates_t = _sigmoid(gt_ref[0])
    q_first = i * tq
    q_last = q_first + tq - 1
    lane = lax.broadcasted_iota(jnp.int32, (1, w), 1)
    qpos = q_first + (lane & (tq - 1))
    if ka_w > nb + NSA_DIM:
        qa_scr[:, :, nb + NSA_DIM:] = jnp.zeros((NSA_KV, w, ka_w - nb - NSA_DIM), BF16)
    qw_scr[...] = jnp.zeros(qw_scr.shape, BF16)
    for g in range(NSA_KV):
        for r in range(NSA_GROUP):
            hd = g * NSA_GROUP + r
            rows = slice(r * tq, (r + 1) * tq)
            q_h = q[:, NSA_DIM * hd:NSA_DIM * (hd + 1)]
            qa_scr[g, rows, 0:nb] = sel_ref[0, g]
            qa_scr[g, rows, nb:nb + NSA_DIM] = q_h
            qw_scr[g, rows, NSA_DIM * g:NSA_DIM * (g + 1)] = q_h

    def ka_at(g, j):
        return ka_ref[0, pl.ds(pl.multiple_of(j * tk, tk), tk), ka_w * g:ka_w * (g + 1)]

    def vs_at(g, j):
        return vt_ref[0, NSA_DIM * g:NSA_DIM * (g + 1), pl.ds(pl.multiple_of(j * tk, tk), tk)]

    def kw_at(g, j):
        return kw_ref[0, pl.ds(pl.multiple_of(j * tkw, tkw), tkw), :]

    def vw_at(g, j):
        return vt_ref[0, 128 + NSA_DIM * g:128 + NSA_DIM * (g + 1), pl.ds(pl.multiple_of(j * tkw, tkw), tkw)]

    def key_pos(j, tile, rows):
        return j * tile + rows.start + lax.broadcasted_iota(jnp.int32, (rows.stop - rows.start, 1), 0)

    def slc_mask(j, rows, cs):
        return key_pos(j, tk, rows) <= qpos[:, cs]

    def win_mask(j, rows, cs):
        kpos = key_pos(j, tkw, rows)
        return (kpos <= qpos[:, cs]) & (kpos >= qpos[:, cs] - WINDOW)

    o_s = _flash_t(ka_at, vs_at, qa_scr, 0, (q_first + 1) // tk, q_last // tk + 1, slc_mask,
                   m_scr, acc_scr, p_scr, s_scr, tk)
    w_lo = jnp.maximum(q_first - WINDOW, 0) // tkw
    o_w = _flash_t(kw_at, vw_at, qw_scr, w_lo, w_lo, q_last // tkw + 1, win_mask,
                   m_scr, acc_scr, p_scr, s_scr, tkw)
    outs = []
    for hd in range(NSA_HEADS):
        g, r = divmod(hd, NSA_GROUP)
        cols = slice(r * tq, (r + 1) * tq)
        o_h = (oc_ref[0, NSA_DIM * hd:NSA_DIM * (hd + 1), :]
               + o_s[g][:, cols] * gates_t[3 * hd + 1:3 * hd + 2, :]
               + o_w[g][:, cols] * gates_t[3 * hd + 2:3 * hd + 3, :])
        outs.append(o_h.T)
    o_ref[0] = jnp.concatenate(outs, axis=1)


def _nsa_sw(qn, ka, kw, vt, selb, smt, o_cmp_t, tq, tk, tkw):
    bsz, t, _ = qn.shape
    nb = selb.shape[3]
    assert tq & (tq - 1) == 0
    w = NSA_GROUP * tq
    ka_w = ka.shape[2] // NSA_KV
    seq = lambda a: pl.BlockSpec((1,) + a.shape[1:], lambda b, i: (b, 0, 0))
    return pl.pallas_call(
        functools.partial(_nsa_sw_kernel, tq=tq, tk=tk, tkw=tkw),
        grid=(bsz, t // tq),
        in_specs=[pl.BlockSpec((1, tq, NSA_WIDTH), lambda b, i: (b, i, 0)),
                  seq(ka), seq(kw), seq(vt),
                  pl.BlockSpec((1, NSA_KV, tq, nb), lambda b, i: (b, 0, i, 0)),
                  pl.BlockSpec((1, 32, tq), lambda b, i: (b, 0, i)),
                  pl.BlockSpec((1, NSA_WIDTH, tq), lambda b, i: (b, 0, i))],
        out_specs=pl.BlockSpec((1, tq, NSA_WIDTH), lambda b, i: (b, i, 0)),
        out_shape=jax.ShapeDtypeStruct((bsz, t, NSA_WIDTH), F32),
        scratch_shapes=[pltpu.VMEM((NSA_KV, w, ka_w), BF16),
                        pltpu.VMEM((NSA_KV, w, 128), BF16),
                        pltpu.VMEM((NSA_KV, 1, w), F32),
                        pltpu.VMEM((NSA_KV, NSA_DIM + 8, w), F32),
                        pltpu.VMEM((NSA_KV, max(tk, tkw), w), BF16),
                        pltpu.VMEM((NSA_KV, max(tk, tkw), w), F32)],
        compiler_params=_cparams(("arbitrary", "arbitrary")),
        name="nsa_select_window",
    )(qn, ka, kw, vt, selb, smt, o_cmp_t)


def _ab_out_kernel(x_ref, mod_ref, hm_ref, w_ref, *rest):
    o_ref = rest[-1]
    o_nsa = rest[0][...]
    for part in rest[1:-1]:
        o_nsa = o_nsa + part[...]
    y = _dot(hm_ref[...], w_ref[:ML_WIDTH, :]) + _dot(o_nsa.astype(BF16), w_ref[ML_WIDTH:, :])
    o_ref[...] = x_ref[...] + mod_ref[0, 2] * y


def _ab_out(x, mod4, tiles_per_seq, hm, o_parts, w, tm):
    n = x.shape[0]
    tok = lambda wd: pl.BlockSpec((tm, wd), lambda i: (i, 0))
    return pl.pallas_call(
        _ab_out_kernel,
        grid=(n // tm,),
        in_specs=[tok(D_MODEL), _mod_spec(mod4, tm, tiles_per_seq), tok(ML_WIDTH),
                  pl.BlockSpec(w.shape, lambda i: (0, 0))] + [tok(NSA_WIDTH) for _ in o_parts],
        out_specs=tok(D_MODEL),
        out_shape=jax.ShapeDtypeStruct((n, D_MODEL), F32),
        compiler_params=_cparams(("arbitrary",)),
        name="ab_out",
    )(x, mod4, hm, w, *o_parts)


def _cl_kernel(x_ref, mod_ref, g_ref, wi_ref, bi_ref, vg_ref, ws_ref, bs_ref, wo_ref, o_ref, v_ref, *, tm):
    x = x_ref[...]
    h = _normmod(x, g_ref[...], mod_ref[0, 1], mod_ref[0, 0]).astype(BF16)
    z = _gelu(_dot(h, wi_ref[...]) + bi_ref[...])
    gw = z.shape[1] // 2
    u = z[:, :gw]
    v = z[:, gw:]
    v = v * lax.rsqrt(jnp.mean(v * v, axis=-1, keepdims=True) + EPS) * vg_ref[...]
    v_ref[...] = v
    vb = v.astype(BF16)
    lch = ws_ref.shape[1]
    tril = lax.broadcasted_iota(jnp.int32, (lch, lch), 0) >= lax.broadcasted_iota(jnp.int32, (lch, lch), 1)
    gd = gw // GM_GROUPS
    rows = []
    for c in range(tm // lch):
        cols = []
        for g in range(GM_GROUPS):
            wsg = jnp.where(tril, ws_ref[g], 0.0).astype(BF16)
            cols.append(_dot(wsg, vb[c * lch:(c + 1) * lch, g * gd:(g + 1) * gd]) + bs_ref[:, g:g + 1])
        rows.append(jnp.concatenate(cols, axis=1))
    s = jnp.concatenate(rows, axis=0) if len(rows) > 1 else rows[0]
    y = _dot((u * s).astype(BF16), wo_ref[...])
    o_ref[...] = x + mod_ref[0, 2] * y


def _cl_mixer(x, mod4, tiles_per_seq, g, wi, bi, vg, ws_eff, bs_eff, wo, tm):
    n = x.shape[0]
    gw = wo.shape[0]
    tok = lambda wd: pl.BlockSpec((tm, wd), lambda i: (i, 0))
    full = lambda a: pl.BlockSpec(a.shape, lambda i: (0,) * a.ndim)
    return pl.pallas_call(
        functools.partial(_cl_kernel, tm=tm),
        grid=(n // tm,),
        in_specs=[tok(D_MODEL), _mod_spec(mod4, tm, tiles_per_seq), full(g), full(wi), full(bi), full(vg),
                  full(ws_eff), full(bs_eff), full(wo)],
        out_specs=[tok(D_MODEL), tok(gw)],
        out_shape=[jax.ShapeDtypeStruct((n, D_MODEL), F32), jax.ShapeDtypeStruct((n, gw), F32)],
        compiler_params=_cparams(("arbitrary",)),
        name="gmlp_mixer",
    )(x, mod4, g, wi, bi, vg, ws_eff, bs_eff, wo)


def _ffn_kernel(x_ref, mod_ref, g_ref, w1_ref, w2_ref, fg_ref, o_ref, h_scr, acc_scr, *, final):
    j = pl.program_id(1)

    @pl.when(j == 0)
    def _():
        h_scr[...] = _normmod(x_ref[...], g_ref[...], mod_ref[0, 4], mod_ref[0, 3]).astype(BF16)
        acc_scr[...] = jnp.zeros(acc_scr.shape, F32)

    a = jnp.maximum(_dot(h_scr[...], w1_ref[...]), 0.0)
    acc_scr[...] += _dot((a * a).astype(BF16), w2_ref[...])

    @pl.when(j == pl.num_programs(1) - 1)
    def _():
        y = x_ref[...] + mod_ref[0, 5] * acc_scr[...]
        if final:
            y = y * lax.rsqrt(jnp.mean(y * y, axis=-1, keepdims=True) + EPS) * fg_ref[...]
        o_ref[...] = y


def _ffn(x, mod4, tiles_per_seq, g, w1, w2, final_g, final, tm, tf):
    n = x.shape[0]
    f = w1.shape[1]
    r = mod4.shape[2]
    if r == 1:
        mspec = pl.BlockSpec((1, 6, 1, D_MODEL), lambda i, j: (i // tiles_per_seq, 0, 0, 0))
    else:
        mspec = pl.BlockSpec((1, 6, tm, D_MODEL), lambda i, j: (i, 0, 0, 0))
    return pl.pallas_call(
        functools.partial(_ffn_kernel, final=final),
        grid=(n // tm, f // tf),
        in_specs=[pl.BlockSpec((tm, D_MODEL), lambda i, j: (i, 0)), mspec,
                  pl.BlockSpec((1, D_MODEL), lambda i, j: (0, 0)),
                  pl.BlockSpec((D_MODEL, tf), lambda i, j: (0, j)),
                  pl.BlockSpec((tf, D_MODEL), lambda i, j: (j, 0)),
                  pl.BlockSpec((1, D_MODEL), lambda i, j: (0, 0))],
        out_specs=pl.BlockSpec((tm, D_MODEL), lambda i, j: (i, 0)),
        out_shape=jax.ShapeDtypeStruct((n, D_MODEL), F32),
        scratch_shapes=[pltpu.VMEM((tm, D_MODEL), BF16), pltpu.VMEM((tm, D_MODEL), F32)],
        compiler_params=_cparams(("arbitrary", "arbitrary")),
        name="ffn",
    )(x, mod4, g, w1, w2, final_g)


SQ = 16


def _page_specs(pages_per_step, row_block):
    def in_map(p):
        return lambda b, s, pt: (pt[b, s * pages_per_step + p], row_block, 0)
    return [pl.BlockSpec((1, 256, PAGE_SIZE), in_map(p)) for p in range(pages_per_step)]


def _decode_cmp_kernel(pt_ref, *refs, pages_per_step, past, n_pick):
    pages = refs[:pages_per_step]
    (q_ref, sm_ref, ovl_ref, wc_ref, w1_ref, pe_ref, b1_ref, w2_ref, b2_ref, o_ref, sel_ref, x_scr) = refs[pages_per_step:]
    s_idx = pl.program_id(1)
    for p in range(pages_per_step):
        row0 = pl.multiple_of((s_idx * pages_per_step + p) * PAGE_SIZE, PAGE_SIZE)
        for kind in range(2):
            x_scr[kind, pl.ds(row0, PAGE_SIZE), :] = pages[p][0, 128 * kind:128 * (kind + 1), :].T

    @pl.when(s_idx == pl.num_programs(1) - 1)
    def _():
        n_sub = past // CMP_STRIDE
        nc = n_sub - 1
        kv = [_compress_math(lambda p: x_scr[kind, pl.ds(p, n_sub, stride=CMP_STRIDE), :],
                             lambda p: wc_ref[kind, p], w1_ref[kind], pe_ref[kind], b1_ref[kind],
                             w2_ref[kind], b2_ref[kind], n_sub) for kind in range(2)]
        q = q_ref[0]
        gates = _sigmoid(sm_ref[0])
        qpos = past + lax.broadcasted_iota(jnp.int32, (SQ, 1), 0)
        n_i = lax.broadcasted_iota(jnp.int32, (1, n_sub), 1)
        valid = (n_i * CMP_STRIDE + (CMP_LEN - 1) <= qpos) & (n_i < nc)
        outs, imps = [], []
        for g in range(NSA_KV):
            kc = kv[0][g].astype(BF16)
            vc = kv[1][g].astype(BF16)
            psum = jnp.zeros((SQ, n_sub), F32)
            for r in range(NSA_GROUP):
                hd = g * NSA_GROUP + r
                s = _dot_nt(q[:, NSA_DIM * hd:NSA_DIM * (hd + 1)], kc)
                s = jnp.where(valid, s, NEG)
                m = jnp.max(s, axis=1, keepdims=True)
                p = jnp.where(valid, jnp.exp2(s - m), 0.0)
                d = jnp.sum(p, axis=1, keepdims=True)
                p = p / jnp.where(d > 0, d, 1.0)
                psum = psum + p
                outs.append(_dot(p.astype(BF16), vc) * gates[:, 3 * hd:3 * hd + 1])
            psum = jnp.concatenate([psum, jnp.zeros((128 - SQ, n_sub), F32)], axis=0)
            imps.append(sum(_dot_nt(ovl_ref[...], part) for part in _split3(psum)))
        qpos_r = past + lax.broadcasted_iota(jnp.int32, (1, 128), 1)
        for g, selb in enumerate(_select_blocks(imps, qpos_r, n_pick)):
            sel_ref[0, g] = selb[0:SQ]
        o_ref[0] = jnp.concatenate(outs, axis=1)


def _decode_cmp(cache_t, page_table, qn, sm, ovl, cw, pages_per_step, n_pick):
    bsz, n_pages = page_table.shape
    past = n_pages * PAGE_SIZE
    nb = ovl.shape[0]
    full = lambda a: pl.BlockSpec(a.shape, lambda b, s, pt: (0,) * a.ndim)
    seq = lambda a: pl.BlockSpec((1,) + a.shape[1:], lambda b, s, pt: (b,) + (0,) * (a.ndim - 1))
    ws = [cw['wc'], cw['w1cat'], cw['pe'], cw['b1'], cw['w2'], cw['b2']]
    return pl.pallas_call(
        functools.partial(_decode_cmp_kernel, pages_per_step=pages_per_step, past=past, n_pick=n_pick),
        grid_spec=pltpu.PrefetchScalarGridSpec(
            num_scalar_prefetch=1,
            grid=(bsz, n_pages // pages_per_step),
            in_specs=_page_specs(pages_per_step, 0) + [seq(qn), seq(sm), full(ovl)] + [full(a) for a in ws],
            out_specs=[pl.BlockSpec((1, SQ, NSA_WIDTH), lambda b, s, pt: (b, 0, 0)),
                       pl.BlockSpec((1, NSA_KV, SQ, nb), lambda b, s, pt: (b, 0, 0, 0))],
            scratch_shapes=[pltpu.VMEM((2, past, 128), F32)]),
        out_shape=[jax.ShapeDtypeStruct((bsz, SQ, NSA_WIDTH), F32),
                   jax.ShapeDtypeStruct((bsz, NSA_KV, SQ, nb), F32)],
        compiler_params=_cparams(("arbitrary", "arbitrary")),
        name="decode_compress_select",
    )(page_table, *([cache_t] * pages_per_step), qn, sm, ovl, *ws)


def _decode_sw_kernel(pt_ref, *refs, pages_per_step, past, n_new):
    pages = refs[:pages_per_step]
    (et_ref, q_ref, sel_ref, kvn_ref, win_ref, sm_ref, o_ref, qa_scr, m_scr, l_scr, acc_scr) = refs[pages_per_step:]
    s_idx = pl.program_id(1)
    nb = sel_ref.shape[3]
    rows_n = NSA_HEADS * SQ
    wlen = win_ref.shape[2]

    @pl.when(s_idx == 0)
    def _():
        q = q_ref[0]
        qa_scr[...] = jnp.zeros(qa_scr.shape, F32)
        for hd in range(NSA_HEADS):
            g = hd // NSA_GROUP
            rows = slice(hd * SQ, (hd + 1) * SQ)
            qa_scr[rows, NSA_DIM * g:NSA_DIM * (g + 1)] = q[:, NSA_DIM * hd:NSA_DIM * (hd + 1)].astype(F32)
            qa_scr[rows, 128:128 + nb] = sel_ref[0, g]
        m_scr[...] = jnp.full(m_scr.shape, -3e38, F32)
        l_scr[...] = jnp.zeros(l_scr.shape, F32)
        acc_scr[...] = jnp.zeros(acc_scr.shape, F32)

    qa = qa_scr[...].astype(BF16)

    def online(s, pv):
        m_prev = m_scr[...]
        m_new = jnp.maximum(m_prev, jnp.max(s, axis=1, keepdims=True))
        alpha = jnp.exp2(m_prev - m_new)
        p = jnp.exp2(s - m_new[:, 0:1])
        l_scr[...] = alpha * l_scr[...] + jnp.sum(p, axis=1, keepdims=True)
        acc_scr[...] = alpha * acc_scr[...] + pv(p.astype(BF16))
        m_scr[...] = m_new

    for pair in range(pages_per_step // 2):
        pa, pb = pages[2 * pair], pages[2 * pair + 1]
        kt = jnp.concatenate([pa[0, 0:128, :], pb[0, 0:128, :]], axis=1).astype(BF16)
        vt = jnp.concatenate([pa[0, 128:256, :], pb[0, 128:256, :]], axis=1).astype(BF16)
        off = pl.multiple_of((s_idx * pages_per_step + 2 * pair) * PAGE_SIZE, 2 * PAGE_SIZE)
        rhs = jnp.concatenate([kt, et_ref[:, pl.ds(off, 2 * PAGE_SIZE)]], axis=0)
        online(_dot(qa, rhs), lambda p: _dot_nt(p, vt))

    @pl.when(s_idx == pl.num_programs(1) - 1)
    def _():
        tok = lax.broadcasted_iota(jnp.int32, (rows_n, 1), 0) & (SQ - 1)
        new_i = lax.broadcasted_iota(jnp.int32, (1, SQ), 1)
        new_ok = (new_i <= tok) & (new_i < n_new)
        kvn = kvn_ref[0]
        q2 = qa[:, 0:128]
        online(jnp.where(new_ok, _dot_nt(q2, kvn[:, 256:384]), NEG), lambda p: _dot(p, kvn[:, 384:512]))
        o_s = acc_scr[...] / l_scr[...]
        s_w = _dot(q2, win_ref[0, 0:128, :].astype(BF16))
        w_i = lax.broadcasted_iota(jnp.int32, (1, wlen), 1)
        s_w = jnp.where(w_i >= tok + (wlen - WINDOW), s_w, NEG)
        s_n = jnp.where(new_ok, _dot_nt(q2, kvn[:, 512:640]), NEG)
        m_w = jnp.maximum(jnp.max(s_w, axis=1, keepdims=True), jnp.max(s_n, axis=1, keepdims=True))
        p_w = jnp.exp2(s_w - m_w)
        p_n = jnp.exp2(s_n - m_w)
        l_w = jnp.sum(p_w, axis=1, keepdims=True) + jnp.sum(p_n, axis=1, keepdims=True)
        o_w = (_dot_nt(p_w.astype(BF16), win_ref[0, 128:256, :].astype(BF16))
               + _dot(p_n.astype(BF16), kvn[:, 640:768])) / l_w
        gates = _sigmoid(sm_ref[0])
        g_s = jnp.concatenate([gates[:, 3 * hd + 1:3 * hd + 2] for hd in range(NSA_HEADS)], axis=0)
        g_w = jnp.concatenate([gates[:, 3 * hd + 2:3 * hd + 3] for hd in range(NSA_HEADS)], axis=0)
        o = o_s * g_s + o_w * g_w
        half = rows_n // 2
        o_ref[0, 0:half, :] = o[0:half, 0:NSA_DIM]
        o_ref[0, half:, :] = o[half:, NSA_DIM:]


def _decode_sw(cache_t, page_table, et, qn, selb, kvn, win_t, sm, pages_per_step, n_new):
    bsz, n_pages = page_table.shape
    past = n_pages * PAGE_SIZE
    nb = et.shape[0]
    rows_n = NSA_HEADS * SQ
    full = lambda a: pl.BlockSpec(a.shape, lambda b, s, pt: (0,) * a.ndim)
    seq = lambda a: pl.BlockSpec((1,) + a.shape[1:], lambda b, s, pt: (b,) + (0,) * (a.ndim - 1))
    return pl.pallas_call(
        functools.partial(_decode_sw_kernel, pages_per_step=pages_per_step, past=past, n_new=n_new),
        grid_spec=pltpu.PrefetchScalarGridSpec(
            num_scalar_prefetch=1,
            grid=(bsz, n_pages // pages_per_step),
            in_specs=_page_specs(pages_per_step, 1) + [full(et), seq(qn), seq(selb), seq(kvn), seq(win_t), seq(sm)],
            out_specs=pl.BlockSpec((1, rows_n, NSA_DIM), lambda b, s, pt: (b, 0, 0)),
            scratch_shapes=[pltpu.VMEM((rows_n, 128 + nb), F32),
                            pltpu.VMEM((rows_n, 128), F32),
                            pltpu.VMEM((rows_n, 128), F32),
                            pltpu.VMEM((rows_n, 128), F32)]),
        out_shape=jax.ShapeDtypeStruct((bsz, rows_n, NSA_DIM), F32),
        compiler_params=_cparams(("arbitrary", "arbitrary")),
        name="decode_select_window",
    )(page_table, *([cache_t] * pages_per_step), et, qn, selb, kvn, win_t, sm)


def _selection_constants(n_keys):
    nb = n_keys // SEL_BLOCK
    n_sub = n_keys // CMP_STRIDE
    i = np.arange(n_sub)[None, :] * CMP_STRIDE
    j = np.arange(nb)[:, None] * SEL_BLOCK
    ovl_t = ((i < j + SEL_BLOCK) & (i + CMP_LEN > j) & (np.arange(n_sub)[None, :] < n_sub - 1))
    et = (np.arange(n_keys)[None, :] // SEL_BLOCK) == np.arange(nb)[:, None]
    return jnp.asarray(ovl_t, BF16), jnp.asarray(et, BF16)


def _ab_layer_prompt(x, mod4, tps, W, l, bsz, t, cfg):
    n = bsz * t
    j = l // 2
    tm, lc = cfg['tm'], cfg['lc']
    qk, v_m, o_pre, qn, kvf, sm, ka, kw, vt, smt = _ab_in(x, mod4, tps, W['norm_g0'][l], W['ab_w_in'][j],
                                                          W['ab_b_in'][j], tm, seq_shape=(bsz, t))
    seq = lambda a: a.reshape(bsz, t, a.shape[1])
    conv0 = jnp.zeros((bsz, CONV_W - 1, 2 * ML_WIDTH), F32)
    c0 = jnp.zeros((bsz, ML_HEADS, ML_DIM, ML_DIM), F32)
    n0 = jnp.zeros((bsz, ML_HEADS, ML_DIM), F32)
    m0 = jnp.zeros((bsz, ML_HEADS), F32)
    hm, c1, n1, m1 = _mlstm(seq(qk), seq(v_m), seq(o_pre), seq(sm), smt, W['ml_conv_w'][j], W['ml_conv_b'][j],
                            W['ml_f_bias'][j], W['ml_out_g'][j], conv0, c0, n0, m0, lc, gr_block=_SM_I // 8)
    conv_new = seq(qk)[:, -(CONV_W - 1):]
    kv_rows = kvf[:, :512].reshape(bsz, t, 4, NSA_KV, NSA_DIM)
    win_rows = kvf[:, 512:].reshape(bsz, t, 2, NSA_KV, NSA_DIM)
    kvc, kvct = _compress(seq(kvf), W['cmp'][j])
    ovl_t, _ = _selection_constants(t)
    o_cmp_t, selb = _nsa_cmp(seq(qn), kvc, kvct, smt, ovl_t, cfg['tq_cmp'], N_SEL, t // CMP_STRIDE - 1)
    o_nsa = _nsa_sw(seq(qn), seq(ka), seq(kw), vt, selb, smt, o_cmp_t, cfg['tq'], cfg['tk'], cfg['tkw'])
    x = _ab_out(x, mod4, tps, hm.reshape(n, ML_WIDTH), [o_nsa.reshape(n, NSA_WIDTH)], W['ab_w_out'][j], tm)
    return x, (conv_new, c1, n1, m1, kv_rows, win_rows[:, -min(WINDOW, t):])


def _ab_layer_decode(x, mod4, W, l, bsz, t, cfg, st, page_table):
    n = bsz * t
    j = l // 2
    tm, lc = cfg['tm'], cfg['lc']
    conv0, c0, n0, m0, (cache_t, page0), win_buf, win_t = st
    page_table = page_table + page0
    qk, v_m, o_pre, qn, kvf, sm, kvb = _ab_in(x, mod4, 1, W['norm_g0'][l], W['ab_w_in'][j], W['ab_b_in'][j], tm)
    seq = lambda a: a.reshape(bsz, t, a.shape[1])
    pad_t = lambda a, tp: jnp.pad(a, ((0, 0), (0, tp - t), (0, 0)))
    sm3 = seq(sm)
    gr = sm3[:, :, _SM_I:_SM_I + 8].transpose(0, 2, 1)
    gr = jnp.concatenate([jnp.pad(gr[:, :4], ((0, 0), (0, 0), (0, lc - t)), constant_values=NEG),
                          jnp.pad(gr[:, 4:], ((0, 0), (0, 0), (0, lc - t)), constant_values=-NEG)], axis=1)
    sm_pad = jnp.zeros((bsz, lc - t, 128), F32).at[:, :, _SM_I:_SM_I + 4].set(NEG).at[:, :, _SM_F:_SM_F + 4].set(-NEG)
    hm, c1, n1, m1 = _mlstm(pad_t(seq(qk), lc), pad_t(seq(v_m), lc), pad_t(seq(o_pre), lc),
                            jnp.concatenate([sm3, sm_pad], axis=1), gr, W['ml_conv_w'][j], W['ml_conv_b'][j],
                            W['ml_f_bias'][j], W['ml_out_g'][j], conv0, c0, n0, m0, lc)
    hm = hm[:, :t].reshape(n, ML_WIDTH)
    conv_new = jnp.concatenate([conv0, seq(qk)], axis=1)[:, -(CONV_W - 1):]
    kv_rows = kvf[:, :512].reshape(bsz, t, 4, NSA_KV, NSA_DIM)
    win_rows = kvf[:, 512:].reshape(bsz, t, 2, NSA_KV, NSA_DIM)
    past = page_table.shape[1] * PAGE_SIZE
    assert (past + t) // CMP_STRIDE == past // CMP_STRIDE and t <= min(SEL_BLOCK, SQ) and past % SEL_BLOCK == 0
    ovl_t, et = _selection_constants(past)
    qn3, sm3q, kvn = pad_t(seq(qn), SQ), pad_t(sm3, SQ), pad_t(seq(kvb), SQ)
    pps = cfg['pages_per_step']
    o_cmp, selb = _decode_cmp(cache_t, page_table, qn3, sm3q, ovl_t, W['cmp'][j], pps, N_SEL - 1)
    o_sw = _decode_sw(cache_t, page_table, et, qn3, selb, kvn, win_t, sm3q, pps, t)
    o_cmp = o_cmp[:, :t].reshape(n, NSA_WIDTH)
    o_sw = o_sw.reshape(bsz, NSA_HEADS, SQ, NSA_DIM).transpose(0, 2, 1, 3)[:, :t].reshape(n, NSA_WIDTH)
    x = _ab_out(x, mod4, 1, hm, [o_cmp, o_sw], W['ab_w_out'][j], tm)
    win_new = jnp.concatenate([win_buf, win_rows], axis=1)[:, -win_buf.shape[1]:]
    return x, (conv_new, c1, n1, m1, kv_rows, win_new)


def _trunk(x3, mods, W, state, page_table, cfg):
    bsz, t, _ = x3.shape
    n = bsz * t
    tm = cfg['tm']
    x = x3.reshape(n, D_MODEL)
    depth = mods.shape[0]
    ab_new, cl_new = [], []
    if t % tm == 0:
        tps = t // tm
        to_mod4 = lambda m: m.reshape(bsz, 6, 1, D_MODEL)
    else:
        assert n == tm
        tps = 1
        to_mod4 = lambda m: jnp.repeat(m.reshape(bsz, 6, D_MODEL), t, axis=0).reshape(n, 6, D_MODEL).transpose(1, 0, 2)[None]
    lch = min(GM_CHUNK, t)
    for l in range(depth):
        mod4 = to_mod4(mods[l])
        j = l // 2
        if l % 2 == 0:
            if state is None:
                x, new = _ab_layer_prompt(x, mod4, tps, W, l, bsz, t, cfg)
            else:
                x, new = _ab_layer_decode(x, mod4, W, l, bsz, t, cfg, tuple(a[j] for a in state), page_table)
            ab_new.append(new)
        else:
            ws = W['cl_ws'][j][:, :lch, :lch]
            bs = W['cl_bs'][j][:, :lch]
            if lch < GM_CHUNK:
                rep = GM_CHUNK // lch
                ws = jnp.einsum('ab,gts->gatbs', jnp.eye(rep, dtype=F32), ws).reshape(GM_GROUPS, GM_CHUNK, GM_CHUNK)
                bs = jnp.tile(bs, (1, rep))
            x, v = _cl_mixer(x, mod4, tps, W['norm_g0'][l], W['cl_w_in'][j], W['cl_b_in'][j], W['cl_v_g'][j],
                             ws, bs.T, W['cl_w_out'][j], tm)
            cl_new.append(v.reshape(bsz, t, -1))
        x = _ffn(x, mod4, max(t // cfg['tm_ffn'], 1), W['norm_g1'][l], W['ffn_w1'][l], W['ffn_w2'][l], W['final_g'],
                 l == depth - 1, cfg['tm_ffn'], cfg['tf'])
    return x.reshape(bsz, t, D_MODEL), ab_new, cl_new


def kernel(x_prompt, x_sample, c_prompt, c_sample, state_mlstm_conv, state_mlstm_C, state_mlstm_n,
           state_mlstm_m, cache_nsa_kv, state_nsa_win, page_table, ada_w, ada_b, norm_g, ab_w_in, ab_b_in,
           ml_conv_w, ml_conv_b, ml_f_bias, ml_out_g, phi_pe, phi_w1, phi_b1, phi_w2, phi_b2, ab_w_out,
           cl_w_in, cl_b_in, cl_v_g, cl_ws, cl_bs, cl_w_out, ffn_w1, ffn_w2, final_g):
    depth = ada_w.shape[0]
    n_ab = ab_w_in.shape[0]
    bp, bs_ = c_prompt.shape[0], c_sample.shape[0]
    rows = bp + bs_
    rows_pad = -(-rows // 8) * 8
    c_all = jnp.concatenate([c_prompt, c_sample, jnp.zeros((rows_pad - rows, D_MODEL), F32)], axis=0)
    mods = _ada_mod(c_all, ada_w, ada_b)
    mods_p = mods[:, :bp]
    mods_s = mods[:, bp:rows]
    w_in = jnp.concatenate([ab_w_in[:, :, 0:2048], ab_w_in[:, :, 2056:3336], ab_w_in[:, :, 3336:3360],
                            ab_w_in[:, :, 2048:2056], jnp.zeros((n_ab, D_MODEL, 96), F32)], axis=2).astype(BF16)
    b_in = jnp.concatenate([ab_b_in[:, 0:2048], ab_b_in[:, 2056:3336], ab_b_in[:, 3336:3360],
                            ab_b_in[:, 2048:2056], jnp.zeros((n_ab, 96), F32)], axis=1)[:, None, :]
    half = CMP_STRIDE * NSA_DIM
    blk = phi_w1.reshape(n_ab, 2, 2, CMP_STRIDE, NSA_DIM, NSA_DIM).transpose(0, 1, 3, 4, 2, 5)
    blk = blk.reshape(n_ab, 2, CMP_STRIDE, NSA_DIM, 2 * NSA_DIM)
    zero = jnp.zeros_like(blk)
    wc = jnp.concatenate([jnp.concatenate([blk, zero], axis=4), jnp.concatenate([zero, blk], axis=4)], axis=3)
    pe = jnp.pad(phi_pe.reshape(n_ab, 2, 2, half), ((0, 0), (0, 0), (0, 6), (0, 0)))
    cmp_w = [dict(wc=wc[j].astype(BF16),
                  w1cat=jnp.concatenate([phi_w1[j, :, :half], phi_w1[j, :, half:]], axis=2).astype(BF16),
                  pe=pe[j], b1=phi_b1[j][:, None, :], w2=phi_w2[j].astype(BF16), b2=phi_b2[j][:, None, :])
             for j in range(n_ab)]
    W = dict(
        norm_g0=norm_g[:, 0][:, None, :], norm_g1=norm_g[:, 1][:, None, :],
        ab_w_in=w_in, ab_b_in=b_in,
        ml_conv_w=ml_conv_w, ml_conv_b=ml_conv_b[:, None, :], ml_f_bias=ml_f_bias, ml_out_g=ml_out_g[:, None, :],
        cmp=cmp_w,
        ab_w_out=ab_w_out.astype(BF16),
        cl_w_in=cl_w_in.astype(BF16), cl_b_in=cl_b_in[:, None, :], cl_v_g=cl_v_g[:, None, :],
        cl_ws=cl_ws, cl_bs=cl_bs, cl_w_out=cl_w_out.astype(BF16),
        ffn_w1=ffn_w1.astype(BF16), ffn_w2=ffn_w2.astype(BF16), final_g=final_g[None, :])
    cfg_p = dict(tm=256, tm_ffn=512, tf=512, lc=256, tq_cmp=128, tq=256, tk=512, tkw=256)
    y_prompt, ab_p, _ = _trunk(x_prompt, mods_p, W, None, None, cfg_p)
    n_s = x_sample.shape[0] * x_sample.shape[1]
    cfg_s = dict(tm=n_s, tm_ffn=n_s, tf=512, lc=128, pages_per_step=8)
    n_pool = cache_nsa_kv.shape[1]
    cache_t = cache_nsa_kv.transpose(0, 1, 3, 4, 5, 2).reshape(n_ab * n_pool, 4 * NSA_KV * NSA_DIM, PAGE_SIZE)
    cache_t = [(cache_t, j * n_pool) for j in range(n_ab)]
    wlen = state_nsa_win.shape[2]
    win_t = state_nsa_win.transpose(0, 1, 3, 4, 5, 2).reshape(n_ab, bs_, 2 * NSA_KV * NSA_DIM, wlen)
    state = (state_mlstm_conv, state_mlstm_C, state_mlstm_n, state_mlstm_m, cache_t, state_nsa_win, win_t)
    y_sample, ab_s, cl_s = _trunk(x_sample, mods_s, W, state, page_table, cfg_s)
    p_out = [jnp.stack(a) for a in zip(*ab_p)]
    s_out = [jnp.stack(a) for a in zip(*ab_s)]
    return (y_prompt, y_sample, *p_out, *s_out, jnp.stack(cl_s))
```

```python
import functools

import numpy as np
import jax
import jax.numpy as jnp
from jax import lax
from jax.experimental import pallas as pl
from jax.experimental.pallas import tpu as pltpu

F32 = jnp.float32
BF16 = jnp.bfloat16

EPS = 1e-6
D_MODEL = 1024
ML_HEADS = 4
ML_DIM = 128
ML_WIDTH = ML_HEADS * ML_DIM
CONV_W = 4
NSA_HEADS = 8
NSA_KV = 2
NSA_GROUP = NSA_HEADS // NSA_KV
NSA_DIM = 64
NSA_WIDTH = NSA_HEADS * NSA_DIM
CMP_LEN = 32
CMP_STRIDE = 16
SEL_BLOCK = 64
N_SEL = 16
WINDOW = 512
GM_GROUPS = 4
GM_CHUNK = 128
PAGE_SIZE = 128

_SEG_QK = (0, 1024)
_SEG_V = (1024, 1536)
_SEG_O = (1536, 2048)
_SEG_QN = (2048, 2560)
_SEG_KV = (2560, 3328)
_SEG_SM = (3328, 3456)
_AB_COLS = 3456
_SM_I = 24
_SM_F = 28

NEG = -1e30
SEL_BIG = 2.0 ** 100
LOG2E = 1.4426950408889634
VMEM_LIMIT = 56 * 1024 * 1024


def _cparams(sem):
    return pltpu.CompilerParams(dimension_semantics=sem, vmem_limit_bytes=VMEM_LIMIT)


def _sigmoid(x):
    return 1.0 / (1.0 + jnp.exp(-x))


def _log_sigmoid(x):
    return jnp.minimum(x, 0.0) - jnp.log1p(jnp.exp(-jnp.abs(x)))


def _gelu(x):
    return 0.5 * x * (1.0 + jnp.tanh(0.7978845608028654 * (x + 0.044715 * (x * x * x))))


def _normmod(x, g, scale, shift):
    y = x * lax.rsqrt(jnp.mean(x * x, axis=-1, keepdims=True) + EPS) * g
    return y * (1.0 + scale) + shift


def _dot(a, b):
    return jnp.dot(a, b, preferred_element_type=F32)


def _dot_nt(a, b):
    return lax.dot_general(a, b, (((1,), (1,)), ((), ())), preferred_element_type=F32)


def _split3(x):
    hi = x.astype(BF16)
    r = x - hi.astype(F32)
    mid = r.astype(BF16)
    lo = (r - mid.astype(F32)).astype(BF16)
    return hi, mid, lo


def _ada_kernel(c_ref, w_ref, b_ref, o_ref):
    c = c_ref[...]
    sc = c * _sigmoid(c)
    sc_hi = sc.astype(BF16)
    sc_lo = (sc - sc_hi.astype(F32)).astype(BF16)
    w = w_ref[0]
    w_hi = w.astype(BF16)
    w_lo = (w - w_hi.astype(F32)).astype(BF16)
    acc = _dot(sc_hi, w_hi) + _dot(sc_lo, w_hi) + _dot(sc_hi, w_lo)
    o_ref[0] = acc + b_ref[0]


def _ada_mod(c_all, ada_w, ada_b):
    depth, d, n = ada_w.shape
    bp = c_all.shape[0]
    tn = 1536
    return pl.pallas_call(
        _ada_kernel,
        grid=(depth, n // tn),
        in_specs=[pl.BlockSpec((bp, d), lambda l, j: (0, 0)),
                  pl.BlockSpec((1, d, tn), lambda l, j: (l, 0, j)),
                  pl.BlockSpec((1, 1, tn), lambda l, j: (l, 0, j))],
        out_specs=pl.BlockSpec((1, bp, tn), lambda l, j: (l, 0, j)),
        out_shape=jax.ShapeDtypeStruct((depth, bp, n), F32),
        compiler_params=_cparams(("arbitrary", "arbitrary")),
        name="ada_mod",
    )(c_all, ada_w, ada_b.reshape(depth, 1, n))


def _mod_spec(mod4, tm, tiles_per_seq):
    r = mod4.shape[2]
    if r == 1:
        return pl.BlockSpec((1, 6, 1, D_MODEL), lambda i: (i // tiles_per_seq, 0, 0, 0))
    assert r == tm
    return pl.BlockSpec((1, 6, tm, D_MODEL), lambda i: (i, 0, 0, 0))


def _ab_in_kernel(x_ref, mod_ref, g_ref, w_ref, b_ref, *rest, tm, tiles_per_seq, nb, seq_layouts):
    h = _normmod(x_ref[...], g_ref[...], mod_ref[0, 1], mod_ref[0, 0]).astype(BF16)

    def seg(ab):
        a, b = ab
        return _dot(h, w_ref[:, a:b]) + b_ref[:, a:b]

    if seq_layouts:
        wt_ref, bt_ref, qk_ref, v_ref, o_ref, qn_ref, kvr_ref, win_ref, sm_ref, ka_ref, kw_ref, vt_ref, smt_ref = rest
    else:
        qk_ref, v_ref, o_ref, qn_ref, kvr_ref, win_ref, sm_ref, kvb_ref = rest
    qk_ref[...] = seg(_SEG_QK)
    v_ref[...] = seg(_SEG_V).astype(BF16)
    o_ref[...] = seg(_SEG_O)
    qn_ref[...] = (seg(_SEG_QN) * (NSA_DIM ** -0.5 * LOG2E)).astype(BF16)
    kv = seg(_SEG_KV)
    kvr_ref[...] = kv[:, :512]
    win_ref[...] = kv[:, 512:]
    sm_ref[...] = seg(_SEG_SM)
    if not seq_layouts:
        kvb_ref[...] = kv.astype(BF16)
        return
    pos = (pl.program_id(0) % tiles_per_seq) * tm + lax.broadcasted_iota(jnp.int32, (tm, nb), 0)
    onehot = jnp.where(pos // SEL_BLOCK == lax.broadcasted_iota(jnp.int32, (tm, nb), 1), 1.0, 0.0).astype(BF16)
    ka_w = ka_ref.shape[1] // NSA_KV
    pieces = []
    for g in range(NSA_KV):
        pieces += [onehot, kv[:, 256 + NSA_DIM * g:256 + NSA_DIM * (g + 1)].astype(BF16)]
        if ka_w > nb + NSA_DIM:
            pieces.append(jnp.zeros((tm, ka_w - nb - NSA_DIM), BF16))
    ka_ref[...] = jnp.concatenate(pieces, axis=1)
    kw_ref[...] = kv[:, 512:640].astype(BF16)
    zt = _dot_nt(wt_ref[...], h) + bt_ref[...]
    vt_ref[0] = zt[:256].astype(BF16)
    smt_ref[0] = zt[256:]


def _ab_in(x, mod4, tiles_per_seq, g, w, b, tm, seq_shape=None):
    n = x.shape[0]
    widths = [(1024, F32), (512, BF16), (512, F32), (512, BF16), (512, F32), (256, F32), (128, F32)]
    ins = [x, mod4, g, w, b]
    in_specs = [pl.BlockSpec((tm, D_MODEL), lambda i: (i, 0)),
                _mod_spec(mod4, tm, tiles_per_seq),
                pl.BlockSpec((1, D_MODEL), lambda i: (0, 0)),
                pl.BlockSpec((D_MODEL, _AB_COLS), lambda i: (0, 0)),
                pl.BlockSpec((1, _AB_COLS), lambda i: (0, 0))]
    nb = 0
    if seq_shape is None:
        widths.append((768, BF16))
    else:
        bsz, t = seq_shape
        nb = t // SEL_BLOCK
        ka_w = -(-(nb + NSA_DIM) // 128) * 128
        widths += [(NSA_KV * ka_w, BF16), (128, BF16)]
        a, c = _SEG_KV[0], _SEG_SM[0]
        cols = jnp.concatenate([w[:, a + 384:a + 512], w[:, a + 640:a + 768], w[:, c:c + 128]], axis=1)
        bcols = jnp.concatenate([b[:, a + 384:a + 512], b[:, a + 640:a + 768], b[:, c:c + 128]], axis=1)
        ins += [cols.T, bcols.T]
        in_specs += [pl.BlockSpec((384, D_MODEL), lambda i: (0, 0)), pl.BlockSpec((384, 1), lambda i: (0, 0))]
    out_specs = [pl.BlockSpec((tm, wd), lambda i: (i, 0)) for wd, _ in widths]
    out_shape = [jax.ShapeDtypeStruct((n, wd), dt) for wd, dt in widths]
    if seq_shape is not None:
        seq_map = lambda i: (i // tiles_per_seq, 0, i % tiles_per_seq)
        out_specs += [pl.BlockSpec((1, 256, tm), seq_map), pl.BlockSpec((1, 128, tm), seq_map)]
        out_shape += [jax.ShapeDtypeStruct((bsz, 256, t), BF16), jax.ShapeDtypeStruct((bsz, 128, t), F32)]
    return pl.pallas_call(
        functools.partial(_ab_in_kernel, tm=tm, tiles_per_seq=tiles_per_seq, nb=nb,
                          seq_layouts=seq_shape is not None),
        grid=(n // tm,),
        in_specs=in_specs,
        out_specs=out_specs,
        out_shape=out_shape,
        compiler_params=_cparams(("arbitrary",)),
        name="ab_in",
    )(*ins)


def _mlstm_kernel(qk_ref, v_ref, o_ref, sm_ref, gr_ref, cw_ref, cb_ref, fbc_ref, fbr_ref, og_ref,
                  conv0_ref, c0_ref, n0_ref, m0_ref,
                  hm_ref, cout_ref, nout_ref, mout_ref,
                  xp_scr, c_scr, n_scr, m_scr, *, lc):
    c = pl.program_id(1)

    @pl.when(c == 0)
    def _():
        xp_scr[0:8, :] = conv0_ref[0]
        c_scr[...] = c0_ref[0]
        n_scr[...] = n0_ref[0]
        m_scr[...] = m0_ref[0]

    xp_scr[8:8 + lc, :] = qk_ref[0]
    y = cb_ref[...]
    for j in range(CONV_W):
        y = y + xp_scr[5 + j:5 + j + lc, :] * cw_ref[j:j + 1, :]
    xp_scr[0:8, :] = xp_scr[lc:lc + 8, :]
    act = y * _sigmoid(y)
    q_all = act[:, :ML_WIDTH]
    k_all = act[:, ML_WIDTH:] * (ML_DIM ** -0.5)

    sm = sm_ref[0]
    gr = gr_ref[0]
    lf_cols = _log_sigmoid(sm + fbc_ref[...])
    lf_rows = _log_sigmoid(gr + fbr_ref[...])
    v_all = v_ref[0]
    o_all = o_ref[0]

    row_i = lax.broadcasted_iota(jnp.int32, (lc, lc), 0)
    col_i = lax.broadcasted_iota(jnp.int32, (lc, lc), 1)
    tril = row_i >= col_i
    triu = row_i <= col_i

    outs = []
    for h in range(ML_HEADS):
        hs = slice(ML_DIM * h, ML_DIM * (h + 1))
        li_c = sm[:, _SM_I + h:_SM_I + h + 1]
        lf_c = lf_cols[:, _SM_F + h:_SM_F + h + 1]
        li_r = gr[h:h + 1, :]
        lf_r = lf_rows[4 + h:5 + h, :]
        b_c = jnp.sum(jnp.where(tril, lf_r, 0.0), axis=1, keepdims=True)
        b_r = jnp.sum(jnp.where(triu, lf_c, 0.0), axis=0, keepdims=True)
        a_r = li_r - b_r
        a_c = li_c - b_c
        m_h = m_scr[h:h + 1, 0:1]
        cm_c = jnp.maximum(m_h, jnp.max(jnp.where(tril, a_r, NEG), axis=1, keepdims=True))
        dm = jnp.exp(jnp.where(tril, a_r - cm_c, NEG))
        w_int = jnp.exp(m_h - cm_c)
        qh = q_all[:, hs]
        kh = k_all[:, hs]
        vh = v_all[:, hs]
        qb = qh.astype(BF16)
        kb = kh.astype(BF16)
        s = _dot_nt(qb, kb) * dm
        c_old = c_scr[h]
        n_old = n_scr[h:h + 1, :]
        num = w_int * _dot(qb, c_old.astype(BF16)) + _dot(s.astype(BF16), vh)
        den = w_int * jnp.sum(qh * n_old, axis=1, keepdims=True) + jnp.sum(s, axis=1, keepdims=True)
        mt = b_c + cm_c
        hh = num / jnp.maximum(jnp.abs(den), jnp.exp(-mt))
        hn = hh * lax.rsqrt(jnp.mean(hh * hh, axis=1, keepdims=True) + EPS) * og_ref[:, hs]
        outs.append(hn * _sigmoid(o_all[:, hs]))
        cm_last = jnp.maximum(m_h, jnp.max(a_r, axis=1, keepdims=True))
        bl = jnp.sum(lf_r, axis=1, keepdims=True)
        decay = jnp.exp(m_h - cm_last)
        ws_c = jnp.exp(a_c - cm_last)
        kt = kh.T.astype(BF16)
        c_scr[h] = decay * c_old + _dot(kt, (ws_c * vh.astype(F32)).astype(BF16))
        n_scr[h:h + 1, :] = decay * n_old + jnp.sum(ws_c * kh, axis=0, keepdims=True)
        m_scr[h:h + 1, :] = jnp.broadcast_to(bl + cm_last, (1, ML_DIM))

    hm_ref[0] = jnp.concatenate(outs, axis=1).astype(BF16)

    @pl.when(c == pl.num_programs(1) - 1)
    def _():
        cout_ref[0] = c_scr[...]
        nout_ref[0] = n_scr[...]
        mout_ref[0] = m_scr[...]


def _mlstm(qk, v, o, sm, gr, conv_w, conv_b, f_bias, out_g, conv0, c0, n0, m0, lc, gr_block=0):
    bsz, t, _ = qk.shape
    nc = t // lc
    fbc = jnp.zeros((1, 128), F32).at[0, _SM_F:_SM_F + ML_HEADS].set(f_bias)
    fbr = jnp.zeros((8, 1), F32).at[4:8, 0].set(f_bias)
    conv0p = jnp.concatenate([jnp.zeros((bsz, 5, 2 * ML_WIDTH), F32), conv0], axis=1)
    m0p = jnp.broadcast_to(m0[:, :, None], (bsz, ML_HEADS, ML_DIM))
    tok = lambda wd: pl.BlockSpec((1, lc, wd), lambda b, c: (b, c, 0))
    full2 = lambda a: pl.BlockSpec(a.shape, lambda b, c: (0, 0))
    st3 = lambda a: pl.BlockSpec((1,) + a.shape[1:], lambda b, c: (b,) + (0,) * (a.ndim - 1))
    hm, c1, n1, m1 = pl.pallas_call(
        functools.partial(_mlstm_kernel, lc=lc),
        grid=(bsz, nc),
        in_specs=[tok(1024), tok(512), tok(512), tok(128),
                  pl.BlockSpec((1, 8, lc), lambda b, c: (b, gr_block, c)),
                  full2(conv_w), full2(conv_b), full2(fbc), full2(fbr), full2(out_g),
                  st3(conv0p), st3(c0), st3(n0), st3(m0p)],
        out_specs=[tok(512), st3(c0), st3(n0), st3(m0p)],
        out_shape=[jax.ShapeDtypeStruct((bsz, t, ML_WIDTH), BF16),
                   jax.ShapeDtypeStruct(c0.shape, F32),
                   jax.ShapeDtypeStruct(n0.shape, F32),
                   jax.ShapeDtypeStruct(m0p.shape, F32)],
        scratch_shapes=[pltpu.VMEM((lc + 8, 2 * ML_WIDTH), F32),
                        pltpu.VMEM((ML_HEADS, ML_DIM, ML_DIM), F32),
                        pltpu.VMEM((ML_HEADS, ML_DIM), F32),
                        pltpu.VMEM((ML_HEADS, ML_DIM), F32)],
        compiler_params=_cparams(("arbitrary", "arbitrary")),
        name="mlstm",
    )(qk, v, o, sm, gr, conv_w, conv_b, fbc, fbr, out_g, conv0p, c0, n0, m0p)
    return hm, c1, n1, m1[:, :, 0]


def _compress_math(x_at, wc_at, w1cat, pe, b1, w2, b2, n_sub):
    acc = None
    for p in range(CMP_STRIDE):
        d = _dot(x_at(p).astype(BF16), wc_at(p))
        acc = d if acc is None else acc + d
    pe_hi = pe.astype(BF16)
    pe_lo = (pe - pe_hi.astype(F32)).astype(BF16)
    pe_c = _dot(pe_hi, w1cat) + _dot(pe_lo, w1cat)
    const = b1 + pe_c[0:1, :NSA_DIM] + pe_c[1:2, NSA_DIM:]
    outs = []
    for g in range(NSA_KV):
        p0 = acc[:, 128 * g:128 * g + NSA_DIM]
        p1 = pltpu.roll(acc[:, 128 * g + NSA_DIM:128 * (g + 1)], n_sub - 1, 0)
        outs.append(_dot(_gelu(const + p0 + p1).astype(BF16), w2) + b2)
    return outs


def _compress_kernel(x_ref, wc_ref, w1_ref, pe_ref, b1_ref, w2_ref, b2_ref, o_ref, ot_ref):
    n_sub = o_ref.shape[3]
    outs = _compress_math(lambda p: x_ref[0, pl.ds(p, n_sub, stride=CMP_STRIDE), :], lambda p: wc_ref[0, p],
                          w1_ref[0], pe_ref[0], b1_ref[0], w2_ref[0], b2_ref[0], n_sub)
    for g in range(NSA_KV):
        o_ref[0, 0, g] = outs[g].astype(BF16)
        ot_ref[0, 0, g] = outs[g].T.astype(BF16)


def _compress(kvr3, cw):
    bsz, t, _ = kvr3.shape
    n_sub = t // CMP_STRIDE
    kind = lambda a: pl.BlockSpec((1,) + a.shape[1:], lambda k, b: (k,) + (0,) * (a.ndim - 1))
    ws = [cw['wc'], cw['w1cat'], cw['pe'], cw['b1'], cw['w2'], cw['b2']]
    return pl.pallas_call(
        _compress_kernel,
        grid=(2, bsz),
        in_specs=[pl.BlockSpec((1, t, 128), lambda k, b: (b, 0, k))] + [kind(a) for a in ws],
        out_specs=[pl.BlockSpec((1, 1, NSA_KV, n_sub, NSA_DIM), lambda k, b: (k, b, 0, 0, 0)),
                   pl.BlockSpec((1, 1, NSA_KV, NSA_DIM, n_sub), lambda k, b: (k, b, 0, 0, 0))],
        out_shape=[jax.ShapeDtypeStruct((2, bsz, NSA_KV, n_sub, NSA_DIM), BF16),
                   jax.ShapeDtypeStruct((2, bsz, NSA_KV, NSA_DIM, n_sub), BF16)],
        compiler_params=_cparams(("arbitrary", "arbitrary")),
        name="nsa_compress",
    )(kvr3, *ws)


def _select_blocks(imps, qpos_r, n_pick):
    nb, cols = imps[0].shape
    blk = lax.broadcasted_iota(jnp.int32, (nb, cols), 0)
    cur = qpos_r // SEL_BLOCK
    avail = blk * SEL_BLOCK <= qpos_r
    forced = (blk == 0) | (blk == cur) | (blk == cur - 1)
    vals = tuple(jnp.where(avail, jnp.where(forced, 1e9, imp), -1.0) for imp in imps)

    def pick(_, carry):
        out = []
        for val, sel in zip(carry[0], carry[1]):
            mx = jnp.max(val, axis=0, keepdims=True)
            first = jnp.min(jnp.where(val == mx, blk, nb), axis=0, keepdims=True)
            hit = blk == first
            out.append((jnp.where(hit, -2.0, val), jnp.where(hit, 1.0, sel)))
        return tuple(o[0] for o in out), tuple(o[1] for o in out)

    _, sels = lax.fori_loop(0, n_pick, pick, (vals, tuple(jnp.zeros((nb, cols), F32) for _ in imps)))
    return [(sel - 1.0) * SEL_BIG for sel in sels]


def _nsa_cmp_kernel(q_ref, kc_ref, vct_ref, gt_ref, ovl_ref, o_ref, sel_ref, *, tq, n_pick, nc):
    i = pl.program_id(1)
    q = q_ref[0]
    gates_t = _sigmoid(gt_ref[0])
    n_pad = kc_ref.shape[3]
    qpos_r = i * tq + lax.broadcasted_iota(jnp.int32, (1, tq), 1)
    n_i = lax.broadcasted_iota(jnp.int32, (n_pad, 1), 0)
    valid = (n_i * CMP_STRIDE + (CMP_LEN - 1) <= qpos_r) & (n_i < nc)
    bias = jnp.where(valid, 0.0, NEG)
    any_valid = qpos_r >= CMP_LEN - 1
    imps = []
    for g in range(NSA_KV):
        q4 = jnp.concatenate([q[:, NSA_DIM * (g * NSA_GROUP + r):NSA_DIM * (g * NSA_GROUP + r + 1)]
                              for r in range(NSA_GROUP)], axis=0)
        s_all = _dot_nt(kc_ref[0, 0, g], q4)
        vct = vct_ref[0, 0, g]
        psum = None
        for r in range(NSA_GROUP):
            hd = g * NSA_GROUP + r
            s = s_all[:, r * tq:(r + 1) * tq] + bias
            m = jnp.max(s, axis=0, keepdims=True)
            p = jnp.exp2(s - m)
            d = jnp.sum(p, axis=0, keepdims=True)
            p = p * jnp.where(any_valid, 1.0 / d, 0.0)
            psum = p if psum is None else psum + p
            o_ref[0, NSA_DIM * hd:NSA_DIM * (hd + 1), :] = _dot(vct, p.astype(BF16)) * gates_t[3 * hd:3 * hd + 1, :]
        imps.append(sum(_dot(ovl_ref[...], part) for part in _split3(psum)))
    for g, selb in enumerate(_select_blocks(imps, qpos_r, n_pick)):
        sel_ref[0, g] = selb.astype(BF16)


def _nsa_cmp(qn, kvc, kvct, smt, ovl_t, tq, n_pick, nc):
    bsz, t, _ = qn.shape
    n_pad = kvc.shape[3]
    nb = ovl_t.shape[0]
    return pl.pallas_call(
        functools.partial(_nsa_cmp_kernel, tq=tq, n_pick=n_pick, nc=nc),
        grid=(bsz, t // tq),
        in_specs=[pl.BlockSpec((1, tq, NSA_WIDTH), lambda b, i: (b, i, 0)),
                  pl.BlockSpec((1, 1, NSA_KV, n_pad, NSA_DIM), lambda b, i: (0, b, 0, 0, 0)),
                  pl.BlockSpec((1, 1, NSA_KV, NSA_DIM, n_pad), lambda b, i: (1, b, 0, 0, 0)),
                  pl.BlockSpec((1, 32, tq), lambda b, i: (b, 0, i)),
                  pl.BlockSpec((nb, n_pad), lambda b, i: (0, 0))],
        out_specs=[pl.BlockSpec((1, NSA_WIDTH, tq), lambda b, i: (b, 0, i)),
                   pl.BlockSpec((1, NSA_KV, nb, tq), lambda b, i: (b, 0, 0, i))],
        out_shape=[jax.ShapeDtypeStruct((bsz, NSA_WIDTH, t), F32),
                   jax.ShapeDtypeStruct((bsz, NSA_KV, nb, t), BF16)],
        compiler_params=_cparams(("arbitrary", "arbitrary")),
        name="nsa_cmp_select",
    )(qn, kvc, kvct, smt, ovl_t)


def _flash_t(k_at, v_at, qt_scr, lo, n_full, hi, mask_at, m_scr, acc_scr, p_scr, s_scr, tk):
    w = qt_scr.shape[2]
    cw = min(w, 128)
    m_scr[...] = jnp.full(m_scr.shape, -3e38, F32)
    acc_scr[...] = jnp.zeros(acc_scr.shape, F32)

    rb = min(tk, 128)
    nrb = tk // rb

    def fold(x, op):
        return op(x.reshape(rb // 8, 8, cw), axis=0) if rb > 8 else x

    def make_body(masked):
        def body(j, carry):
            for g in range(NSA_KV):
                s_scr[g, 0:tk, :] = _dot(k_at(g, j), qt_scr[g])
            for g in range(NSA_KV):
                for c in range(w // cw):
                    cs = slice(c * cw, (c + 1) * cw)
                    m8 = None
                    for i in range(nrb):
                        rows = slice(i * rb, (i + 1) * rb)
                        s = s_scr[g, rows, cs]
                        if masked:
                            s = jnp.where(mask_at(j, rows, c * cw, cw), s, NEG)
                            s_scr[g, rows, cs] = s
                        f = fold(s, jnp.max)
                        m8 = f if m8 is None else jnp.maximum(m8, f)
                    m_prev = m_scr[g, :, cs]
                    m_new = jnp.maximum(m_prev, jnp.max(m8, axis=0, keepdims=True))
                    alpha = jnp.exp2(m_prev - m_new)
                    for i in range(nrb):
                        rows = slice(i * rb, (i + 1) * rb)
                        p_scr[g, rows, cs] = jnp.exp2(s_scr[g, rows, cs] - m_new).astype(BF16)
                    m_scr[g, :, cs] = m_new
                    acc_scr[g, :, cs] = acc_scr[g, :, cs] * alpha
                v1 = jnp.concatenate([v_at(g, j), jnp.ones((8, tk), BF16)], axis=0)
                acc_scr[g] += _dot(v1, p_scr[g, 0:tk, :])
            return carry
        return body

    lax.fori_loop(lo, n_full, make_body(False), 0)
    lax.fori_loop(n_full, hi, make_body(True), 0)
    return [acc_scr[g, 0:NSA_DIM] / acc_scr[g, NSA_DIM:NSA_DIM + 1] for g in range(NSA_KV)]


def _nsa_sw_kernel(q_ref, ka_ref, kw_ref, vt_ref, sel_ref, gt_ref, oc_ref, o_ref,
                   qa_scr, qw_scr, m_scr, acc_scr, p_scr, s_scr, *, tq, tk, tkw):
    i = pl.program_id(1)
    w = NSA_GROUP * tq
    nb = sel_ref.shape[2]
    ka_w = qa_scr.shape[1]
    qt = q_ref[0].astype(F32).T.astype(BF16)
    gates_t = _sigmoid(gt_ref[0])
    q_first = i * tq
    q_last = q_first + tq - 1
    if ka_w > nb + NSA_DIM:
        qa_scr[:, nb + NSA_DIM:, :] = jnp.zeros((NSA_KV, ka_w - nb - NSA_DIM, w), BF16)
    qw_scr[...] = jnp.zeros(qw_scr.shape, BF16)
    for g in range(NSA_KV):
        for r in range(NSA_GROUP):
            hd = g * NSA_GROUP + r
            cols = slice(r * tq, (r + 1) * tq)
            q_h = qt[NSA_DIM * hd:NSA_DIM * (hd + 1), :]
            qa_scr[g, 0:nb, cols] = sel_ref[0, g]
            qa_scr[g, nb:nb + NSA_DIM, cols] = q_h
            qw_scr[g, NSA_DIM * g:NSA_DIM * (g + 1), cols] = q_h

    def ka_at(g, j):
        return ka_ref[0, pl.ds(pl.multiple_of(j * tk, tk), tk), ka_w * g:ka_w * (g + 1)]

    def vs_at(g, j):
        return vt_ref[0, NSA_DIM * g:NSA_DIM * (g + 1), pl.ds(pl.multiple_of(j * tk, tk), tk)]

    def kw_at(g, j):
        return kw_ref[0, pl.ds(pl.multiple_of(j * tkw, tkw), tkw), :]

    def vw_at(g, j):
        return vt_ref[0, 128 + NSA_DIM * g:128 + NSA_DIM * (g + 1), pl.ds(pl.multiple_of(j * tkw, tkw), tkw)]

    def key_pos(j, tile, rows):
        return j * tile + rows.start + lax.broadcasted_iota(jnp.int32, (rows.stop - rows.start, 1), 0)

    def query_pos(col0, cols):
        return q_first + ((col0 + lax.broadcasted_iota(jnp.int32, (1, cols), 1)) & (tq - 1))

    def slc_mask(j, rows, col0, cols):
        return key_pos(j, tk, rows) <= query_pos(col0, cols)

    def win_mask(j, rows, col0, cols):
        kpos = key_pos(j, tkw, rows)
        qp = query_pos(col0, cols)
        return (kpos <= qp) & (kpos >= qp - WINDOW)

    o_s = _flash_t(ka_at, vs_at, qa_scr, 0, (q_first + 1) // tk, q_last // tk + 1, slc_mask,
                   m_scr, acc_scr, p_scr, s_scr, tk)
    w_lo = jnp.maximum(q_first - WINDOW, 0) // tkw
    o_w = _flash_t(kw_at, vw_at, qw_scr, w_lo, w_lo, q_last // tkw + 1, win_mask,
                   m_scr, acc_scr, p_scr, s_scr, tkw)
    outs = []
    for hd in range(NSA_HEADS):
        g, r = divmod(hd, NSA_GROUP)
        cols = slice(r * tq, (r + 1) * tq)
        o_h = (oc_ref[0, NSA_DIM * hd:NSA_DIM * (hd + 1), :]
               + o_s[g][:, cols] * gates_t[3 * hd + 1:3 * hd + 2, :]
               + o_w[g][:, cols] * gates_t[3 * hd + 2:3 * hd + 3, :])
        outs.append(o_h.T)
    o_ref[0] = jnp.concatenate(outs, axis=1)


def _nsa_sw(qn, ka, kw, vt, selb, smt, o_cmp_t, tq, tk, tkw):
    bsz, t, _ = qn.shape
    nb = selb.shape[2]
    assert tq & (tq - 1) == 0
    w = NSA_GROUP * tq
    ka_w = ka.shape[2] // NSA_KV
    seq = lambda a: pl.BlockSpec((1,) + a.shape[1:], lambda b, i: (b, 0, 0))
    return pl.pallas_call(
        functools.partial(_nsa_sw_kernel, tq=tq, tk=tk, tkw=tkw),
        grid=(bsz, t // tq),
        in_specs=[pl.BlockSpec((1, tq, NSA_WIDTH), lambda b, i: (b, i, 0)),
                  seq(ka), seq(kw), seq(vt),
                  pl.BlockSpec((1, NSA_KV, nb, tq), lambda b, i: (b, 0, 0, i)),
                  pl.BlockSpec((1, 32, tq), lambda b, i: (b, 0, i)),
                  pl.BlockSpec((1, NSA_WIDTH, tq), lambda b, i: (b, 0, i))],
        out_specs=pl.BlockSpec((1, tq, NSA_WIDTH), lambda b, i: (b, i, 0)),
        out_shape=jax.ShapeDtypeStruct((bsz, t, NSA_WIDTH), F32),
        scratch_shapes=[pltpu.VMEM((NSA_KV, ka_w, w), BF16),
                        pltpu.VMEM((NSA_KV, 128, w), BF16),
                        pltpu.VMEM((NSA_KV, 1, w), F32),
                        pltpu.VMEM((NSA_KV, NSA_DIM + 8, w), F32),
                        pltpu.VMEM((NSA_KV, max(tk, tkw), w), BF16),
                        pltpu.VMEM((NSA_KV, max(tk, tkw), w), F32)],
        compiler_params=_cparams(("arbitrary", "arbitrary")),
        name="nsa_select_window",
    )(qn, ka, kw, vt, selb, smt, o_cmp_t)


def _ab_out_kernel(x_ref, mod_ref, hm_ref, w_ref, *rest):
    o_ref = rest[-1]
    o_nsa = rest[0][...]
    for part in rest[1:-1]:
        o_nsa = o_nsa + part[...]
    y = _dot(hm_ref[...], w_ref[:ML_WIDTH, :]) + _dot(o_nsa.astype(BF16), w_ref[ML_WIDTH:, :])
    o_ref[...] = x_ref[...] + mod_ref[0, 2] * y


def _ab_out(x, mod4, tiles_per_seq, hm, o_parts, w, tm):
    n = x.shape[0]
    tok = lambda wd: pl.BlockSpec((tm, wd), lambda i: (i, 0))
    return pl.pallas_call(
        _ab_out_kernel,
        grid=(n // tm,),
        in_specs=[tok(D_MODEL), _mod_spec(mod4, tm, tiles_per_seq), tok(ML_WIDTH),
                  pl.BlockSpec(w.shape, lambda i: (0, 0))] + [tok(NSA_WIDTH) for _ in o_parts],
        out_specs=tok(D_MODEL),
        out_shape=jax.ShapeDtypeStruct((n, D_MODEL), F32),
        compiler_params=_cparams(("arbitrary",)),
        name="ab_out",
    )(x, mod4, hm, w, *o_parts)


def _cl_kernel(x_ref, mod_ref, g_ref, wi_ref, bi_ref, vg_ref, ws_ref, bs_ref, wo_ref, o_ref, v_ref, *, tm):
    x = x_ref[...]
    h = _normmod(x, g_ref[...], mod_ref[0, 1], mod_ref[0, 0]).astype(BF16)
    z = _gelu(_dot(h, wi_ref[...]) + bi_ref[...])
    gw = z.shape[1] // 2
    u = z[:, :gw]
    v = z[:, gw:]
    v = v * lax.rsqrt(jnp.mean(v * v, axis=-1, keepdims=True) + EPS) * vg_ref[...]
    v_ref[...] = v
    vb = v.astype(BF16)
    lch = ws_ref.shape[1]
    tril = lax.broadcasted_iota(jnp.int32, (lch, lch), 0) >= lax.broadcasted_iota(jnp.int32, (lch, lch), 1)
    gd = gw // GM_GROUPS
    rows = []
    for c in range(tm // lch):
        cols = []
        for g in range(GM_GROUPS):
            wsg = jnp.where(tril, ws_ref[g], 0.0).astype(BF16)
            cols.append(_dot(wsg, vb[c * lch:(c + 1) * lch, g * gd:(g + 1) * gd]) + bs_ref[:, g:g + 1])
        rows.append(jnp.concatenate(cols, axis=1))
    s = jnp.concatenate(rows, axis=0) if len(rows) > 1 else rows[0]
    y = _dot((u * s).astype(BF16), wo_ref[...])
    o_ref[...] = x + mod_ref[0, 2] * y


def _cl_mixer(x, mod4, tiles_per_seq, g, wi, bi, vg, ws_eff, bs_eff, wo, tm):
    n = x.shape[0]
    gw = wo.shape[0]
    tok = lambda wd: pl.BlockSpec((tm, wd), lambda i: (i, 0))
    full = lambda a: pl.BlockSpec(a.shape, lambda i: (0,) * a.ndim)
    return pl.pallas_call(
        functools.partial(_cl_kernel, tm=tm),
        grid=(n // tm,),
        in_specs=[tok(D_MODEL), _mod_spec(mod4, tm, tiles_per_seq), full(g), full(wi), full(bi), full(vg),
                  full(ws_eff), full(bs_eff), full(wo)],
        out_specs=[tok(D_MODEL), tok(gw)],
        out_shape=[jax.ShapeDtypeStruct((n, D_MODEL), F32), jax.ShapeDtypeStruct((n, gw), F32)],
        compiler_params=_cparams(("arbitrary",)),
        name="gmlp_mixer",
    )(x, mod4, g, wi, bi, vg, ws_eff, bs_eff, wo)


def _ffn_kernel(x_ref, mod_ref, g_ref, w1_ref, w2_ref, fg_ref, o_ref, h_scr, acc_scr, *, final):
    j = pl.program_id(1)

    @pl.when(j == 0)
    def _():
        h_scr[...] = _normmod(x_ref[...], g_ref[...], mod_ref[0, 4], mod_ref[0, 3]).astype(BF16)
        acc_scr[...] = jnp.zeros(acc_scr.shape, F32)

    a = jnp.maximum(_dot(h_scr[...], w1_ref[...]), 0.0)
    acc_scr[...] += _dot((a * a).astype(BF16), w2_ref[...])

    @pl.when(j == pl.num_programs(1) - 1)
    def _():
        y = x_ref[...] + mod_ref[0, 5] * acc_scr[...]
        if final:
            y = y * lax.rsqrt(jnp.mean(y * y, axis=-1, keepdims=True) + EPS) * fg_ref[...]
        o_ref[...] = y


def _ffn(x, mod4, tiles_per_seq, g, w1, w2, final_g, final, tm, tf):
    n = x.shape[0]
    f = w1.shape[1]
    r = mod4.shape[2]
    if r == 1:
        mspec = pl.BlockSpec((1, 6, 1, D_MODEL), lambda i, j: (i // tiles_per_seq, 0, 0, 0))
    else:
        mspec = pl.BlockSpec((1, 6, tm, D_MODEL), lambda i, j: (i, 0, 0, 0))
    return pl.pallas_call(
        functools.partial(_ffn_kernel, final=final),
        grid=(n // tm, f // tf),
        in_specs=[pl.BlockSpec((tm, D_MODEL), lambda i, j: (i, 0)), mspec,
                  pl.BlockSpec((1, D_MODEL), lambda i, j: (0, 0)),
                  pl.BlockSpec((D_MODEL, tf), lambda i, j: (0, j)),
                  pl.BlockSpec((tf, D_MODEL), lambda i, j: (j, 0)),
                  pl.BlockSpec((1, D_MODEL), lambda i, j: (0, 0))],
        out_specs=pl.BlockSpec((tm, D_MODEL), lambda i, j: (i, 0)),
        out_shape=jax.ShapeDtypeStruct((n, D_MODEL), F32),
        scratch_shapes=[pltpu.VMEM((tm, D_MODEL), BF16), pltpu.VMEM((tm, D_MODEL), F32)],
        compiler_params=_cparams(("arbitrary", "arbitrary")),
        name="ffn",
    )(x, mod4, g, w1, w2, final_g)


SQ = 16


def _page_specs(pages_per_step, row_block):
    def in_map(p):
        return lambda b, s, pt: (pt[b, s * pages_per_step + p], row_block, 0)
    return [pl.BlockSpec((1, 256, PAGE_SIZE), in_map(p)) for p in range(pages_per_step)]


def _decode_cmp_kernel(pt_ref, *refs, pages_per_step, past, n_pick):
    pages = refs[:pages_per_step]
    (q_ref, sm_ref, ovl_ref, wc_ref, w1_ref, pe_ref, b1_ref, w2_ref, b2_ref, o_ref, sel_ref, x_scr) = refs[pages_per_step:]
    s_idx = pl.program_id(1)
    for p in range(pages_per_step):
        row0 = pl.multiple_of((s_idx * pages_per_step + p) * PAGE_SIZE, PAGE_SIZE)
        for kind in range(2):
            x_scr[kind, pl.ds(row0, PAGE_SIZE), :] = pages[p][0, 128 * kind:128 * (kind + 1), :].T

    @pl.when(s_idx == pl.num_programs(1) - 1)
    def _():
        n_sub = past // CMP_STRIDE
        nc = n_sub - 1
        kv = [_compress_math(lambda p: x_scr[kind, pl.ds(p, n_sub, stride=CMP_STRIDE), :],
                             lambda p: wc_ref[kind, p], w1_ref[kind], pe_ref[kind], b1_ref[kind],
                             w2_ref[kind], b2_ref[kind], n_sub) for kind in range(2)]
        q = q_ref[0]
        gates = _sigmoid(sm_ref[0])
        qpos = past + lax.broadcasted_iota(jnp.int32, (SQ, 1), 0)
        n_i = lax.broadcasted_iota(jnp.int32, (1, n_sub), 1)
        valid = (n_i * CMP_STRIDE + (CMP_LEN - 1) <= qpos) & (n_i < nc)
        outs, imps = [], []
        for g in range(NSA_KV):
            kc = kv[0][g].astype(BF16)
            vc = kv[1][g].astype(BF16)
            psum = jnp.zeros((SQ, n_sub), F32)
            for r in range(NSA_GROUP):
                hd = g * NSA_GROUP + r
                s = _dot_nt(q[:, NSA_DIM * hd:NSA_DIM * (hd + 1)], kc)
                s = jnp.where(valid, s, NEG)
                m = jnp.max(s, axis=1, keepdims=True)
                p = jnp.where(valid, jnp.exp2(s - m), 0.0)
                d = jnp.sum(p, axis=1, keepdims=True)
                p = p / jnp.where(d > 0, d, 1.0)
                psum = psum + p
                outs.append(_dot(p.astype(BF16), vc) * gates[:, 3 * hd:3 * hd + 1])
            psum = jnp.concatenate([psum, jnp.zeros((128 - SQ, n_sub), F32)], axis=0)
            imps.append(sum(_dot_nt(ovl_ref[...], part) for part in _split3(psum)))
        qpos_r = past + lax.broadcasted_iota(jnp.int32, (1, 128), 1)
        for g, selb in enumerate(_select_blocks(imps, qpos_r, n_pick)):
            sel_ref[0, g] = selb.T[0:SQ]
        o_ref[0] = jnp.concatenate(outs, axis=1)


def _decode_cmp(cache_t, page_table, qn, sm, ovl, cw, pages_per_step, n_pick):
    bsz, n_pages = page_table.shape
    past = n_pages * PAGE_SIZE
    nb = ovl.shape[0]
    full = lambda a: pl.BlockSpec(a.shape, lambda b, s, pt: (0,) * a.ndim)
    seq = lambda a: pl.BlockSpec((1,) + a.shape[1:], lambda b, s, pt: (b,) + (0,) * (a.ndim - 1))
    ws = [cw['wc'], cw['w1cat'], cw['pe'], cw['b1'], cw['w2'], cw['b2']]
    return pl.pallas_call(
        functools.partial(_decode_cmp_kernel, pages_per_step=pages_per_step, past=past, n_pick=n_pick),
        grid_spec=pltpu.PrefetchScalarGridSpec(
            num_scalar_prefetch=1,
            grid=(bsz, n_pages // pages_per_step),
            in_specs=_page_specs(pages_per_step, 0) + [seq(qn), seq(sm), full(ovl)] + [full(a) for a in ws],
            out_specs=[pl.BlockSpec((1, SQ, NSA_WIDTH), lambda b, s, pt: (b, 0, 0)),
                       pl.BlockSpec((1, NSA_KV, SQ, nb), lambda b, s, pt: (b, 0, 0, 0))],
            scratch_shapes=[pltpu.VMEM((2, past, 128), F32)]),
        out_shape=[jax.ShapeDtypeStruct((bsz, SQ, NSA_WIDTH), F32),
                   jax.ShapeDtypeStruct((bsz, NSA_KV, SQ, nb), F32)],
        compiler_params=_cparams(("arbitrary", "arbitrary")),
        name="decode_compress_select",
    )(page_table, *([cache_t] * pages_per_step), qn, sm, ovl, *ws)


def _decode_sw_kernel(pt_ref, *refs, pages_per_step, past, n_new):
    pages = refs[:pages_per_step]
    (et_ref, q_ref, sel_ref, kvn_ref, win_ref, sm_ref, o_ref, qa_scr, m_scr, l_scr, acc_scr) = refs[pages_per_step:]
    s_idx = pl.program_id(1)
    nb = sel_ref.shape[3]
    rows_n = NSA_HEADS * SQ
    wlen = win_ref.shape[2]

    @pl.when(s_idx == 0)
    def _():
        q = q_ref[0]
        qa_scr[...] = jnp.zeros(qa_scr.shape, F32)
        for hd in range(NSA_HEADS):
            g = hd // NSA_GROUP
            rows = slice(hd * SQ, (hd + 1) * SQ)
            qa_scr[rows, NSA_DIM * g:NSA_DIM * (g + 1)] = q[:, NSA_DIM * hd:NSA_DIM * (hd + 1)].astype(F32)
            qa_scr[rows, 128:128 + nb] = sel_ref[0, g]
        m_scr[...] = jnp.full(m_scr.shape, -3e38, F32)
        l_scr[...] = jnp.zeros(l_scr.shape, F32)
        acc_scr[...] = jnp.zeros(acc_scr.shape, F32)

    qa = qa_scr[...].astype(BF16)

    def online(s, pv):
        m_prev = m_scr[...]
        m_new = jnp.maximum(m_prev, jnp.max(s, axis=1, keepdims=True))
        alpha = jnp.exp2(m_prev - m_new)
        p = jnp.exp2(s - m_new[:, 0:1])
        l_scr[...] = alpha * l_scr[...] + jnp.sum(p, axis=1, keepdims=True)
        acc_scr[...] = alpha * acc_scr[...] + pv(p.astype(BF16))
        m_scr[...] = m_new

    for pair in range(pages_per_step // 2):
        pa, pb = pages[2 * pair], pages[2 * pair + 1]
        kt = jnp.concatenate([pa[0, 0:128, :], pb[0, 0:128, :]], axis=1).astype(BF16)
        vt = jnp.concatenate([pa[0, 128:256, :], pb[0, 128:256, :]], axis=1).astype(BF16)
        off = pl.multiple_of((s_idx * pages_per_step + 2 * pair) * PAGE_SIZE, 2 * PAGE_SIZE)
        rhs = jnp.concatenate([kt, et_ref[:, pl.ds(off, 2 * PAGE_SIZE)]], axis=0)
        online(_dot(qa, rhs), lambda p: _dot_nt(p, vt))

    @pl.when(s_idx == pl.num_programs(1) - 1)
    def _():
        tok = lax.broadcasted_iota(jnp.int32, (rows_n, 1), 0) & (SQ - 1)
        new_i = lax.broadcasted_iota(jnp.int32, (1, SQ), 1)
        new_ok = (new_i <= tok) & (new_i < n_new)
        kvn = kvn_ref[0]
        q2 = qa[:, 0:128]
        online(jnp.where(new_ok, _dot_nt(q2, kvn[:, 256:384]), NEG), lambda p: _dot(p, kvn[:, 384:512]))
        o_s = acc_scr[...] / l_scr[...]
        s_w = _dot(q2, win_ref[0, 0:128, :].astype(BF16))
        w_i = lax.broadcasted_iota(jnp.int32, (1, wlen), 1)
        s_w = jnp.where(w_i >= tok + (wlen - WINDOW), s_w, NEG)
        s_n = jnp.where(new_ok, _dot_nt(q2, kvn[:, 512:640]), NEG)
        m_w = jnp.maximum(jnp.max(s_w, axis=1, keepdims=True), jnp.max(s_n, axis=1, keepdims=True))
        p_w = jnp.exp2(s_w - m_w)
        p_n = jnp.exp2(s_n - m_w)
        l_w = jnp.sum(p_w, axis=1, keepdims=True) + jnp.sum(p_n, axis=1, keepdims=True)
        o_w = (_dot_nt(p_w.astype(BF16), win_ref[0, 128:256, :].astype(BF16))
               + _dot(p_n.astype(BF16), kvn[:, 640:768])) / l_w
        gates = _sigmoid(sm_ref[0])
        g_s = jnp.concatenate([gates[:, 3 * hd + 1:3 * hd + 2] for hd in range(NSA_HEADS)], axis=0)
        g_w = jnp.concatenate([gates[:, 3 * hd + 2:3 * hd + 3] for hd in range(NSA_HEADS)], axis=0)
        o = o_s * g_s + o_w * g_w
        half = rows_n // 2
        o_ref[0, 0:half, :] = o[0:half, 0:NSA_DIM]
        o_ref[0, half:, :] = o[half:, NSA_DIM:]


def _decode_sw(cache_t, page_table, et, qn, selb, kvn, win_t, sm, pages_per_step, n_new):
    bsz, n_pages = page_table.shape
    past = n_pages * PAGE_SIZE
    nb = et.shape[0]
    rows_n = NSA_HEADS * SQ
    full = lambda a: pl.BlockSpec(a.shape, lambda b, s, pt: (0,) * a.ndim)
    seq = lambda a: pl.BlockSpec((1,) + a.shape[1:], lambda b, s, pt: (b,) + (0,) * (a.ndim - 1))
    return pl.pallas_call(
        functools.partial(_decode_sw_kernel, pages_per_step=pages_per_step, past=past, n_new=n_new),
        grid_spec=pltpu.PrefetchScalarGridSpec(
            num_scalar_prefetch=1,
            grid=(bsz, n_pages // pages_per_step),
            in_specs=_page_specs(pages_per_step, 1) + [full(et), seq(qn), seq(selb), seq(kvn), seq(win_t), seq(sm)],
            out_specs=pl.BlockSpec((1, rows_n, NSA_DIM), lambda b, s, pt: (b, 0, 0)),
            scratch_shapes=[pltpu.VMEM((rows_n, 128 + nb), F32),
                            pltpu.VMEM((rows_n, 128), F32),
                            pltpu.VMEM((rows_n, 128), F32),
                            pltpu.VMEM((rows_n, 128), F32)]),
        out_shape=jax.ShapeDtypeStruct((bsz, rows_n, NSA_DIM), F32),
        compiler_params=_cparams(("arbitrary", "arbitrary")),
        name="decode_select_window",
    )(page_table, *([cache_t] * pages_per_step), et, qn, selb, kvn, win_t, sm)


def _selection_constants(n_keys):
    nb = n_keys // SEL_BLOCK
    n_sub = n_keys // CMP_STRIDE
    i = np.arange(n_sub)[None, :] * CMP_STRIDE
    j = np.arange(nb)[:, None] * SEL_BLOCK
    ovl_t = ((i < j + SEL_BLOCK) & (i + CMP_LEN > j) & (np.arange(n_sub)[None, :] < n_sub - 1))
    et = (np.arange(n_keys)[None, :] // SEL_BLOCK) == np.arange(nb)[:, None]
    return jnp.asarray(ovl_t, BF16), jnp.asarray(et, BF16)


def _ab_layer_prompt(x, mod4, tps, W, l, bsz, t, cfg):
    n = bsz * t
    j = l // 2
    tm, lc = cfg['tm'], cfg['lc']
    qk, v_m, o_pre, qn, kvr, win, sm, ka, kw, vt, smt = _ab_in(x, mod4, tps, W['norm_g0'][l], W['ab_w_in'][j],
                                                          W['ab_b_in'][j], tm, seq_shape=(bsz, t))
    seq = lambda a: a.reshape(bsz, t, a.shape[1])
    conv0 = jnp.zeros((bsz, CONV_W - 1, 2 * ML_WIDTH), F32)
    c0 = jnp.zeros((bsz, ML_HEADS, ML_DIM, ML_DIM), F32)
    n0 = jnp.zeros((bsz, ML_HEADS, ML_DIM), F32)
    m0 = jnp.zeros((bsz, ML_HEADS), F32)
    hm, c1, n1, m1 = _mlstm(seq(qk), seq(v_m), seq(o_pre), seq(sm), smt, W['ml_conv_w'][j], W['ml_conv_b'][j],
                            W['ml_f_bias'][j], W['ml_out_g'][j], conv0, c0, n0, m0, lc, gr_block=_SM_I // 8)
    conv_new = seq(qk)[:, -(CONV_W - 1):]
    kv_rows = kvr.reshape(bsz, t, 4, NSA_KV, NSA_DIM)
    win_rows = win.reshape(bsz, t, 2, NSA_KV, NSA_DIM)
    kvc, kvct = _compress(seq(kvr), W['cmp'][j])
    ovl_t, _ = _selection_constants(t)
    o_cmp_t, selb = _nsa_cmp(seq(qn), kvc, kvct, smt, ovl_t, cfg['tq_cmp'], N_SEL, t // CMP_STRIDE - 1)
    o_nsa = _nsa_sw(seq(qn), seq(ka), seq(kw), vt, selb, smt, o_cmp_t, cfg['tq'], cfg['tk'], cfg['tkw'])
    x = _ab_out(x, mod4, tps, hm.reshape(n, ML_WIDTH), [o_nsa.reshape(n, NSA_WIDTH)], W['ab_w_out'][j], tm)
    return x, (conv_new, c1, n1, m1, kv_rows, win_rows[:, -min(WINDOW, t):])


def _ab_layer_decode(x, mod4, W, l, bsz, t, cfg, st, page_table):
    n = bsz * t
    j = l // 2
    tm, lc = cfg['tm'], cfg['lc']
    conv0, c0, n0, m0, (cache_t, page0), win_buf, win_t = st
    page_table = page_table + page0
    qk, v_m, o_pre, qn, kvr, win, sm, kvb = _ab_in(x, mod4, 1, W['norm_g0'][l], W['ab_w_in'][j], W['ab_b_in'][j], tm)
    seq = lambda a: a.reshape(bsz, t, a.shape[1])
    pad_t = lambda a, tp: jnp.pad(a, ((0, 0), (0, tp - t), (0, 0)))
    sm3 = seq(sm)
    gr = sm3[:, :, _SM_I:_SM_I + 8].transpose(0, 2, 1)
    gr = jnp.concatenate([jnp.pad(gr[:, :4], ((0, 0), (0, 0), (0, lc - t)), constant_values=NEG),
                          jnp.pad(gr[:, 4:], ((0, 0), (0, 0), (0, lc - t)), constant_values=-NEG)], axis=1)
    sm_pad = jnp.zeros((bsz, lc - t, 128), F32).at[:, :, _SM_I:_SM_I + 4].set(NEG).at[:, :, _SM_F:_SM_F + 4].set(-NEG)
    hm, c1, n1, m1 = _mlstm(pad_t(seq(qk), lc), pad_t(seq(v_m), lc), pad_t(seq(o_pre), lc),
                            jnp.concatenate([sm3, sm_pad], axis=1), gr, W['ml_conv_w'][j], W['ml_conv_b'][j],
                            W['ml_f_bias'][j], W['ml_out_g'][j], conv0, c0, n0, m0, lc)
    hm = hm[:, :t].reshape(n, ML_WIDTH)
    conv_new = jnp.concatenate([conv0, seq(qk)], axis=1)[:, -(CONV_W - 1):]
    kv_rows = kvr.reshape(bsz, t, 4, NSA_KV, NSA_DIM)
    win_rows = win.reshape(bsz, t, 2, NSA_KV, NSA_DIM)
    past = page_table.shape[1] * PAGE_SIZE
    assert (past + t) // CMP_STRIDE == past // CMP_STRIDE and t <= min(SEL_BLOCK, SQ) and past % SEL_BLOCK == 0
    ovl_t, et = _selection_constants(past)
    qn3, sm3q, kvn = pad_t(seq(qn), SQ), pad_t(sm3, SQ), pad_t(seq(kvb), SQ)
    pps = cfg['pages_per_step']
    o_cmp, selb = _decode_cmp(cache_t, page_table, qn3, sm3q, ovl_t, W['cmp'][j], pps, N_SEL - 1)
    o_sw = _decode_sw(cache_t, page_table, et, qn3, selb, kvn, win_t, sm3q, pps, t)
    o_cmp = o_cmp[:, :t].reshape(n, NSA_WIDTH)
    o_sw = o_sw.reshape(bsz, NSA_HEADS, SQ, NSA_DIM).transpose(0, 2, 1, 3)[:, :t].reshape(n, NSA_WIDTH)
    x = _ab_out(x, mod4, 1, hm, [o_cmp, o_sw], W['ab_w_out'][j], tm)
    win_new = jnp.concatenate([win_buf, win_rows], axis=1)[:, -win_buf.shape[1]:]
    return x, (conv_new, c1, n1, m1, kv_rows, win_new)


def _trunk(x3, mods, W, state, page_table, cfg):
    bsz, t, _ = x3.shape
    n = bsz * t
    tm = cfg['tm']
    x = x3.reshape(n, D_MODEL)
    depth = mods.shape[0]
    ab_new, cl_new = [], []
    if t % tm == 0:
        tps = t // tm
        to_mod4 = lambda m: m.reshape(bsz, 6, 1, D_MODEL)
    else:
        assert n == tm
        tps = 1
        to_mod4 = lambda m: jnp.repeat(m.reshape(bsz, 6, D_MODEL), t, axis=0).reshape(n, 6, D_MODEL).transpose(1, 0, 2)[None]
    lch = min(GM_CHUNK, t)
    for l in range(depth):
        mod4 = to_mod4(mods[l])
        j = l // 2
        if l % 2 == 0:
            if state is None:
                x, new = _ab_layer_prompt(x, mod4, tps, W, l, bsz, t, cfg)
            else:
                x, new = _ab_layer_decode(x, mod4, W, l, bsz, t, cfg, tuple(a[j] for a in state), page_table)
            ab_new.append(new)
        else:
            ws = W['cl_ws'][j][:, :lch, :lch]
            bs = W['cl_bs'][j][:, :lch]
            if lch < GM_CHUNK:
                rep = GM_CHUNK // lch
                ws = jnp.einsum('ab,gts->gatbs', jnp.eye(rep, dtype=F32), ws).reshape(GM_GROUPS, GM_CHUNK, GM_CHUNK)
                bs = jnp.tile(bs, (1, rep))
            x, v = _cl_mixer(x, mod4, tps, W['norm_g0'][l], W['cl_w_in'][j], W['cl_b_in'][j], W['cl_v_g'][j],
                             ws, bs.T, W['cl_w_out'][j], tm)
            cl_new.append(v.reshape(bsz, t, -1))
        x = _ffn(x, mod4, max(t // cfg['tm_ffn'], 1), W['norm_g1'][l], W['ffn_w1'][l], W['ffn_w2'][l], W['final_g'],
                 l == depth - 1, cfg['tm_ffn'], cfg['tf'])
    return x.reshape(bsz, t, D_MODEL), ab_new, cl_new


def kernel(x_prompt, x_sample, c_prompt, c_sample, state_mlstm_conv, state_mlstm_C, state_mlstm_n,
           state_mlstm_m, cache_nsa_kv, state_nsa_win, page_table, ada_w, ada_b, norm_g, ab_w_in, ab_b_in,
           ml_conv_w, ml_conv_b, ml_f_bias, ml_out_g, phi_pe, phi_w1, phi_b1, phi_w2, phi_b2, ab_w_out,
           cl_w_in, cl_b_in, cl_v_g, cl_ws, cl_bs, cl_w_out, ffn_w1, ffn_w2, final_g):
    depth = ada_w.shape[0]
    n_ab = ab_w_in.shape[0]
    bp, bs_ = c_prompt.shape[0], c_sample.shape[0]
    rows = bp + bs_
    rows_pad = -(-rows // 8) * 8
    c_all = jnp.concatenate([c_prompt, c_sample, jnp.zeros((rows_pad - rows, D_MODEL), F32)], axis=0)
    mods = _ada_mod(c_all, ada_w, ada_b)
    mods_p = mods[:, :bp]
    mods_s = mods[:, bp:rows]
    w_in = jnp.concatenate([ab_w_in[:, :, 0:2048], ab_w_in[:, :, 2056:3336], ab_w_in[:, :, 3336:3360],
                            ab_w_in[:, :, 2048:2056], jnp.zeros((n_ab, D_MODEL, 96), F32)], axis=2).astype(BF16)
    b_in = jnp.concatenate([ab_b_in[:, 0:2048], ab_b_in[:, 2056:3336], ab_b_in[:, 3336:3360],
                            ab_b_in[:, 2048:2056], jnp.zeros((n_ab, 96), F32)], axis=1)[:, None, :]
    half = CMP_STRIDE * NSA_DIM
    blk = phi_w1.reshape(n_ab, 2, 2, CMP_STRIDE, NSA_DIM, NSA_DIM).transpose(0, 1, 3, 4, 2, 5)
    blk = blk.reshape(n_ab, 2, CMP_STRIDE, NSA_DIM, 2 * NSA_DIM)
    zero = jnp.zeros_like(blk)
    wc = jnp.concatenate([jnp.concatenate([blk, zero], axis=4), jnp.concatenate([zero, blk], axis=4)], axis=3)
    pe = jnp.pad(phi_pe.reshape(n_ab, 2, 2, half), ((0, 0), (0, 0), (0, 6), (0, 0)))
    cmp_w = [dict(wc=wc[j].astype(BF16),
                  w1cat=jnp.concatenate([phi_w1[j, :, :half], phi_w1[j, :, half:]], axis=2).astype(BF16),
                  pe=pe[j], b1=phi_b1[j][:, None, :], w2=phi_w2[j].astype(BF16), b2=phi_b2[j][:, None, :])
             for j in range(n_ab)]
    W = dict(
        norm_g0=norm_g[:, 0][:, None, :], norm_g1=norm_g[:, 1][:, None, :],
        ab_w_in=w_in, ab_b_in=b_in,
        ml_conv_w=ml_conv_w, ml_conv_b=ml_conv_b[:, None, :], ml_f_bias=ml_f_bias, ml_out_g=ml_out_g[:, None, :],
        cmp=cmp_w,
        ab_w_out=ab_w_out.astype(BF16),
        cl_w_in=cl_w_in.astype(BF16), cl_b_in=cl_b_in[:, None, :], cl_v_g=cl_v_g[:, None, :],
        cl_ws=cl_ws, cl_bs=cl_bs, cl_w_out=cl_w_out.astype(BF16),
        ffn_w1=ffn_w1.astype(BF16), ffn_w2=ffn_w2.astype(BF16), final_g=final_g[None, :])
    cfg_p = dict(tm=256, tm_ffn=512, tf=512, lc=512, tq_cmp=128, tq=256, tk=512, tkw=256)
    y_prompt, ab_p, _ = _trunk(x_prompt, mods_p, W, None, None, cfg_p)
    n_s = x_sample.shape[0] * x_sample.shape[1]
    cfg_s = dict(tm=n_s, tm_ffn=n_s, tf=512, lc=128, pages_per_step=16)
    n_pool = cache_nsa_kv.shape[1]
    cache_t = cache_nsa_kv.transpose(0, 1, 3, 4, 5, 2).reshape(n_ab * n_pool, 4 * NSA_KV * NSA_DIM, PAGE_SIZE)
    cache_t = [(cache_t, j * n_pool) for j in range(n_ab)]
    wlen = state_nsa_win.shape[2]
    win_t = state_nsa_win.transpose(0, 1, 3, 4, 5, 2).reshape(n_ab, bs_, 2 * NSA_KV * NSA_DIM, wlen)
    state = (state_mlstm_conv, state_mlstm_C, state_mlstm_n, state_mlstm_m, cache_t, state_nsa_win, win_t)
    y_sample, ab_s, cl_s = _trunk(x_sample, mods_s, W, state, page_table, cfg_s)
    p_out = [jnp.stack(a) for a in zip(*ab_p)]
    s_out = [jnp.stack(a) for a in zip(*ab_s)]
    return (y_prompt, y_sample, *p_out, *s_out, jnp.stack(cl_s))
```

```python
import functools

import numpy as np
import jax
import jax.numpy as jnp
from jax import lax
from jax.experimental import pallas as pl
from jax.experimental.pallas import tpu as pltpu

F32 = jnp.float32
BF16 = jnp.bfloat16

EPS = 1e-6
D_MODEL = 1024
ML_HEADS = 4
ML_DIM = 128
ML_WIDTH = ML_HEADS * ML_DIM
CONV_W = 4
NSA_HEADS = 8
NSA_KV = 2
NSA_GROUP = NSA_HEADS // NSA_KV
NSA_DIM = 64
NSA_WIDTH = NSA_HEADS * NSA_DIM
CMP_LEN = 32
CMP_STRIDE = 16
SEL_BLOCK = 64
N_SEL = 16
WINDOW = 512
GM_GROUPS = 4
GM_CHUNK = 128
PAGE_SIZE = 128

_SEG_QK = (0, 1024)
_SEG_V = (1024, 1536)
_SEG_O = (1536, 2048)
_SEG_QN = (2048, 2560)
_SEG_KV = (2560, 3328)
_SEG_SM = (3328, 3456)
_AB_COLS = 3456
_SM_I = 24
_SM_F = 28

NEG = -1e30
SEL_BIG = 2.0 ** 100
LOG2E = 1.4426950408889634
VMEM_LIMIT = 56 * 1024 * 1024


def _cparams(sem):
    return pltpu.CompilerParams(dimension_semantics=sem, vmem_limit_bytes=VMEM_LIMIT)


def _sigmoid(x):
    return 1.0 / (1.0 + jnp.exp(-x))


def _log_sigmoid(x):
    return jnp.minimum(x, 0.0) - jnp.log1p(jnp.exp(-jnp.abs(x)))


def _gelu(x):
    return 0.5 * x * (1.0 + jnp.tanh(0.7978845608028654 * (x + 0.044715 * (x * x * x))))


def _normmod(x, g, scale, shift):
    y = x * lax.rsqrt(jnp.mean(x * x, axis=-1, keepdims=True) + EPS) * g
    return y * (1.0 + scale) + shift


def _dot(a, b):
    return jnp.dot(a, b, preferred_element_type=F32)


def _dot_nt(a, b):
    return lax.dot_general(a, b, (((1,), (1,)), ((), ())), preferred_element_type=F32)


def _split3(x):
    hi = x.astype(BF16)
    r = x - hi.astype(F32)
    mid = r.astype(BF16)
    lo = (r - mid.astype(F32)).astype(BF16)
    return hi, mid, lo


def _ada_kernel(c_ref, w_ref, b_ref, o_ref):
    c = c_ref[...]
    sc = c * _sigmoid(c)
    sc_hi = sc.astype(BF16)
    sc_lo = (sc - sc_hi.astype(F32)).astype(BF16)
    w = w_ref[0]
    w_hi = w.astype(BF16)
    w_lo = (w - w_hi.astype(F32)).astype(BF16)
    acc = _dot(sc_hi, w_hi) + _dot(sc_lo, w_hi) + _dot(sc_hi, w_lo)
    o_ref[0] = acc + b_ref[0]


def _ada_mod(c_all, ada_w, ada_b):
    depth, d, n = ada_w.shape
    bp = c_all.shape[0]
    tn = 1536
    return pl.pallas_call(
        _ada_kernel,
        grid=(depth, n // tn),
        in_specs=[pl.BlockSpec((bp, d), lambda l, j: (0, 0)),
                  pl.BlockSpec((1, d, tn), lambda l, j: (l, 0, j)),
                  pl.BlockSpec((1, 1, tn), lambda l, j: (l, 0, j))],
        out_specs=pl.BlockSpec((1, bp, tn), lambda l, j: (l, 0, j)),
        out_shape=jax.ShapeDtypeStruct((depth, bp, n), F32),
        compiler_params=_cparams(("arbitrary", "arbitrary")),
        name="ada_mod",
    )(c_all, ada_w, ada_b.reshape(depth, 1, n))


def _mod_spec(mod4, tm, tiles_per_seq):
    r = mod4.shape[2]
    if r == 1:
        return pl.BlockSpec((1, 6, 1, D_MODEL), lambda i: (i // tiles_per_seq, 0, 0, 0))
    assert r == tm
    return pl.BlockSpec((1, 6, tm, D_MODEL), lambda i: (i, 0, 0, 0))


def _ab_in_kernel(x_ref, mod_ref, g_ref, w_ref, b_ref, *rest, tm, tiles_per_seq, nb, seq_layouts):
    h = _normmod(x_ref[...], g_ref[...], mod_ref[0, 1], mod_ref[0, 0]).astype(BF16)

    def seg(ab):
        a, b = ab
        return _dot(h, w_ref[:, a:b]) + b_ref[:, a:b]

    if seq_layouts:
        wt_ref, bt_ref, qk_ref, v_ref, o_ref, qn_ref, kvr_ref, win_ref, sm_ref, ka_ref, kw_ref, vt_ref, smt_ref = rest
    else:
        qk_ref, v_ref, o_ref, qn_ref, kvr_ref, win_ref, sm_ref, kvb_ref = rest
    qk_ref[...] = seg(_SEG_QK)
    v_ref[...] = seg(_SEG_V).astype(BF16)
    o_ref[...] = seg(_SEG_O)
    qn_ref[...] = (seg(_SEG_QN) * (NSA_DIM ** -0.5 * LOG2E)).astype(BF16)
    kv = seg(_SEG_KV)
    kvr_ref[...] = kv[:, :512]
    win_ref[...] = kv[:, 512:]
    sm_ref[...] = seg(_SEG_SM)
    if not seq_layouts:
        kvb_ref[...] = kv.astype(BF16)
        return
    pos = (pl.program_id(0) % tiles_per_seq) * tm + lax.broadcasted_iota(jnp.int32, (tm, nb), 0)
    onehot = jnp.where(pos // SEL_BLOCK == lax.broadcasted_iota(jnp.int32, (tm, nb), 1), 1.0, 0.0).astype(BF16)
    ka_w = ka_ref.shape[1] // NSA_KV
    pieces = []
    for g in range(NSA_KV):
        pieces += [onehot, kv[:, 256 + NSA_DIM * g:256 + NSA_DIM * (g + 1)].astype(BF16)]
        if ka_w > nb + NSA_DIM:
            pieces.append(jnp.zeros((tm, ka_w - nb - NSA_DIM), BF16))
    ka_ref[...] = jnp.concatenate(pieces, axis=1)
    kw_ref[...] = kv[:, 512:640].astype(BF16)
    zt = _dot_nt(wt_ref[...], h) + bt_ref[...]
    vt_ref[0] = zt[:256].astype(BF16)
    smt_ref[0] = zt[256:]


def _ab_in(x, mod4, tiles_per_seq, g, w, b, tm, seq_shape=None):
    n = x.shape[0]
    widths = [(1024, F32), (512, BF16), (512, F32), (512, BF16), (512, F32), (256, F32), (128, F32)]
    ins = [x, mod4, g, w, b]
    in_specs = [pl.BlockSpec((tm, D_MODEL), lambda i: (i, 0)),
                _mod_spec(mod4, tm, tiles_per_seq),
                pl.BlockSpec((1, D_MODEL), lambda i: (0, 0)),
                pl.BlockSpec((D_MODEL, _AB_COLS), lambda i: (0, 0)),
                pl.BlockSpec((1, _AB_COLS), lambda i: (0, 0))]
    nb = 0
    if seq_shape is None:
        widths.append((768, BF16))
    else:
        bsz, t = seq_shape
        nb = t // SEL_BLOCK
        ka_w = -(-(nb + NSA_DIM) // 128) * 128
        widths += [(NSA_KV * ka_w, BF16), (128, BF16)]
        a, c = _SEG_KV[0], _SEG_SM[0]
        cols = jnp.concatenate([w[:, a + 384:a + 512], w[:, a + 640:a + 768], w[:, c:c + 128]], axis=1)
        bcols = jnp.concatenate([b[:, a + 384:a + 512], b[:, a + 640:a + 768], b[:, c:c + 128]], axis=1)
        ins += [cols.T, bcols.T]
        in_specs += [pl.BlockSpec((384, D_MODEL), lambda i: (0, 0)), pl.BlockSpec((384, 1), lambda i: (0, 0))]
    out_specs = [pl.BlockSpec((tm, wd), lambda i: (i, 0)) for wd, _ in widths]
    out_shape = [jax.ShapeDtypeStruct((n, wd), dt) for wd, dt in widths]
    if seq_shape is not None:
        seq_map = lambda i: (i // tiles_per_seq, 0, i % tiles_per_seq)
        out_specs += [pl.BlockSpec((1, 256, tm), seq_map), pl.BlockSpec((1, 128, tm), seq_map)]
        out_shape += [jax.ShapeDtypeStruct((bsz, 256, t), BF16), jax.ShapeDtypeStruct((bsz, 128, t), F32)]
    return pl.pallas_call(
        functools.partial(_ab_in_kernel, tm=tm, tiles_per_seq=tiles_per_seq, nb=nb,
                          seq_layouts=seq_shape is not None),
        grid=(n // tm,),
        in_specs=in_specs,
        out_specs=out_specs,
        out_shape=out_shape,
        compiler_params=_cparams(("arbitrary",)),
        name="ab_in",
    )(*ins)


def _mlstm_kernel(qk_ref, v_ref, o_ref, sm_ref, gr_ref, cw_ref, cb_ref, fbc_ref, fbr_ref, og_ref,
                  conv0_ref, c0_ref, n0_ref, m0_ref,
                  hm_ref, cout_ref, nout_ref, mout_ref,
                  xp_scr, c_scr, n_scr, m_scr, *, lc):
    c = pl.program_id(1)

    @pl.when(c == 0)
    def _():
        xp_scr[0:8, :] = conv0_ref[0]
        c_scr[...] = c0_ref[0]
        n_scr[...] = n0_ref[0]
        m_scr[...] = m0_ref[0]

    xp_scr[8:8 + lc, :] = qk_ref[0]
    y = cb_ref[...]
    for j in range(CONV_W):
        y = y + xp_scr[5 + j:5 + j + lc, :] * cw_ref[j:j + 1, :]
    xp_scr[0:8, :] = xp_scr[lc:lc + 8, :]
    act = y * _sigmoid(y)
    q_all = act[:, :ML_WIDTH]
    k_all = act[:, ML_WIDTH:] * (ML_DIM ** -0.5)

    sm = sm_ref[0]
    gr = gr_ref[0]
    lf_cols = _log_sigmoid(sm + fbc_ref[...])
    lf_rows = _log_sigmoid(gr + fbr_ref[...])
    v_all = v_ref[0]
    o_all = o_ref[0]

    row_i = lax.broadcasted_iota(jnp.int32, (lc, lc), 0)
    col_i = lax.broadcasted_iota(jnp.int32, (lc, lc), 1)
    tril = row_i >= col_i
    triu = row_i <= col_i

    outs = []
    for h in range(ML_HEADS):
        hs = slice(ML_DIM * h, ML_DIM * (h + 1))
        li_c = sm[:, _SM_I + h:_SM_I + h + 1]
        lf_c = lf_cols[:, _SM_F + h:_SM_F + h + 1]
        li_r = gr[h:h + 1, :]
        lf_r = lf_rows[4 + h:5 + h, :]
        b_c = jnp.sum(jnp.where(tril, lf_r, 0.0), axis=1, keepdims=True)
        b_r = jnp.sum(jnp.where(triu, lf_c, 0.0), axis=0, keepdims=True)
        a_r = li_r - b_r
        a_c = li_c - b_c
        m_h = m_scr[h:h + 1, 0:1]
        cm_c = jnp.maximum(m_h, jnp.max(jnp.where(tril, a_r, NEG), axis=1, keepdims=True))
        dm = jnp.exp(jnp.where(tril, a_r - cm_c, NEG))
        w_int = jnp.exp(m_h - cm_c)
        qh = q_all[:, hs]
        kh = k_all[:, hs]
        vh = v_all[:, hs]
        qb = qh.astype(BF16)
        kb = kh.astype(BF16)
        s = _dot_nt(qb, kb) * dm
        c_old = c_scr[h]
        n_old = n_scr[h:h + 1, :]
        num = w_int * _dot(qb, c_old.astype(BF16)) + _dot(s.astype(BF16), vh)
        den = w_int * jnp.sum(qh * n_old, axis=1, keepdims=True) + jnp.sum(s, axis=1, keepdims=True)
        mt = b_c + cm_c
        hh = num / jnp.maximum(jnp.abs(den), jnp.exp(-mt))
        hn = hh * lax.rsqrt(jnp.mean(hh * hh, axis=1, keepdims=True) + EPS) * og_ref[:, hs]
        outs.append(hn * _sigmoid(o_all[:, hs]))
        cm_last = jnp.maximum(m_h, jnp.max(a_r, axis=1, keepdims=True))
        bl = jnp.sum(lf_r, axis=1, keepdims=True)
        decay = jnp.exp(m_h - cm_last)
        ws_c = jnp.exp(a_c - cm_last)
        kt = kh.T.astype(BF16)
        c_scr[h] = decay * c_old + _dot(kt, (ws_c * vh.astype(F32)).astype(BF16))
        n_scr[h:h + 1, :] = decay * n_old + jnp.sum(ws_c * kh, axis=0, keepdims=True)
        m_scr[h:h + 1, :] = jnp.broadcast_to(bl + cm_last, (1, ML_DIM))

    hm_ref[0] = jnp.concatenate(outs, axis=1).astype(BF16)

    @pl.when(c == pl.num_programs(1) - 1)
    def _():
        cout_ref[0] = c_scr[...]
        nout_ref[0] = n_scr[...]
        mout_ref[0] = m_scr[...]


def _mlstm(qk, v, o, sm, gr, conv_w, conv_b, f_bias, out_g, conv0, c0, n0, m0, lc, gr_block=0):
    bsz, t, _ = qk.shape
    nc = t // lc
    fbc = jnp.zeros((1, 128), F32).at[0, _SM_F:_SM_F + ML_HEADS].set(f_bias)
    fbr = jnp.zeros((8, 1), F32).at[4:8, 0].set(f_bias)
    conv0p = jnp.concatenate([jnp.zeros((bsz, 5, 2 * ML_WIDTH), F32), conv0], axis=1)
    m0p = jnp.broadcast_to(m0[:, :, None], (bsz, ML_HEADS, ML_DIM))
    tok = lambda wd: pl.BlockSpec((1, lc, wd), lambda b, c: (b, c, 0))
    full2 = lambda a: pl.BlockSpec(a.shape, lambda b, c: (0, 0))
    st3 = lambda a: pl.BlockSpec((1,) + a.shape[1:], lambda b, c: (b,) + (0,) * (a.ndim - 1))
    hm, c1, n1, m1 = pl.pallas_call(
        functools.partial(_mlstm_kernel, lc=lc),
        grid=(bsz, nc),
        in_specs=[tok(1024), tok(512), tok(512), tok(128),
                  pl.BlockSpec((1, 8, lc), lambda b, c: (b, gr_block, c)),
                  full2(conv_w), full2(conv_b), full2(fbc), full2(fbr), full2(out_g),
                  st3(conv0p), st3(c0), st3(n0), st3(m0p)],
        out_specs=[tok(512), st3(c0), st3(n0), st3(m0p)],
        out_shape=[jax.ShapeDtypeStruct((bsz, t, ML_WIDTH), BF16),
                   jax.ShapeDtypeStruct(c0.shape, F32),
                   jax.ShapeDtypeStruct(n0.shape, F32),
                   jax.ShapeDtypeStruct(m0p.shape, F32)],
        scratch_shapes=[pltpu.VMEM((lc + 8, 2 * ML_WIDTH), F32),
                        pltpu.VMEM((ML_HEADS, ML_DIM, ML_DIM), F32),
                        pltpu.VMEM((ML_HEADS, ML_DIM), F32),
                        pltpu.VMEM((ML_HEADS, ML_DIM), F32)],
        compiler_params=_cparams(("arbitrary", "arbitrary")),
        name="mlstm",
    )(qk, v, o, sm, gr, conv_w, conv_b, fbc, fbr, out_g, conv0p, c0, n0, m0p)
    return hm, c1, n1, m1[:, :, 0]


def _compress_math(x_at, wc_at, w1cat, pe, b1, w2, b2, n_sub):
    acc = None
    for p in range(CMP_STRIDE):
        d = _dot(x_at(p).astype(BF16), wc_at(p))
        acc = d if acc is None else acc + d
    pe_hi = pe.astype(BF16)
    pe_lo = (pe - pe_hi.astype(F32)).astype(BF16)
    pe_c = _dot(pe_hi, w1cat) + _dot(pe_lo, w1cat)
    const = b1 + pe_c[0:1, :NSA_DIM] + pe_c[1:2, NSA_DIM:]
    outs = []
    for g in range(NSA_KV):
        p0 = acc[:, 128 * g:128 * g + NSA_DIM]
        p1 = pltpu.roll(acc[:, 128 * g + NSA_DIM:128 * (g + 1)], n_sub - 1, 0)
        outs.append(_dot(_gelu(const + p0 + p1).astype(BF16), w2) + b2)
    return outs


def _compress_kernel(x_ref, wc_ref, w1_ref, pe_ref, b1_ref, w2_ref, b2_ref, o_ref, ot_ref):
    n_sub = o_ref.shape[3]
    outs = _compress_math(lambda p: x_ref[0, pl.ds(p, n_sub, stride=CMP_STRIDE), :], lambda p: wc_ref[0, p],
                          w1_ref[0], pe_ref[0], b1_ref[0], w2_ref[0], b2_ref[0], n_sub)
    for g in range(NSA_KV):
        o_ref[0, 0, g] = outs[g].astype(BF16)
        ot_ref[0, 0, g] = outs[g].T.astype(BF16)


def _compress(kvr3, cw):
    bsz, t, _ = kvr3.shape
    n_sub = t // CMP_STRIDE
    kind = lambda a: pl.BlockSpec((1,) + a.shape[1:], lambda k, b: (k,) + (0,) * (a.ndim - 1))
    ws = [cw['wc'], cw['w1cat'], cw['pe'], cw['b1'], cw['w2'], cw['b2']]
    return pl.pallas_call(
        _compress_kernel,
        grid=(2, bsz),
        in_specs=[pl.BlockSpec((1, t, 128), lambda k, b: (b, 0, k))] + [kind(a) for a in ws],
        out_specs=[pl.BlockSpec((1, 1, NSA_KV, n_sub, NSA_DIM), lambda k, b: (k, b, 0, 0, 0)),
                   pl.BlockSpec((1, 1, NSA_KV, NSA_DIM, n_sub), lambda k, b: (k, b, 0, 0, 0))],
        out_shape=[jax.ShapeDtypeStruct((2, bsz, NSA_KV, n_sub, NSA_DIM), BF16),
                   jax.ShapeDtypeStruct((2, bsz, NSA_KV, NSA_DIM, n_sub), BF16)],
        compiler_params=_cparams(("arbitrary", "arbitrary")),
        name="nsa_compress",
    )(kvr3, *ws)


def _select_blocks(imps, qpos_r, n_pick):
    nb, cols = imps[0].shape
    blk = lax.broadcasted_iota(jnp.int32, (nb, cols), 0)
    cur = qpos_r // SEL_BLOCK
    avail = blk * SEL_BLOCK <= qpos_r
    forced = (blk == 0) | (blk == cur) | (blk == cur - 1)
    vals = tuple(jnp.where(avail, jnp.where(forced, 1e9, imp), -1.0) for imp in imps)

    def pick(_, carry):
        out = []
        for val, sel in zip(carry[0], carry[1]):
            mx = jnp.max(val, axis=0, keepdims=True)
            first = jnp.min(jnp.where(val == mx, blk, nb), axis=0, keepdims=True)
            hit = blk == first
            out.append((jnp.where(hit, -2.0, val), jnp.where(hit, 1.0, sel)))
        return tuple(o[0] for o in out), tuple(o[1] for o in out)

    _, sels = lax.fori_loop(0, n_pick, pick, (vals, tuple(jnp.zeros((nb, cols), F32) for _ in imps)))
    return [(sel - 1.0) * SEL_BIG for sel in sels]


def _nsa_cmp_kernel(q_ref, kc_ref, vct_ref, gt_ref, ovl_ref, o_ref, sel_ref, *, tq, n_pick, nc):
    i = pl.program_id(1)
    q = q_ref[0]
    gates_t = _sigmoid(gt_ref[0])
    n_pad = kc_ref.shape[3]
    qpos_r = i * tq + lax.broadcasted_iota(jnp.int32, (1, tq), 1)
    n_i = lax.broadcasted_iota(jnp.int32, (n_pad, 1), 0)
    valid = (n_i * CMP_STRIDE + (CMP_LEN - 1) <= qpos_r) & (n_i < nc)
    bias = jnp.where(valid, 0.0, NEG)
    any_valid = qpos_r >= CMP_LEN - 1
    imps = []
    for g in range(NSA_KV):
        q4 = jnp.concatenate([q[:, NSA_DIM * (g * NSA_GROUP + r):NSA_DIM * (g * NSA_GROUP + r + 1)]
                              for r in range(NSA_GROUP)], axis=0)
        s_all = _dot_nt(kc_ref[0, 0, g], q4)
        vct = vct_ref[0, 0, g]
        psum = None
        for r in range(NSA_GROUP):
            hd = g * NSA_GROUP + r
            s = s_all[:, r * tq:(r + 1) * tq] + bias
            m = jnp.max(s, axis=0, keepdims=True)
            p = jnp.exp2(s - m)
            d = jnp.sum(p, axis=0, keepdims=True)
            p = p * jnp.where(any_valid, 1.0 / d, 0.0)
            psum = p if psum is None else psum + p
            o_ref[0, NSA_DIM * hd:NSA_DIM * (hd + 1), :] = _dot(vct, p.astype(BF16)) * gates_t[3 * hd:3 * hd + 1, :]
        imps.append(sum(_dot(ovl_ref[...], part) for part in _split3(psum)))
    for g, selb in enumerate(_select_blocks(imps, qpos_r, n_pick)):
        sel_ref[0, g] = selb.astype(BF16)


def _nsa_cmp(qn, kvc, kvct, smt, ovl_t, tq, n_pick, nc):
    bsz, t, _ = qn.shape
    n_pad = kvc.shape[3]
    nb = ovl_t.shape[0]
    return pl.pallas_call(
        functools.partial(_nsa_cmp_kernel, tq=tq, n_pick=n_pick, nc=nc),
        grid=(bsz, t // tq),
        in_specs=[pl.BlockSpec((1, tq, NSA_WIDTH), lambda b, i: (b, i, 0)),
                  pl.BlockSpec((1, 1, NSA_KV, n_pad, NSA_DIM), lambda b, i: (0, b, 0, 0, 0)),
                  pl.BlockSpec((1, 1, NSA_KV, NSA_DIM, n_pad), lambda b, i: (1, b, 0, 0, 0)),
                  pl.BlockSpec((1, 32, tq), lambda b, i: (b, 0, i)),
                  pl.BlockSpec((nb, n_pad), lambda b, i: (0, 0))],
        out_specs=[pl.BlockSpec((1, NSA_WIDTH, tq), lambda b, i: (b, 0, i)),
                   pl.BlockSpec((1, NSA_KV, nb, tq), lambda b, i: (b, 0, 0, i))],
        out_shape=[jax.ShapeDtypeStruct((bsz, NSA_WIDTH, t), F32),
                   jax.ShapeDtypeStruct((bsz, NSA_KV, nb, t), BF16)],
        compiler_params=_cparams(("arbitrary", "arbitrary")),
        name="nsa_cmp_select",
    )(qn, kvc, kvct, smt, ovl_t)


def _flash_t(k_at, v_at, qt_scr, lo, n_full, hi, mask_at, m_scr, acc_scr, p_scr, s_scr, tk):
    w = qt_scr.shape[2]
    cw = min(w, 128)
    m_scr[...] = jnp.full(m_scr.shape, -3e38, F32)
    acc_scr[...] = jnp.zeros(acc_scr.shape, F32)

    rb = min(tk, 128)
    nrb = tk // rb

    def fold(x, op):
        return op(x.reshape(rb // 8, 8, cw), axis=0) if rb > 8 else x

    def make_body(masked):
        def body(j, carry):
            for g in range(NSA_KV):
                s_scr[g, 0:tk, :] = _dot(k_at(g, j), qt_scr[g])
            for g in range(NSA_KV):
                for c in range(w // cw):
                    cs = slice(c * cw, (c + 1) * cw)
                    m8 = None
                    for i in range(nrb):
                        rows = slice(i * rb, (i + 1) * rb)
                        s = s_scr[g, rows, cs]
                        if masked:
                            s = jnp.where(mask_at(j, rows, c * cw, cw), s, NEG)
                            s_scr[g, rows, cs] = s
                        f = fold(s, jnp.max)
                        m8 = f if m8 is None else jnp.maximum(m8, f)
                    m_prev = m_scr[g, :, cs]
                    m_new = jnp.maximum(m_prev, jnp.max(m8, axis=0, keepdims=True))
                    alpha = jnp.exp2(m_prev - m_new)
                    for i in range(nrb):
                        rows = slice(i * rb, (i + 1) * rb)
                        p_scr[g, rows, cs] = jnp.exp2(s_scr[g, rows, cs] - m_new).astype(BF16)
                    m_scr[g, :, cs] = m_new
                    acc_scr[g, :, cs] = acc_scr[g, :, cs] * alpha
                v1 = jnp.concatenate([v_at(g, j), jnp.ones((8, tk), BF16)], axis=0)
                acc_scr[g] += _dot(v1, p_scr[g, 0:tk, :])
            return carry
        return body

    lax.fori_loop(lo, n_full, make_body(False), 0)
    lax.fori_loop(n_full, hi, make_body(True), 0)
    return [acc_scr[g, 0:NSA_DIM] / acc_scr[g, NSA_DIM:NSA_DIM + 1] for g in range(NSA_KV)]


def _nsa_sw_kernel(q_ref, ka_ref, kw_ref, vt_ref, sel_ref, gt_ref, oc_ref, o_ref,
                   qa_scr, qw_scr, m_scr, acc_scr, p_scr, s_scr, *, tq, tk, tkw):
    i = pl.program_id(1)
    w = NSA_GROUP * tq
    nb = sel_ref.shape[2]
    ka_w = qa_scr.shape[1]
    qt = q_ref[0].astype(F32).T.astype(BF16)
    gates_t = _sigmoid(gt_ref[0])
    q_first = i * tq
    q_last = q_first + tq - 1
    if ka_w > nb + NSA_DIM:
        qa_scr[:, nb + NSA_DIM:, :] = jnp.zeros((NSA_KV, ka_w - nb - NSA_DIM, w), BF16)
    qw_scr[...] = jnp.zeros(qw_scr.shape, BF16)
    for g in range(NSA_KV):
        for r in range(NSA_GROUP):
            hd = g * NSA_GROUP + r
            cols = slice(r * tq, (r + 1) * tq)
            q_h = qt[NSA_DIM * hd:NSA_DIM * (hd + 1), :]
            qa_scr[g, 0:nb, cols] = sel_ref[0, g]
            qa_scr[g, nb:nb + NSA_DIM, cols] = q_h
            qw_scr[g, NSA_DIM * g:NSA_DIM * (g + 1), cols] = q_h

    def ka_at(g, j):
        return ka_ref[0, pl.ds(pl.multiple_of(j * tk, tk), tk), ka_w * g:ka_w * (g + 1)]

    def vs_at(g, j):
        return vt_ref[0, NSA_DIM * g:NSA_DIM * (g + 1), pl.ds(pl.multiple_of(j * tk, tk), tk)]

    def kw_at(g, j):
        return kw_ref[0, pl.ds(pl.multiple_of(j * tkw, tkw), tkw), :]

    def vw_at(g, j):
        return vt_ref[0, 128 + NSA_DIM * g:128 + NSA_DIM * (g + 1), pl.ds(pl.multiple_of(j * tkw, tkw), tkw)]

    def key_pos(j, tile, rows):
        return j * tile + rows.start + lax.broadcasted_iota(jnp.int32, (rows.stop - rows.start, 1), 0)

    def query_pos(col0, cols):
        return q_first + ((col0 + lax.broadcasted_iota(jnp.int32, (1, cols), 1)) & (tq - 1))

    def slc_mask(j, rows, col0, cols):
        return key_pos(j, tk, rows) <= query_pos(col0, cols)

    def win_mask(j, rows, col0, cols):
        kpos = key_pos(j, tkw, rows)
        qp = query_pos(col0, cols)
        return (kpos <= qp) & (kpos >= qp - WINDOW)

    o_s = _flash_t(ka_at, vs_at, qa_scr, 0, (q_first + 1) // tk, q_last // tk + 1, slc_mask,
                   m_scr, acc_scr, p_scr, s_scr, tk)
    w_lo = jnp.maximum(q_first - WINDOW, 0) // tkw
    o_w = _flash_t(kw_at, vw_at, qw_scr, w_lo, w_lo, q_last // tkw + 1, win_mask,
                   m_scr, acc_scr, p_scr, s_scr, tkw)
    outs = []
    for hd in range(NSA_HEADS):
        g, r = divmod(hd, NSA_GROUP)
        cols = slice(r * tq, (r + 1) * tq)
        o_h = (oc_ref[0, NSA_DIM * hd:NSA_DIM * (hd + 1), :]
               + o_s[g][:, cols] * gates_t[3 * hd + 1:3 * hd + 2, :]
               + o_w[g][:, cols] * gates_t[3 * hd + 2:3 * hd + 3, :])
        outs.append(o_h.T)
    o_ref[0] = jnp.concatenate(outs, axis=1)


def _nsa_sw(qn, ka, kw, vt, selb, smt, o_cmp_t, tq, tk, tkw):
    bsz, t, _ = qn.shape
    nb = selb.shape[2]
    assert tq & (tq - 1) == 0
    w = NSA_GROUP * tq
    ka_w = ka.shape[2] // NSA_KV
    seq = lambda a: pl.BlockSpec((1,) + a.shape[1:], lambda b, i: (b, 0, 0))
    return pl.pallas_call(
        functools.partial(_nsa_sw_kernel, tq=tq, tk=tk, tkw=tkw),
        grid=(bsz, t // tq),
        in_specs=[pl.BlockSpec((1, tq, NSA_WIDTH), lambda b, i: (b, i, 0)),
                  seq(ka), seq(kw), seq(vt),
                  pl.BlockSpec((1, NSA_KV, nb, tq), lambda b, i: (b, 0, 0, i)),
                  pl.BlockSpec((1, 32, tq), lambda b, i: (b, 0, i)),
                  pl.BlockSpec((1, NSA_WIDTH, tq), lambda b, i: (b, 0, i))],
        out_specs=pl.BlockSpec((1, tq, NSA_WIDTH), lambda b, i: (b, i, 0)),
        out_shape=jax.ShapeDtypeStruct((bsz, t, NSA_WIDTH), F32),
        scratch_shapes=[pltpu.VMEM((NSA_KV, ka_w, w), BF16),
                        pltpu.VMEM((NSA_KV, 128, w), BF16),
                        pltpu.VMEM((NSA_KV, 1, w), F32),
                        pltpu.VMEM((NSA_KV, NSA_DIM + 8, w), F32),
                        pltpu.VMEM((NSA_KV, max(tk, tkw), w), BF16),
                        pltpu.VMEM((NSA_KV, max(tk, tkw), w), F32)],
        compiler_params=_cparams(("arbitrary", "arbitrary")),
        name="nsa_select_window",
    )(qn, ka, kw, vt, selb, smt, o_cmp_t)


def _ab_out_kernel(x_ref, mod_ref, hm_ref, w_ref, *rest):
    o_ref = rest[-1]
    o_nsa = rest[0][...]
    for part in rest[1:-1]:
        o_nsa = o_nsa + part[...]
    y = _dot(hm_ref[...], w_ref[:ML_WIDTH, :]) + _dot(o_nsa.astype(BF16), w_ref[ML_WIDTH:, :])
    o_ref[...] = x_ref[...] + mod_ref[0, 2] * y


def _ab_out(x, mod4, tiles_per_seq, hm, o_parts, w, tm):
    n = x.shape[0]
    tok = lambda wd: pl.BlockSpec((tm, wd), lambda i: (i, 0))
    return pl.pallas_call(
        _ab_out_kernel,
        grid=(n // tm,),
        in_specs=[tok(D_MODEL), _mod_spec(mod4, tm, tiles_per_seq), tok(ML_WIDTH),
                  pl.BlockSpec(w.shape, lambda i: (0, 0))] + [tok(NSA_WIDTH) for _ in o_parts],
        out_specs=tok(D_MODEL),
        out_shape=jax.ShapeDtypeStruct((n, D_MODEL), F32),
        compiler_params=_cparams(("arbitrary",)),
        name="ab_out",
    )(x, mod4, hm, w, *o_parts)


def _cl_kernel(x_ref, mod_ref, g_ref, wi_ref, bi_ref, vg_ref, ws_ref, bs_ref, wo_ref, o_ref, v_ref, *, tm):
    x = x_ref[...]
    h = _normmod(x, g_ref[...], mod_ref[0, 1], mod_ref[0, 0]).astype(BF16)
    z = _gelu(_dot(h, wi_ref[...]) + bi_ref[...])
    gw = z.shape[1] // 2
    u = z[:, :gw]
    v = z[:, gw:]
    v = v * lax.rsqrt(jnp.mean(v * v, axis=-1, keepdims=True) + EPS) * vg_ref[...]
    v_ref[...] = v
    vb = v.astype(BF16)
    lch = ws_ref.shape[1]
    tril = lax.broadcasted_iota(jnp.int32, (lch, lch), 0) >= lax.broadcasted_iota(jnp.int32, (lch, lch), 1)
    gd = gw // GM_GROUPS
    rows = []
    for c in range(tm // lch):
        cols = []
        for g in range(GM_GROUPS):
            wsg = jnp.where(tril, ws_ref[g], 0.0).astype(BF16)
            cols.append(_dot(wsg, vb[c * lch:(c + 1) * lch, g * gd:(g + 1) * gd]) + bs_ref[:, g:g + 1])
        rows.append(jnp.concatenate(cols, axis=1))
    s = jnp.concatenate(rows, axis=0) if len(rows) > 1 else rows[0]
    y = _dot((u * s).astype(BF16), wo_ref[...])
    o_ref[...] = x + mod_ref[0, 2] * y


def _cl_mixer(x, mod4, tiles_per_seq, g, wi, bi, vg, ws_eff, bs_eff, wo, tm):
    n = x.shape[0]
    gw = wo.shape[0]
    tok = lambda wd: pl.BlockSpec((tm, wd), lambda i: (i, 0))
    full = lambda a: pl.BlockSpec(a.shape, lambda i: (0,) * a.ndim)
    return pl.pallas_call(
        functools.partial(_cl_kernel, tm=tm),
        grid=(n // tm,),
        in_specs=[tok(D_MODEL), _mod_spec(mod4, tm, tiles_per_seq), full(g), full(wi), full(bi), full(vg),
                  full(ws_eff), full(bs_eff), full(wo)],
        out_specs=[tok(D_MODEL), tok(gw)],
        out_shape=[jax.ShapeDtypeStruct((n, D_MODEL), F32), jax.ShapeDtypeStruct((n, gw), F32)],
        compiler_params=_cparams(("arbitrary",)),
        name="gmlp_mixer",
    )(x, mod4, g, wi, bi, vg, ws_eff, bs_eff, wo)


def _ffn_kernel(x_ref, mod_ref, g_ref, w1_ref, w2_ref, fg_ref, o_ref, h_scr, acc_scr, *, final):
    j = pl.program_id(1)

    @pl.when(j == 0)
    def _():
        h_scr[...] = _normmod(x_ref[...], g_ref[...], mod_ref[0, 4], mod_ref[0, 3]).astype(BF16)
        acc_scr[...] = jnp.zeros(acc_scr.shape, F32)

    a = jnp.maximum(_dot(h_scr[...], w1_ref[0]), 0.0)
    acc_scr[...] += _dot((a * a).astype(BF16), w2_ref[0])

    @pl.when(j == pl.num_programs(1) - 1)
    def _():
        y = x_ref[...] + mod_ref[0, 5] * acc_scr[...]
        if final:
            y = y * lax.rsqrt(jnp.mean(y * y, axis=-1, keepdims=True) + EPS) * fg_ref[...]
        o_ref[...] = y


def _ffn(x, mod4, tiles_per_seq, g, w1, w2, layer, final_g, final, tm, tf):
    n = x.shape[0]
    f = w1.shape[2]
    r = mod4.shape[2]
    if r == 1:
        mspec = pl.BlockSpec((1, 6, 1, D_MODEL), lambda i, j: (i // tiles_per_seq, 0, 0, 0))
    else:
        mspec = pl.BlockSpec((1, 6, tm, D_MODEL), lambda i, j: (i, 0, 0, 0))
    return pl.pallas_call(
        functools.partial(_ffn_kernel, final=final),
        grid=(n // tm, f // tf),
        in_specs=[pl.BlockSpec((tm, D_MODEL), lambda i, j: (i, 0)), mspec,
                  pl.BlockSpec((1, D_MODEL), lambda i, j: (0, 0)),
                  pl.BlockSpec((1, D_MODEL, tf), lambda i, j: (layer, 0, j)),
                  pl.BlockSpec((1, tf, D_MODEL), lambda i, j: (layer, j, 0)),
                  pl.BlockSpec((1, D_MODEL), lambda i, j: (0, 0))],
        out_specs=pl.BlockSpec((tm, D_MODEL), lambda i, j: (i, 0)),
        out_shape=jax.ShapeDtypeStruct((n, D_MODEL), F32),
        scratch_shapes=[pltpu.VMEM((tm, D_MODEL), BF16), pltpu.VMEM((tm, D_MODEL), F32)],
        compiler_params=_cparams(("arbitrary", "arbitrary")),
        name="ffn",
    )(x, mod4, g, w1, w2, final_g)


SQ = 16


def _page_specs(pages_per_step, row_block):
    def in_map(p):
        return lambda b, s, pt: (pt[b, s * pages_per_step + p], row_block, 0)
    return [pl.BlockSpec((1, 256, PAGE_SIZE), in_map(p)) for p in range(pages_per_step)]


def _decode_cmp_kernel(pt_ref, *refs, pages_per_step, past, n_pick):
    pages = refs[:pages_per_step]
    (q_ref, sm_ref, ovl_ref, wc_ref, w1_ref, pe_ref, b1_ref, w2_ref, b2_ref, o_ref, sel_ref, x_scr) = refs[pages_per_step:]
    s_idx = pl.program_id(1)
    for p in range(pages_per_step):
        row0 = pl.multiple_of((s_idx * pages_per_step + p) * PAGE_SIZE, PAGE_SIZE)
        for kind in range(2):
            x_scr[kind, pl.ds(row0, PAGE_SIZE), :] = pages[p][0, 128 * kind:128 * (kind + 1), :].T

    @pl.when(s_idx == pl.num_programs(1) - 1)
    def _():
        n_sub = past // CMP_STRIDE
        nc = n_sub - 1
        kv = [_compress_math(lambda p: x_scr[kind, pl.ds(p, n_sub, stride=CMP_STRIDE), :],
                             lambda p: wc_ref[kind, p], w1_ref[kind], pe_ref[kind], b1_ref[kind],
                             w2_ref[kind], b2_ref[kind], n_sub) for kind in range(2)]
        q = q_ref[0]
        gates = _sigmoid(sm_ref[0])
        qpos = past + lax.broadcasted_iota(jnp.int32, (SQ, 1), 0)
        n_i = lax.broadcasted_iota(jnp.int32, (1, n_sub), 1)
        valid = (n_i * CMP_STRIDE + (CMP_LEN - 1) <= qpos) & (n_i < nc)
        outs, imps = [], []
        for g in range(NSA_KV):
            kc = kv[0][g].astype(BF16)
            vc = kv[1][g].astype(BF16)
            psum = jnp.zeros((SQ, n_sub), F32)
            for r in range(NSA_GROUP):
                hd = g * NSA_GROUP + r
                s = _dot_nt(q[:, NSA_DIM * hd:NSA_DIM * (hd + 1)], kc)
                s = jnp.where(valid, s, NEG)
                m = jnp.max(s, axis=1, keepdims=True)
                p = jnp.where(valid, jnp.exp2(s - m), 0.0)
                d = jnp.sum(p, axis=1, keepdims=True)
                p = p / jnp.where(d > 0, d, 1.0)
                psum = psum + p
                outs.append(_dot(p.astype(BF16), vc) * gates[:, 3 * hd:3 * hd + 1])
            psum = jnp.concatenate([psum, jnp.zeros((128 - SQ, n_sub), F32)], axis=0)
            imps.append(sum(_dot_nt(ovl_ref[...], part) for part in _split3(psum)))
        qpos_r = past + lax.broadcasted_iota(jnp.int32, (1, 128), 1)
        for g, selb in enumerate(_select_blocks(imps, qpos_r, n_pick)):
            sel_ref[0, g] = selb.T[0:SQ]
        o_ref[0] = jnp.concatenate(outs, axis=1)


def _decode_cmp(cache_t, page_table, qn, sm, ovl, cw, pages_per_step, n_pick):
    bsz, n_pages = page_table.shape
    past = n_pages * PAGE_SIZE
    nb = ovl.shape[0]
    full = lambda a: pl.BlockSpec(a.shape, lambda b, s, pt: (0,) * a.ndim)
    seq = lambda a: pl.BlockSpec((1,) + a.shape[1:], lambda b, s, pt: (b,) + (0,) * (a.ndim - 1))
    ws = [cw['wc'], cw['w1cat'], cw['pe'], cw['b1'], cw['w2'], cw['b2']]
    return pl.pallas_call(
        functools.partial(_decode_cmp_kernel, pages_per_step=pages_per_step, past=past, n_pick=n_pick),
        grid_spec=pltpu.PrefetchScalarGridSpec(
            num_scalar_prefetch=1,
            grid=(bsz, n_pages // pages_per_step),
            in_specs=_page_specs(pages_per_step, 0) + [seq(qn), seq(sm), full(ovl)] + [full(a) for a in ws],
            out_specs=[pl.BlockSpec((1, SQ, NSA_WIDTH), lambda b, s, pt: (b, 0, 0)),
                       pl.BlockSpec((1, NSA_KV, SQ, nb), lambda b, s, pt: (b, 0, 0, 0))],
            scratch_shapes=[pltpu.VMEM((2, past, 128), F32)]),
        out_shape=[jax.ShapeDtypeStruct((bsz, SQ, NSA_WIDTH), F32),
                   jax.ShapeDtypeStruct((bsz, NSA_KV, SQ, nb), F32)],
        compiler_params=_cparams(("arbitrary", "arbitrary")),
        name="decode_compress_select",
    )(page_table, *([cache_t] * pages_per_step), qn, sm, ovl, *ws)


def _decode_sw_kernel(pt_ref, *refs, pages_per_step, past, n_new):
    pages = refs[:pages_per_step]
    (et_ref, q_ref, sel_ref, kvn_ref, win_ref, sm_ref, o_ref, qa_scr, m_scr, l_scr, acc_scr) = refs[pages_per_step:]
    s_idx = pl.program_id(1)
    nb = sel_ref.shape[3]
    rows_n = NSA_HEADS * SQ
    wlen = win_ref.shape[2]

    @pl.when(s_idx == 0)
    def _():
        q = q_ref[0]
        qa_scr[...] = jnp.zeros(qa_scr.shape, F32)
        for hd in range(NSA_HEADS):
            g = hd // NSA_GROUP
            rows = slice(hd * SQ, (hd + 1) * SQ)
            qa_scr[rows, NSA_DIM * g:NSA_DIM * (g + 1)] = q[:, NSA_DIM * hd:NSA_DIM * (hd + 1)].astype(F32)
            qa_scr[rows, 128:128 + nb] = sel_ref[0, g]
        m_scr[...] = jnp.full(m_scr.shape, -3e38, F32)
        l_scr[...] = jnp.zeros(l_scr.shape, F32)
        acc_scr[...] = jnp.zeros(acc_scr.shape, F32)

    qa = qa_scr[...].astype(BF16)

    def online(s, pv):
        m_prev = m_scr[...]
        m_new = jnp.maximum(m_prev, jnp.max(s, axis=1, keepdims=True))
        alpha = jnp.exp2(m_prev - m_new)
        p = jnp.exp2(s - m_new[:, 0:1])
        l_scr[...] = alpha * l_scr[...] + jnp.sum(p, axis=1, keepdims=True)
        acc_scr[...] = alpha * acc_scr[...] + pv(p.astype(BF16))
        m_scr[...] = m_new

    n_pair = pages_per_step // 2
    s_parts = []
    for pair in range(n_pair):
        pa, pb = pages[2 * pair], pages[2 * pair + 1]
        kt = jnp.concatenate([pa[0, 0:128, :], pb[0, 0:128, :]], axis=1).astype(BF16)
        off = pl.multiple_of((s_idx * pages_per_step + 2 * pair) * PAGE_SIZE, 2 * PAGE_SIZE)
        rhs = jnp.concatenate([kt, et_ref[:, pl.ds(off, 2 * PAGE_SIZE)]], axis=0)
        s_parts.append(_dot(qa, rhs))

    def step_pv(p):
        out = None
        for pair in range(n_pair):
            pa, pb = pages[2 * pair], pages[2 * pair + 1]
            vt = jnp.concatenate([pa[0, 128:256, :], pb[0, 128:256, :]], axis=1).astype(BF16)
            d = _dot_nt(p[:, 2 * PAGE_SIZE * pair:2 * PAGE_SIZE * (pair + 1)], vt)
            out = d if out is None else out + d
        return out

    online(jnp.concatenate(s_parts, axis=1), step_pv)

    @pl.when(s_idx == pl.num_programs(1) - 1)
    def _():
        tok = lax.broadcasted_iota(jnp.int32, (rows_n, 1), 0) & (SQ - 1)
        new_i = lax.broadcasted_iota(jnp.int32, (1, SQ), 1)
        new_ok = (new_i <= tok) & (new_i < n_new)
        kvn = kvn_ref[0]
        q2 = qa[:, 0:128]
        online(jnp.where(new_ok, _dot_nt(q2, kvn[:, 256:384]), NEG), lambda p: _dot(p, kvn[:, 384:512]))
        o_s = acc_scr[...] / l_scr[...]
        s_w = _dot(q2, win_ref[0, 0:128, :].astype(BF16))
        w_i = lax.broadcasted_iota(jnp.int32, (1, wlen), 1)
        s_w = jnp.where(w_i >= tok + (wlen - WINDOW), s_w, NEG)
        s_n = jnp.where(new_ok, _dot_nt(q2, kvn[:, 512:640]), NEG)
        m_w = jnp.maximum(jnp.max(s_w, axis=1, keepdims=True), jnp.max(s_n, axis=1, keepdims=True))
        p_w = jnp.exp2(s_w - m_w)
        p_n = jnp.exp2(s_n - m_w)
        l_w = jnp.sum(p_w, axis=1, keepdims=True) + jnp.sum(p_n, axis=1, keepdims=True)
        o_w = (_dot_nt(p_w.astype(BF16), win_ref[0, 128:256, :].astype(BF16))
               + _dot(p_n.astype(BF16), kvn[:, 640:768])) / l_w
        gates = _sigmoid(sm_ref[0])
        g_s = jnp.concatenate([gates[:, 3 * hd + 1:3 * hd + 2] for hd in range(NSA_HEADS)], axis=0)
        g_w = jnp.concatenate([gates[:, 3 * hd + 2:3 * hd + 3] for hd in range(NSA_HEADS)], axis=0)
        o = o_s * g_s + o_w * g_w
        half = rows_n // 2
        o_ref[0, 0:half, :] = o[0:half, 0:NSA_DIM]
        o_ref[0, half:, :] = o[half:, NSA_DIM:]


def _decode_sw(cache_t, page_table, et, qn, selb, kvn, win_t, sm, pages_per_step, n_new):
    bsz, n_pages = page_table.shape
    past = n_pages * PAGE_SIZE
    nb = et.shape[0]
    rows_n = NSA_HEADS * SQ
    full = lambda a: pl.BlockSpec(a.shape, lambda b, s, pt: (0,) * a.ndim)
    seq = lambda a: pl.BlockSpec((1,) + a.shape[1:], lambda b, s, pt: (b,) + (0,) * (a.ndim - 1))
    return pl.pallas_call(
        functools.partial(_decode_sw_kernel, pages_per_step=pages_per_step, past=past, n_new=n_new),
        grid_spec=pltpu.PrefetchScalarGridSpec(
            num_scalar_prefetch=1,
            grid=(bsz, n_pages // pages_per_step),
            in_specs=_page_specs(pages_per_step, 1) + [full(et), seq(qn), seq(selb), seq(kvn), seq(win_t), seq(sm)],
            out_specs=pl.BlockSpec((1, rows_n, NSA_DIM), lambda b, s, pt: (b, 0, 0)),
            scratch_shapes=[pltpu.VMEM((rows_n, 128 + nb), F32),
                            pltpu.VMEM((rows_n, 128), F32),
                            pltpu.VMEM((rows_n, 128), F32),
                            pltpu.VMEM((rows_n, 128), F32)]),
        out_shape=jax.ShapeDtypeStruct((bsz, rows_n, NSA_DIM), F32),
        compiler_params=_cparams(("arbitrary", "arbitrary")),
        name="decode_select_window",
    )(page_table, *([cache_t] * pages_per_step), et, qn, selb, kvn, win_t, sm)


def _selection_constants(n_keys):
    nb = n_keys // SEL_BLOCK
    n_sub = n_keys // CMP_STRIDE
    i = np.arange(n_sub)[None, :] * CMP_STRIDE
    j = np.arange(nb)[:, None] * SEL_BLOCK
    ovl_t = ((i < j + SEL_BLOCK) & (i + CMP_LEN > j) & (np.arange(n_sub)[None, :] < n_sub - 1))
    et = (np.arange(n_keys)[None, :] // SEL_BLOCK) == np.arange(nb)[:, None]
    return jnp.asarray(ovl_t, BF16), jnp.asarray(et, BF16)


def _ab_layer_prompt(x, mod4, tps, W, l, bsz, t, cfg):
    n = bsz * t
    j = l // 2
    tm, lc = cfg['tm'], cfg['lc']
    qk, v_m, o_pre, qn, kvr, win, sm, ka, kw, vt, smt = _ab_in(x, mod4, tps, W['norm_g0'][l], W['ab_w_in'][j],
                                                          W['ab_b_in'][j], tm, seq_shape=(bsz, t))
    seq = lambda a: a.reshape(bsz, t, a.shape[1])
    conv0 = jnp.zeros((bsz, CONV_W - 1, 2 * ML_WIDTH), F32)
    c0 = jnp.zeros((bsz, ML_HEADS, ML_DIM, ML_DIM), F32)
    n0 = jnp.zeros((bsz, ML_HEADS, ML_DIM), F32)
    m0 = jnp.zeros((bsz, ML_HEADS), F32)
    hm, c1, n1, m1 = _mlstm(seq(qk), seq(v_m), seq(o_pre), seq(sm), smt, W['ml_conv_w'][j], W['ml_conv_b'][j],
                            W['ml_f_bias'][j], W['ml_out_g'][j], conv0, c0, n0, m0, lc, gr_block=_SM_I // 8)
    conv_new = seq(qk)[:, -(CONV_W - 1):]
    kv_rows = kvr.reshape(bsz, t, 4, NSA_KV, NSA_DIM)
    win_rows = win.reshape(bsz, t, 2, NSA_KV, NSA_DIM)
    kvc, kvct = _compress(seq(kvr), W['cmp'][j])
    ovl_t, _ = _selection_constants(t)
    o_cmp_t, selb = _nsa_cmp(seq(qn), kvc, kvct, smt, ovl_t, cfg['tq_cmp'], N_SEL, t // CMP_STRIDE - 1)
    o_nsa = _nsa_sw(seq(qn), seq(ka), seq(kw), vt, selb, smt, o_cmp_t, cfg['tq'], cfg['tk'], cfg['tkw'])
    x = _ab_out(x, mod4, tps, hm.reshape(n, ML_WIDTH), [o_nsa.reshape(n, NSA_WIDTH)], W['ab_w_out'][j], tm)
    return x, (conv_new, c1, n1, m1, kv_rows, win_rows[:, -min(WINDOW, t):])


def _ab_layer_decode(x, mod4, W, l, bsz, t, cfg, st, page_table):
    n = bsz * t
    j = l // 2
    tm, lc = cfg['tm'], cfg['lc']
    conv0, c0, n0, m0, (cache_t, page0), win_buf, win_t = st
    page_table = page_table + page0
    qk, v_m, o_pre, qn, kvr, win, sm, kvb = _ab_in(x, mod4, 1, W['norm_g0'][l], W['ab_w_in'][j], W['ab_b_in'][j], tm)
    seq = lambda a: a.reshape(bsz, t, a.shape[1])
    pad_t = lambda a, tp: jnp.pad(a, ((0, 0), (0, tp - t), (0, 0)))
    sm3 = seq(sm)
    gr = sm3[:, :, _SM_I:_SM_I + 8].transpose(0, 2, 1)
    gr = jnp.concatenate([jnp.pad(gr[:, :4], ((0, 0), (0, 0), (0, lc - t)), constant_values=NEG),
                          jnp.pad(gr[:, 4:], ((0, 0), (0, 0), (0, lc - t)), constant_values=-NEG)], axis=1)
    sm_pad = jnp.zeros((bsz, lc - t, 128), F32).at[:, :, _SM_I:_SM_I + 4].set(NEG).at[:, :, _SM_F:_SM_F + 4].set(-NEG)
    hm, c1, n1, m1 = _mlstm(pad_t(seq(qk), lc), pad_t(seq(v_m), lc), pad_t(seq(o_pre), lc),
                            jnp.concatenate([sm3, sm_pad], axis=1), gr, W['ml_conv_w'][j], W['ml_conv_b'][j],
                            W['ml_f_bias'][j], W['ml_out_g'][j], conv0, c0, n0, m0, lc)
    hm = hm[:, :t].reshape(n, ML_WIDTH)
    conv_new = jnp.concatenate([conv0, seq(qk)], axis=1)[:, -(CONV_W - 1):]
    kv_rows = kvr.reshape(bsz, t, 4, NSA_KV, NSA_DIM)
    win_rows = win.reshape(bsz, t, 2, NSA_KV, NSA_DIM)
    past = page_table.shape[1] * PAGE_SIZE
    assert (past + t) // CMP_STRIDE == past // CMP_STRIDE and t <= min(SEL_BLOCK, SQ) and past % SEL_BLOCK == 0
    ovl_t, et = _selection_constants(past)
    qn3, sm3q, kvn = pad_t(seq(qn), SQ), pad_t(sm3, SQ), pad_t(seq(kvb), SQ)
    pps = cfg['pages_per_step']
    o_cmp, selb = _decode_cmp(cache_t, page_table, qn3, sm3q, ovl_t, W['cmp'][j], pps, N_SEL - 1)
    o_sw = _decode_sw(cache_t, page_table, et, qn3, selb, kvn, win_t, sm3q, pps, t)
    o_cmp = o_cmp[:, :t].reshape(n, NSA_WIDTH)
    o_sw = o_sw.reshape(bsz, NSA_HEADS, SQ, NSA_DIM).transpose(0, 2, 1, 3)[:, :t].reshape(n, NSA_WIDTH)
    x = _ab_out(x, mod4, 1, hm, [o_cmp, o_sw], W['ab_w_out'][j], tm)
    win_new = jnp.concatenate([win_buf, win_rows], axis=1)[:, -win_buf.shape[1]:]
    return x, (conv_new, c1, n1, m1, kv_rows, win_new)


def _trunk(x3, mods, W, state, page_table, cfg):
    bsz, t, _ = x3.shape
    n = bsz * t
    tm = cfg['tm']
    x = x3.reshape(n, D_MODEL)
    depth = mods.shape[0]
    ab_new, cl_new = [], []
    if t % tm == 0:
        tps = t // tm
        to_mod4 = lambda m: m.reshape(bsz, 6, 1, D_MODEL)
    else:
        assert n == tm
        tps = 1
        to_mod4 = lambda m: jnp.repeat(m.reshape(bsz, 6, D_MODEL), t, axis=0).reshape(n, 6, D_MODEL).transpose(1, 0, 2)[None]
    lch = min(GM_CHUNK, t)
    for l in range(depth):
        mod4 = to_mod4(mods[l])
        j = l // 2
        if l % 2 == 0:
            if state is None:
                x, new = _ab_layer_prompt(x, mod4, tps, W, l, bsz, t, cfg)
            else:
                x, new = _ab_layer_decode(x, mod4, W, l, bsz, t, cfg, tuple(a[j] for a in state), page_table)
            ab_new.append(new)
        else:
            ws = W['cl_ws'][j][:, :lch, :lch]
            bs = W['cl_bs'][j][:, :lch]
            if lch < GM_CHUNK:
                rep = GM_CHUNK // lch
                ws = jnp.einsum('ab,gts->gatbs', jnp.eye(rep, dtype=F32), ws).reshape(GM_GROUPS, GM_CHUNK, GM_CHUNK)
                bs = jnp.tile(bs, (1, rep))
            x, v = _cl_mixer(x, mod4, tps, W['norm_g0'][l], W['cl_w_in'][j], W['cl_b_in'][j], W['cl_v_g'][j],
                             ws, bs.T, W['cl_w_out'][j], tm)
            cl_new.append(v.reshape(bsz, t, -1))
        x = _ffn(x, mod4, max(t // cfg['tm_ffn'], 1), W['norm_g1'][l], W['ffn_w1'], W['ffn_w2'], l, W['final_g'],
                 l == depth - 1, cfg['tm_ffn'], cfg['tf'])
    return x.reshape(bsz, t, D_MODEL), ab_new, cl_new


def kernel(x_prompt, x_sample, c_prompt, c_sample, state_mlstm_conv, state_mlstm_C, state_mlstm_n,
           state_mlstm_m, cache_nsa_kv, state_nsa_win, page_table, ada_w, ada_b, norm_g, ab_w_in, ab_b_in,
           ml_conv_w, ml_conv_b, ml_f_bias, ml_out_g, phi_pe, phi_w1, phi_b1, phi_w2, phi_b2, ab_w_out,
           cl_w_in, cl_b_in, cl_v_g, cl_ws, cl_bs, cl_w_out, ffn_w1, ffn_w2, final_g):
    depth = ada_w.shape[0]
    n_ab = ab_w_in.shape[0]
    bp, bs_ = c_prompt.shape[0], c_sample.shape[0]
    rows = bp + bs_
    rows_pad = -(-rows // 8) * 8
    c_all = jnp.concatenate([c_prompt, c_sample, jnp.zeros((rows_pad - rows, D_MODEL), F32)], axis=0)
    mods = _ada_mod(c_all, ada_w, ada_b)
    mods_p = mods[:, :bp]
    mods_s = mods[:, bp:rows]
    w_in = jnp.concatenate([ab_w_in[:, :, 0:2048], ab_w_in[:, :, 2056:3336], ab_w_in[:, :, 3336:3360],
                            ab_w_in[:, :, 2048:2056], jnp.zeros((n_ab, D_MODEL, 96), F32)], axis=2).astype(BF16)
    b_in = jnp.concatenate([ab_b_in[:, 0:2048], ab_b_in[:, 2056:3336], ab_b_in[:, 3336:3360],
                            ab_b_in[:, 2048:2056], jnp.zeros((n_ab, 96), F32)], axis=1)[:, None, :]
    half = CMP_STRIDE * NSA_DIM
    blk = phi_w1.reshape(n_ab, 2, 2, CMP_STRIDE, NSA_DIM, NSA_DIM).transpose(0, 1, 3, 4, 2, 5)
    blk = blk.reshape(n_ab, 2, CMP_STRIDE, NSA_DIM, 2 * NSA_DIM)
    zero = jnp.zeros_like(blk)
    wc = jnp.concatenate([jnp.concatenate([blk, zero], axis=4), jnp.concatenate([zero, blk], axis=4)], axis=3)
    pe = jnp.pad(phi_pe.reshape(n_ab, 2, 2, half), ((0, 0), (0, 0), (0, 6), (0, 0)))
    cmp_w = [dict(wc=wc[j].astype(BF16),
                  w1cat=jnp.concatenate([phi_w1[j, :, :half], phi_w1[j, :, half:]], axis=2).astype(BF16),
                  pe=pe[j], b1=phi_b1[j][:, None, :], w2=phi_w2[j].astype(BF16), b2=phi_b2[j][:, None, :])
             for j in range(n_ab)]
    W = dict(
        norm_g0=norm_g[:, 0][:, None, :], norm_g1=norm_g[:, 1][:, None, :],
        ab_w_in=w_in, ab_b_in=b_in,
        ml_conv_w=ml_conv_w, ml_conv_b=ml_conv_b[:, None, :], ml_f_bias=ml_f_bias, ml_out_g=ml_out_g[:, None, :],
        cmp=cmp_w,
        ab_w_out=ab_w_out.astype(BF16),
        cl_w_in=cl_w_in.astype(BF16), cl_b_in=cl_b_in[:, None, :], cl_v_g=cl_v_g[:, None, :],
        cl_ws=cl_ws, cl_bs=cl_bs, cl_w_out=cl_w_out.astype(BF16),
        ffn_w1=ffn_w1.astype(BF16), ffn_w2=ffn_w2.astype(BF16), final_g=final_g[None, :])
    cfg_p = dict(tm=256, tm_ffn=512, tf=1024, lc=512, tq_cmp=128, tq=256, tk=512, tkw=256)
    y_prompt, ab_p, _ = _trunk(x_prompt, mods_p, W, None, None, cfg_p)
    n_s = x_sample.shape[0] * x_sample.shape[1]
    cfg_s = dict(tm=n_s, tm_ffn=n_s, tf=512, lc=128, pages_per_step=16)
    n_pool = cache_nsa_kv.shape[1]
    cache_t = cache_nsa_kv.transpose(0, 1, 3, 4, 5, 2).reshape(n_ab * n_pool, 4 * NSA_KV * NSA_DIM, PAGE_SIZE)
    cache_t = [(cache_t, j * n_pool) for j in range(n_ab)]
    wlen = state_nsa_win.shape[2]
    win_t = state_nsa_win.transpose(0, 1, 3, 4, 5, 2).reshape(n_ab, bs_, 2 * NSA_KV * NSA_DIM, wlen)
    state = (state_mlstm_conv, state_mlstm_C, state_mlstm_n, state_mlstm_m, cache_t, state_nsa_win, win_t)
    y_sample, ab_s, cl_s = _trunk(x_sample, mods_s, W, state, page_table, cfg_s)
    p_out = [jnp.stack(a) for a in zip(*ab_p)]
    s_out = [jnp.stack(a) for a in zip(*ab_s)]
    return (y_prompt, y_sample, *p_out, *s_out, jnp.stack(cl_s))
```

```python
import functools

import numpy as np
import jax
import jax.numpy as jnp
from jax import lax
from jax.experimental import pallas as pl
from jax.experimental.pallas import tpu as pltpu

F32 = jnp.float32
BF16 = jnp.bfloat16

EPS = 1e-6
D_MODEL = 1024
ML_HEADS = 4
ML_DIM = 128
ML_WIDTH = ML_HEADS * ML_DIM
CONV_W = 4
NSA_HEADS = 8
NSA_KV = 2
NSA_GROUP = NSA_HEADS // NSA_KV
NSA_DIM = 64
NSA_WIDTH = NSA_HEADS * NSA_DIM
CMP_LEN = 32
CMP_STRIDE = 16
SEL_BLOCK = 64
N_SEL = 16
WINDOW = 512
GM_GROUPS = 4
GM_CHUNK = 128
PAGE_SIZE = 128

_SEG_QK = (0, 1024)
_SEG_V = (1024, 1536)
_SEG_O = (1536, 2048)
_SEG_QN = (2048, 2560)
_SEG_KV = (2560, 3328)
_SEG_SM = (3328, 3456)
_AB_COLS = 3456
_SM_I = 24
_SM_F = 28

NEG = -1e30
SEL_BIG = 2.0 ** 100
LOG2E = 1.4426950408889634
VMEM_LIMIT = 56 * 1024 * 1024


def _cparams(sem):
    return pltpu.CompilerParams(dimension_semantics=sem, vmem_limit_bytes=VMEM_LIMIT)


def _sigmoid(x):
    return 1.0 / (1.0 + jnp.exp(-x))


def _log_sigmoid(x):
    return jnp.minimum(x, 0.0) - jnp.log1p(jnp.exp(-jnp.abs(x)))


def _gelu(x):
    return 0.5 * x * (1.0 + jnp.tanh(0.7978845608028654 * (x + 0.044715 * (x * x * x))))


def _normmod(x, g, scale, shift):
    y = x * lax.rsqrt(jnp.mean(x * x, axis=-1, keepdims=True) + EPS) * g
    return y * (1.0 + scale) + shift


def _dot(a, b):
    return jnp.dot(a, b, preferred_element_type=F32)


def _dot_nt(a, b):
    return lax.dot_general(a, b, (((1,), (1,)), ((), ())), preferred_element_type=F32)


def _split3(x):
    hi = x.astype(BF16)
    r = x - hi.astype(F32)
    mid = r.astype(BF16)
    lo = (r - mid.astype(F32)).astype(BF16)
    return hi, mid, lo


def _ada_kernel(c_ref, w_ref, b_ref, o_ref):
    c = c_ref[...]
    sc = c * _sigmoid(c)
    sc_hi = sc.astype(BF16)
    sc_lo = (sc - sc_hi.astype(F32)).astype(BF16)
    w = w_ref[0]
    w_hi = w.astype(BF16)
    w_lo = (w - w_hi.astype(F32)).astype(BF16)
    acc = _dot(sc_hi, w_hi) + _dot(sc_lo, w_hi) + _dot(sc_hi, w_lo)
    o_ref[0] = acc + b_ref[0]


def _ada_mod(c_all, ada_w, ada_b):
    depth, d, n = ada_w.shape
    bp = c_all.shape[0]
    tn = 1536
    return pl.pallas_call(
        _ada_kernel,
        grid=(depth, n // tn),
        in_specs=[pl.BlockSpec((bp, d), lambda l, j: (0, 0)),
                  pl.BlockSpec((1, d, tn), lambda l, j: (l, 0, j)),
                  pl.BlockSpec((1, 1, tn), lambda l, j: (l, 0, j))],
        out_specs=pl.BlockSpec((1, bp, tn), lambda l, j: (l, 0, j)),
        out_shape=jax.ShapeDtypeStruct((depth, bp, n), F32),
        compiler_params=_cparams(("arbitrary", "arbitrary")),
        name="ada_mod",
    )(c_all, ada_w, ada_b.reshape(depth, 1, n))


def _mod_spec(mod4, tm, tiles_per_seq):
    r = mod4.shape[2]
    if r == 1:
        return pl.BlockSpec((1, 6, 1, D_MODEL), lambda i: (i // tiles_per_seq, 0, 0, 0))
    assert r == tm
    return pl.BlockSpec((1, 6, tm, D_MODEL), lambda i: (i, 0, 0, 0))


def _ab_in_kernel(x_ref, mod_ref, g_ref, w_ref, b_ref, *rest, tm, tiles_per_seq, nb, seq_layouts):
    h = _normmod(x_ref[...], g_ref[...], mod_ref[0, 1], mod_ref[0, 0]).astype(BF16)

    def seg(ab):
        a, b = ab
        return _dot(h, w_ref[:, a:b]) + b_ref[:, a:b]

    if seq_layouts:
        wt_ref, bt_ref, qk_ref, v_ref, o_ref, qn_ref, kvr_ref, win_ref, sm_ref, ka_ref, kw_ref, vt_ref, smt_ref = rest
    else:
        qk_ref, v_ref, o_ref, qn_ref, kvr_ref, win_ref, sm_ref, kvb_ref = rest
    qk_ref[...] = seg(_SEG_QK)
    v_ref[...] = seg(_SEG_V).astype(BF16)
    o_ref[...] = seg(_SEG_O)
    qn_ref[...] = (seg(_SEG_QN) * (NSA_DIM ** -0.5 * LOG2E)).astype(BF16)
    kv = seg(_SEG_KV)
    kvr_ref[...] = kv[:, :512]
    win_ref[...] = kv[:, 512:]
    sm_ref[...] = seg(_SEG_SM)
    if not seq_layouts:
        kvb_ref[...] = kv.astype(BF16)
        return
    pos = (pl.program_id(0) % tiles_per_seq) * tm + lax.broadcasted_iota(jnp.int32, (tm, nb), 0)
    onehot = jnp.where(pos // SEL_BLOCK == lax.broadcasted_iota(jnp.int32, (tm, nb), 1), 1.0, 0.0).astype(BF16)
    ka_w = ka_ref.shape[1] // NSA_KV
    pieces = []
    for g in range(NSA_KV):
        pieces += [onehot, kv[:, 256 + NSA_DIM * g:256 + NSA_DIM * (g + 1)].astype(BF16)]
        if ka_w > nb + NSA_DIM:
            pieces.append(jnp.zeros((tm, ka_w - nb - NSA_DIM), BF16))
    ka_ref[...] = jnp.concatenate(pieces, axis=1)
    kw_ref[...] = kv[:, 512:640].astype(BF16)
    zt = _dot_nt(wt_ref[...], h) + bt_ref[...]
    vt_ref[0] = zt[:256].astype(BF16)
    smt_ref[0] = zt[256:]


def _ab_in(x, mod4, tiles_per_seq, g, w, b, tm, seq_shape=None):
    n = x.shape[0]
    widths = [(1024, F32), (512, BF16), (512, F32), (512, BF16), (512, F32), (256, F32), (128, F32)]
    ins = [x, mod4, g, w, b]
    in_specs = [pl.BlockSpec((tm, D_MODEL), lambda i: (i, 0)),
                _mod_spec(mod4, tm, tiles_per_seq),
                pl.BlockSpec((1, D_MODEL), lambda i: (0, 0)),
                pl.BlockSpec((D_MODEL, _AB_COLS), lambda i: (0, 0)),
                pl.BlockSpec((1, _AB_COLS), lambda i: (0, 0))]
    nb = 0
    if seq_shape is None:
        widths.append((768, BF16))
    else:
        bsz, t = seq_shape
        nb = t // SEL_BLOCK
        ka_w = -(-(nb + NSA_DIM) // 128) * 128
        widths += [(NSA_KV * ka_w, BF16), (128, BF16)]
        a, c = _SEG_KV[0], _SEG_SM[0]
        cols = jnp.concatenate([w[:, a + 384:a + 512], w[:, a + 640:a + 768], w[:, c:c + 128]], axis=1)
        bcols = jnp.concatenate([b[:, a + 384:a + 512], b[:, a + 640:a + 768], b[:, c:c + 128]], axis=1)
        ins += [cols.T, bcols.T]
        in_specs += [pl.BlockSpec((384, D_MODEL), lambda i: (0, 0)), pl.BlockSpec((384, 1), lambda i: (0, 0))]
    out_specs = [pl.BlockSpec((tm, wd), lambda i: (i, 0)) for wd, _ in widths]
    out_shape = [jax.ShapeDtypeStruct((n, wd), dt) for wd, dt in widths]
    if seq_shape is not None:
        seq_map = lambda i: (i // tiles_per_seq, 0, i % tiles_per_seq)
        out_specs += [pl.BlockSpec((1, 256, tm), seq_map), pl.BlockSpec((1, 128, tm), seq_map)]
        out_shape += [jax.ShapeDtypeStruct((bsz, 256, t), BF16), jax.ShapeDtypeStruct((bsz, 128, t), F32)]
    return pl.pallas_call(
        functools.partial(_ab_in_kernel, tm=tm, tiles_per_seq=tiles_per_seq, nb=nb,
                          seq_layouts=seq_shape is not None),
        grid=(n // tm,),
        in_specs=in_specs,
        out_specs=out_specs,
        out_shape=out_shape,
        compiler_params=_cparams(("arbitrary",)),
        name="ab_in",
    )(*ins)


def _mlstm_kernel(qk_ref, v_ref, o_ref, sm_ref, gr_ref, cw_ref, cb_ref, fbc_ref, fbr_ref, og_ref,
                  conv0_ref, c0_ref, n0_ref, m0_ref,
                  hm_ref, cout_ref, nout_ref, mout_ref,
                  xp_scr, c_scr, n_scr, m_scr, *, lc):
    c = pl.program_id(1)

    @pl.when(c == 0)
    def _():
        xp_scr[0:8, :] = conv0_ref[0]
        c_scr[...] = c0_ref[0]
        n_scr[...] = n0_ref[0]
        m_scr[...] = m0_ref[0]

    xp_scr[8:8 + lc, :] = qk_ref[0]
    y = cb_ref[...]
    for j in range(CONV_W):
        y = y + xp_scr[5 + j:5 + j + lc, :] * cw_ref[j:j + 1, :]
    xp_scr[0:8, :] = xp_scr[lc:lc + 8, :]
    act = y * _sigmoid(y)
    q_all = act[:, :ML_WIDTH]
    k_all = act[:, ML_WIDTH:] * (ML_DIM ** -0.5)

    sm = sm_ref[0]
    gr = gr_ref[0]
    lf_cols = _log_sigmoid(sm + fbc_ref[...])
    lf_rows = _log_sigmoid(gr + fbr_ref[...])
    v_all = v_ref[0]
    o_all = o_ref[0]

    row_i = lax.broadcasted_iota(jnp.int32, (lc, lc), 0)
    col_i = lax.broadcasted_iota(jnp.int32, (lc, lc), 1)
    tril = row_i >= col_i
    triu = row_i <= col_i

    outs = []
    for h in range(ML_HEADS):
        hs = slice(ML_DIM * h, ML_DIM * (h + 1))
        li_c = sm[:, _SM_I + h:_SM_I + h + 1]
        lf_c = lf_cols[:, _SM_F + h:_SM_F + h + 1]
        li_r = gr[h:h + 1, :]
        lf_r = lf_rows[4 + h:5 + h, :]
        b_c = jnp.sum(jnp.where(tril, lf_r, 0.0), axis=1, keepdims=True)
        b_r = jnp.sum(jnp.where(triu, lf_c, 0.0), axis=0, keepdims=True)
        a_r = li_r - b_r
        a_c = li_c - b_c
        m_h = m_scr[h:h + 1, 0:1]
        cm_c = jnp.maximum(m_h, jnp.max(jnp.where(tril, a_r, NEG), axis=1, keepdims=True))
        dm = jnp.exp(jnp.where(tril, a_r - cm_c, NEG))
        w_int = jnp.exp(m_h - cm_c)
        qh = q_all[:, hs]
        kh = k_all[:, hs]
        vh = v_all[:, hs]
        qb = qh.astype(BF16)
        kb = kh.astype(BF16)
        s = _dot_nt(qb, kb) * dm
        c_old = c_scr[h]
        n_old = n_scr[h:h + 1, :]
        num = w_int * _dot(qb, c_old.astype(BF16)) + _dot(s.astype(BF16), vh)
        den = w_int * jnp.sum(qh * n_old, axis=1, keepdims=True) + jnp.sum(s, axis=1, keepdims=True)
        mt = b_c + cm_c
        hh = num / jnp.maximum(jnp.abs(den), jnp.exp(-mt))
        hn = hh * lax.rsqrt(jnp.mean(hh * hh, axis=1, keepdims=True) + EPS) * og_ref[:, hs]
        outs.append(hn * _sigmoid(o_all[:, hs]))
        cm_last = jnp.maximum(m_h, jnp.max(a_r, axis=1, keepdims=True))
        bl = jnp.sum(lf_r, axis=1, keepdims=True)
        decay = jnp.exp(m_h - cm_last)
        ws_c = jnp.exp(a_c - cm_last)
        kt = kh.T.astype(BF16)
        c_scr[h] = decay * c_old + _dot(kt, (ws_c * vh.astype(F32)).astype(BF16))
        n_scr[h:h + 1, :] = decay * n_old + jnp.sum(ws_c * kh, axis=0, keepdims=True)
        m_scr[h:h + 1, :] = jnp.broadcast_to(bl + cm_last, (1, ML_DIM))

    hm_ref[0] = jnp.concatenate(outs, axis=1).astype(BF16)

    @pl.when(c == pl.num_programs(1) - 1)
    def _():
        cout_ref[0] = c_scr[...]
        nout_ref[0] = n_scr[...]
        mout_ref[0] = m_scr[...]


def _mlstm(qk, v, o, sm, gr, conv_w, conv_b, f_bias, out_g, conv0, c0, n0, m0, lc, gr_block=0):
    bsz, t, _ = qk.shape
    nc = t // lc
    fbc = jnp.zeros((1, 128), F32).at[0, _SM_F:_SM_F + ML_HEADS].set(f_bias)
    fbr = jnp.zeros((8, 1), F32).at[4:8, 0].set(f_bias)
    conv0p = jnp.concatenate([jnp.zeros((bsz, 5, 2 * ML_WIDTH), F32), conv0], axis=1)
    m0p = jnp.broadcast_to(m0[:, :, None], (bsz, ML_HEADS, ML_DIM))
    tok = lambda wd: pl.BlockSpec((1, lc, wd), lambda b, c: (b, c, 0))
    full2 = lambda a: pl.BlockSpec(a.shape, lambda b, c: (0, 0))
    st3 = lambda a: pl.BlockSpec((1,) + a.shape[1:], lambda b, c: (b,) + (0,) * (a.ndim - 1))
    hm, c1, n1, m1 = pl.pallas_call(
        functools.partial(_mlstm_kernel, lc=lc),
        grid=(bsz, nc),
        in_specs=[tok(1024), tok(512), tok(512), tok(128),
                  pl.BlockSpec((1, 8, lc), lambda b, c: (b, gr_block, c)),
                  full2(conv_w), full2(conv_b), full2(fbc), full2(fbr), full2(out_g),
                  st3(conv0p), st3(c0), st3(n0), st3(m0p)],
        out_specs=[tok(512), st3(c0), st3(n0), st3(m0p)],
        out_shape=[jax.ShapeDtypeStruct((bsz, t, ML_WIDTH), BF16),
                   jax.ShapeDtypeStruct(c0.shape, F32),
                   jax.ShapeDtypeStruct(n0.shape, F32),
                   jax.ShapeDtypeStruct(m0p.shape, F32)],
        scratch_shapes=[pltpu.VMEM((lc + 8, 2 * ML_WIDTH), F32),
                        pltpu.VMEM((ML_HEADS, ML_DIM, ML_DIM), F32),
                        pltpu.VMEM((ML_HEADS, ML_DIM), F32),
                        pltpu.VMEM((ML_HEADS, ML_DIM), F32)],
        compiler_params=_cparams(("arbitrary", "arbitrary")),
        name="mlstm",
    )(qk, v, o, sm, gr, conv_w, conv_b, fbc, fbr, out_g, conv0p, c0, n0, m0p)
    return hm, c1, n1, m1[:, :, 0]


def _compress_math(x_at, wc_at, w1cat, pe, b1, w2, b2, n_sub):
    acc = None
    for p in range(CMP_STRIDE):
        d = _dot(x_at(p).astype(BF16), wc_at(p))
        acc = d if acc is None else acc + d
    pe_hi = pe.astype(BF16)
    pe_lo = (pe - pe_hi.astype(F32)).astype(BF16)
    pe_c = _dot(pe_hi, w1cat) + _dot(pe_lo, w1cat)
    const = b1 + pe_c[0:1, :NSA_DIM] + pe_c[1:2, NSA_DIM:]
    outs = []
    for g in range(NSA_KV):
        p0 = acc[:, 128 * g:128 * g + NSA_DIM]
        p1 = pltpu.roll(acc[:, 128 * g + NSA_DIM:128 * (g + 1)], n_sub - 1, 0)
        outs.append(_dot(_gelu(const + p0 + p1).astype(BF16), w2) + b2)
    return outs


def _compress_kernel(x_ref, wc_ref, w1_ref, pe_ref, b1_ref, w2_ref, b2_ref, o_ref, ot_ref):
    n_sub = o_ref.shape[3]
    outs = _compress_math(lambda p: x_ref[0, pl.ds(p, n_sub, stride=CMP_STRIDE), :], lambda p: wc_ref[0, p],
                          w1_ref[0], pe_ref[0], b1_ref[0], w2_ref[0], b2_ref[0], n_sub)
    for g in range(NSA_KV):
        o_ref[0, 0, g] = outs[g].astype(BF16)
        ot_ref[0, 0, g] = outs[g].T.astype(BF16)


def _compress(kvr3, cw):
    bsz, t, _ = kvr3.shape
    n_sub = t // CMP_STRIDE
    kind = lambda a: pl.BlockSpec((1,) + a.shape[1:], lambda k, b: (k,) + (0,) * (a.ndim - 1))
    ws = [cw['wc'], cw['w1cat'], cw['pe'], cw['b1'], cw['w2'], cw['b2']]
    return pl.pallas_call(
        _compress_kernel,
        grid=(2, bsz),
        in_specs=[pl.BlockSpec((1, t, 128), lambda k, b: (b, 0, k))] + [kind(a) for a in ws],
        out_specs=[pl.BlockSpec((1, 1, NSA_KV, n_sub, NSA_DIM), lambda k, b: (k, b, 0, 0, 0)),
                   pl.BlockSpec((1, 1, NSA_KV, NSA_DIM, n_sub), lambda k, b: (k, b, 0, 0, 0))],
        out_shape=[jax.ShapeDtypeStruct((2, bsz, NSA_KV, n_sub, NSA_DIM), BF16),
                   jax.ShapeDtypeStruct((2, bsz, NSA_KV, NSA_DIM, n_sub), BF16)],
        compiler_params=_cparams(("arbitrary", "arbitrary")),
        name="nsa_compress",
    )(kvr3, *ws)


def _select_blocks(imps, qpos_r, n_pick):
    nb, cols = imps[0].shape
    blk = lax.broadcasted_iota(jnp.int32, (nb, cols), 0)
    cur = qpos_r // SEL_BLOCK
    avail = blk * SEL_BLOCK <= qpos_r
    forced = (blk == 0) | (blk == cur) | (blk == cur - 1)
    vals = tuple(jnp.where(avail, jnp.where(forced, 1e9, imp), -1.0) for imp in imps)

    def pick(_, carry):
        out = []
        for val, sel in zip(carry[0], carry[1]):
            mx = jnp.max(val, axis=0, keepdims=True)
            first = jnp.min(jnp.where(val == mx, blk, nb), axis=0, keepdims=True)
            hit = blk == first
            out.append((jnp.where(hit, -2.0, val), jnp.where(hit, 1.0, sel)))
        return tuple(o[0] for o in out), tuple(o[1] for o in out)

    _, sels = lax.fori_loop(0, n_pick, pick, (vals, tuple(jnp.zeros((nb, cols), F32) for _ in imps)))
    return [(sel - 1.0) * SEL_BIG for sel in sels]


def _nsa_cmp_kernel(q_ref, kc_ref, vct_ref, gt_ref, ovl_ref, o_ref, sel_ref, *, tq, n_pick, nc):
    i = pl.program_id(1)
    q = q_ref[0]
    gates_t = _sigmoid(gt_ref[0])
    n_pad = kc_ref.shape[3]
    qpos_r = i * tq + lax.broadcasted_iota(jnp.int32, (1, tq), 1)
    n_i = lax.broadcasted_iota(jnp.int32, (n_pad, 1), 0)
    valid = (n_i * CMP_STRIDE + (CMP_LEN - 1) <= qpos_r) & (n_i < nc)
    bias = jnp.where(valid, 0.0, NEG)
    any_valid = qpos_r >= CMP_LEN - 1
    imps = []
    for g in range(NSA_KV):
        q4 = jnp.concatenate([q[:, NSA_DIM * (g * NSA_GROUP + r):NSA_DIM * (g * NSA_GROUP + r + 1)]
                              for r in range(NSA_GROUP)], axis=0)
        s_all = _dot_nt(kc_ref[0, 0, g], q4)
        vct = vct_ref[0, 0, g]
        psum = None
        for r in range(NSA_GROUP):
            hd = g * NSA_GROUP + r
            s = s_all[:, r * tq:(r + 1) * tq] + bias
            m = jnp.max(s, axis=0, keepdims=True)
            p = jnp.exp2(s - m)
            d = jnp.sum(p, axis=0, keepdims=True)
            p = p * jnp.where(any_valid, 1.0 / d, 0.0)
            psum = p if psum is None else psum + p
            o_ref[0, NSA_DIM * hd:NSA_DIM * (hd + 1), :] = _dot(vct, p.astype(BF16)) * gates_t[3 * hd:3 * hd + 1, :]
        imps.append(sum(_dot(ovl_ref[...], part) for part in _split3(psum)))
    for g, selb in enumerate(_select_blocks(imps, qpos_r, n_pick)):
        sel_ref[0, g] = selb.astype(BF16)


def _nsa_cmp(qn, kvc, kvct, smt, ovl_t, tq, n_pick, nc):
    bsz, t, _ = qn.shape
    n_pad = kvc.shape[3]
    nb = ovl_t.shape[0]
    return pl.pallas_call(
        functools.partial(_nsa_cmp_kernel, tq=tq, n_pick=n_pick, nc=nc),
        grid=(bsz, t // tq),
        in_specs=[pl.BlockSpec((1, tq, NSA_WIDTH), lambda b, i: (b, i, 0)),
                  pl.BlockSpec((1, 1, NSA_KV, n_pad, NSA_DIM), lambda b, i: (0, b, 0, 0, 0)),
                  pl.BlockSpec((1, 1, NSA_KV, NSA_DIM, n_pad), lambda b, i: (1, b, 0, 0, 0)),
                  pl.BlockSpec((1, 32, tq), lambda b, i: (b, 0, i)),
                  pl.BlockSpec((nb, n_pad), lambda b, i: (0, 0))],
        out_specs=[pl.BlockSpec((1, NSA_WIDTH, tq), lambda b, i: (b, 0, i)),
                   pl.BlockSpec((1, NSA_KV, nb, tq), lambda b, i: (b, 0, 0, i))],
        out_shape=[jax.ShapeDtypeStruct((bsz, NSA_WIDTH, t), F32),
                   jax.ShapeDtypeStruct((bsz, NSA_KV, nb, t), BF16)],
        compiler_params=_cparams(("arbitrary", "arbitrary")),
        name="nsa_cmp_select",
    )(qn, kvc, kvct, smt, ovl_t)


def _flash_t(k_at, v_at, qt_scr, lo, n_full, hi, mask_at, m_scr, acc_scr, p_scr, s_scr, tk):
    w = qt_scr.shape[2]
    cw = min(w, 128)
    m_scr[...] = jnp.full(m_scr.shape, -3e38, F32)
    acc_scr[...] = jnp.zeros(acc_scr.shape, F32)

    rb = min(tk, 128)
    nrb = tk // rb

    def fold(x, op):
        return op(x.reshape(rb // 8, 8, cw), axis=0) if rb > 8 else x

    def make_body(masked):
        def body(j, carry):
            for g in range(NSA_KV):
                s_scr[g, 0:tk, :] = _dot(k_at(g, j), qt_scr[g])
            for g in range(NSA_KV):
                for c in range(w // cw):
                    cs = slice(c * cw, (c + 1) * cw)
                    m8 = None
                    for i in range(nrb):
                        rows = slice(i * rb, (i + 1) * rb)
                        s = s_scr[g, rows, cs]
                        if masked:
                            s = jnp.where(mask_at(j, rows, c * cw, cw), s, NEG)
                            s_scr[g, rows, cs] = s
                        f = fold(s, jnp.max)
                        m8 = f if m8 is None else jnp.maximum(m8, f)
                    m_prev = m_scr[g, :, cs]
                    m_new = jnp.maximum(m_prev, jnp.max(m8, axis=0, keepdims=True))
                    alpha = jnp.exp2(m_prev - m_new)
                    for i in range(nrb):
                        rows = slice(i * rb, (i + 1) * rb)
                        p_scr[g, rows, cs] = jnp.exp2(s_scr[g, rows, cs] - m_new).astype(BF16)
                    m_scr[g, :, cs] = m_new
                    acc_scr[g, :, cs] = acc_scr[g, :, cs] * alpha
                v1 = jnp.concatenate([v_at(g, j), jnp.ones((8, tk), BF16)], axis=0)
                acc_scr[g] += _dot(v1, p_scr[g, 0:tk, :])
            return carry
        return body

    lax.fori_loop(lo, n_full, make_body(False), 0)
    lax.fori_loop(n_full, hi, make_body(True), 0)
    return [acc_scr[g, 0:NSA_DIM] / acc_scr[g, NSA_DIM:NSA_DIM + 1] for g in range(NSA_KV)]


def _nsa_sw_kernel(q_ref, ka_ref, kw_ref, vt_ref, sel_ref, gt_ref, oc_ref, o_ref,
                   qa_scr, qw_scr, m_scr, acc_scr, p_scr, s_scr, *, tq, tk, tkw):
    i = pl.program_id(1)
    w = NSA_GROUP * tq
    nb = sel_ref.shape[2]
    ka_w = qa_scr.shape[1]
    qt = q_ref[0].astype(F32).T.astype(BF16)
    gates_t = _sigmoid(gt_ref[0])
    q_first = i * tq
    q_last = q_first + tq - 1
    if ka_w > nb + NSA_DIM:
        qa_scr[:, nb + NSA_DIM:, :] = jnp.zeros((NSA_KV, ka_w - nb - NSA_DIM, w), BF16)
    qw_scr[...] = jnp.zeros(qw_scr.shape, BF16)
    for g in range(NSA_KV):
        for r in range(NSA_GROUP):
            hd = g * NSA_GROUP + r
            cols = slice(r * tq, (r + 1) * tq)
            q_h = qt[NSA_DIM * hd:NSA_DIM * (hd + 1), :]
            qa_scr[g, 0:nb, cols] = sel_ref[0, g]
            qa_scr[g, nb:nb + NSA_DIM, cols] = q_h
            qw_scr[g, NSA_DIM * g:NSA_DIM * (g + 1), cols] = q_h

    def ka_at(g, j):
        return ka_ref[0, pl.ds(pl.multiple_of(j * tk, tk), tk), ka_w * g:ka_w * (g + 1)]

    def vs_at(g, j):
        return vt_ref[0, NSA_DIM * g:NSA_DIM * (g + 1), pl.ds(pl.multiple_of(j * tk, tk), tk)]

    def kw_at(g, j):
        return kw_ref[0, pl.ds(pl.multiple_of(j * tkw, tkw), tkw), :]

    def vw_at(g, j):
        return vt_ref[0, 128 + NSA_DIM * g:128 + NSA_DIM * (g + 1), pl.ds(pl.multiple_of(j * tkw, tkw), tkw)]

    def key_pos(j, tile, rows):
        return j * tile + rows.start + lax.broadcasted_iota(jnp.int32, (rows.stop - rows.start, 1), 0)

    def query_pos(col0, cols):
        return q_first + ((col0 + lax.broadcasted_iota(jnp.int32, (1, cols), 1)) & (tq - 1))

    def slc_mask(j, rows, col0, cols):
        return key_pos(j, tk, rows) <= query_pos(col0, cols)

    def win_mask(j, rows, col0, cols):
        kpos = key_pos(j, tkw, rows)
        qp = query_pos(col0, cols)
        return (kpos <= qp) & (kpos >= qp - WINDOW)

    o_s = _flash_t(ka_at, vs_at, qa_scr, 0, (q_first + 1) // tk, q_last // tk + 1, slc_mask,
                   m_scr, acc_scr, p_scr, s_scr, tk)
    w_lo = jnp.maximum(q_first - WINDOW, 0) // tkw
    o_w = _flash_t(kw_at, vw_at, qw_scr, w_lo, w_lo, q_last // tkw + 1, win_mask,
                   m_scr, acc_scr, p_scr, s_scr, tkw)
    outs = []
    for hd in range(NSA_HEADS):
        g, r = divmod(hd, NSA_GROUP)
        cols = slice(r * tq, (r + 1) * tq)
        o_h = (oc_ref[0, NSA_DIM * hd:NSA_DIM * (hd + 1), :]
               + o_s[g][:, cols] * gates_t[3 * hd + 1:3 * hd + 2, :]
               + o_w[g][:, cols] * gates_t[3 * hd + 2:3 * hd + 3, :])
        outs.append(o_h.T)
    o_ref[0] = jnp.concatenate(outs, axis=1)


def _nsa_sw(qn, ka, kw, vt, selb, smt, o_cmp_t, tq, tk, tkw):
    bsz, t, _ = qn.shape
    nb = selb.shape[2]
    assert tq & (tq - 1) == 0
    w = NSA_GROUP * tq
    ka_w = ka.shape[2] // NSA_KV
    seq = lambda a: pl.BlockSpec((1,) + a.shape[1:], lambda b, i: (b, 0, 0))
    return pl.pallas_call(
        functools.partial(_nsa_sw_kernel, tq=tq, tk=tk, tkw=tkw),
        grid=(bsz, t // tq),
        in_specs=[pl.BlockSpec((1, tq, NSA_WIDTH), lambda b, i: (b, i, 0)),
                  seq(ka), seq(kw), seq(vt),
                  pl.BlockSpec((1, NSA_KV, nb, tq), lambda b, i: (b, 0, 0, i)),
                  pl.BlockSpec((1, 32, tq), lambda b, i: (b, 0, i)),
                  pl.BlockSpec((1, NSA_WIDTH, tq), lambda b, i: (b, 0, i))],
        out_specs=pl.BlockSpec((1, tq, NSA_WIDTH), lambda b, i: (b, i, 0)),
        out_shape=jax.ShapeDtypeStruct((bsz, t, NSA_WIDTH), F32),
        scratch_shapes=[pltpu.VMEM((NSA_KV, ka_w, w), BF16),
                        pltpu.VMEM((NSA_KV, 128, w), BF16),
                        pltpu.VMEM((NSA_KV, 1, w), F32),
                        pltpu.VMEM((NSA_KV, NSA_DIM + 8, w), F32),
                        pltpu.VMEM((NSA_KV, max(tk, tkw), w), BF16),
                        pltpu.VMEM((NSA_KV, max(tk, tkw), w), F32)],
        compiler_params=_cparams(("arbitrary", "arbitrary")),
        name="nsa_select_window",
    )(qn, ka, kw, vt, selb, smt, o_cmp_t)


def _ab_out_kernel(x_ref, mod_ref, hm_ref, w_ref, *rest):
    o_ref = rest[-1]
    o_nsa = rest[0][...]
    for part in rest[1:-1]:
        o_nsa = o_nsa + part[...]
    y = _dot(hm_ref[...], w_ref[:ML_WIDTH, :]) + _dot(o_nsa.astype(BF16), w_ref[ML_WIDTH:, :])
    o_ref[...] = x_ref[...] + mod_ref[0, 2] * y


def _ab_out(x, mod4, tiles_per_seq, hm, o_parts, w, tm):
    n = x.shape[0]
    tok = lambda wd: pl.BlockSpec((tm, wd), lambda i: (i, 0))
    return pl.pallas_call(
        _ab_out_kernel,
        grid=(n // tm,),
        in_specs=[tok(D_MODEL), _mod_spec(mod4, tm, tiles_per_seq), tok(ML_WIDTH),
                  pl.BlockSpec(w.shape, lambda i: (0, 0))] + [tok(NSA_WIDTH) for _ in o_parts],
        out_specs=tok(D_MODEL),
        out_shape=jax.ShapeDtypeStruct((n, D_MODEL), F32),
        compiler_params=_cparams(("arbitrary",)),
        name="ab_out",
    )(x, mod4, hm, w, *o_parts)


def _cl_kernel(x_ref, mod_ref, g_ref, wi_ref, bi_ref, vg_ref, ws_ref, bs_ref, wo_ref, o_ref, v_ref, *, tm):
    x = x_ref[...]
    h = _normmod(x, g_ref[...], mod_ref[0, 1], mod_ref[0, 0]).astype(BF16)
    z = _gelu(_dot(h, wi_ref[...]) + bi_ref[...])
    gw = z.shape[1] // 2
    u = z[:, :gw]
    v = z[:, gw:]
    v = v * lax.rsqrt(jnp.mean(v * v, axis=-1, keepdims=True) + EPS) * vg_ref[...]
    v_ref[...] = v
    vb = v.astype(BF16)
    lch = ws_ref.shape[1]
    tril = lax.broadcasted_iota(jnp.int32, (lch, lch), 0) >= lax.broadcasted_iota(jnp.int32, (lch, lch), 1)
    gd = gw // GM_GROUPS
    rows = []
    for c in range(tm // lch):
        cols = []
        for g in range(GM_GROUPS):
            wsg = jnp.where(tril, ws_ref[g], 0.0).astype(BF16)
            cols.append(_dot(wsg, vb[c * lch:(c + 1) * lch, g * gd:(g + 1) * gd]) + bs_ref[:, g:g + 1])
        rows.append(jnp.concatenate(cols, axis=1))
    s = jnp.concatenate(rows, axis=0) if len(rows) > 1 else rows[0]
    y = _dot((u * s).astype(BF16), wo_ref[...])
    o_ref[...] = x + mod_ref[0, 2] * y


def _cl_mixer(x, mod4, tiles_per_seq, g, wi, bi, vg, ws_eff, bs_eff, wo, tm):
    n = x.shape[0]
    gw = wo.shape[0]
    tok = lambda wd: pl.BlockSpec((tm, wd), lambda i: (i, 0))
    full = lambda a: pl.BlockSpec(a.shape, lambda i: (0,) * a.ndim)
    return pl.pallas_call(
        functools.partial(_cl_kernel, tm=tm),
        grid=(n // tm,),
        in_specs=[tok(D_MODEL), _mod_spec(mod4, tm, tiles_per_seq), full(g), full(wi), full(bi), full(vg),
                  full(ws_eff), full(bs_eff), full(wo)],
        out_specs=[tok(D_MODEL), tok(gw)],
        out_shape=[jax.ShapeDtypeStruct((n, D_MODEL), F32), jax.ShapeDtypeStruct((n, gw), F32)],
        compiler_params=_cparams(("arbitrary",)),
        name="gmlp_mixer",
    )(x, mod4, g, wi, bi, vg, ws_eff, bs_eff, wo)


def _ffn_kernel(x_ref, mod_ref, g_ref, w1_ref, w2_ref, fg_ref, o_ref, h_scr, acc_scr, *, final):
    j = pl.program_id(1)

    @pl.when(j == 0)
    def _():
        h_scr[...] = _normmod(x_ref[...], g_ref[...], mod_ref[0, 4], mod_ref[0, 3]).astype(BF16)
        acc_scr[...] = jnp.zeros(acc_scr.shape, F32)

    a = jnp.maximum(_dot(h_scr[...], w1_ref[0]), 0.0)
    acc_scr[...] += _dot((a * a).astype(BF16), w2_ref[0])

    @pl.when(j == pl.num_programs(1) - 1)
    def _():
        y = x_ref[...] + mod_ref[0, 5] * acc_scr[...]
        if final:
            y = y * lax.rsqrt(jnp.mean(y * y, axis=-1, keepdims=True) + EPS) * fg_ref[...]
        o_ref[...] = y


def _ffn(x, mod4, tiles_per_seq, g, w1, w2, layer, final_g, final, tm, tf):
    n = x.shape[0]
    f = w1.shape[2]
    r = mod4.shape[2]
    if r == 1:
        mspec = pl.BlockSpec((1, 6, 1, D_MODEL), lambda i, j: (i // tiles_per_seq, 0, 0, 0))
    else:
        mspec = pl.BlockSpec((1, 6, tm, D_MODEL), lambda i, j: (i, 0, 0, 0))
    return pl.pallas_call(
        functools.partial(_ffn_kernel, final=final),
        grid=(n // tm, f // tf),
        in_specs=[pl.BlockSpec((tm, D_MODEL), lambda i, j: (i, 0)), mspec,
                  pl.BlockSpec((1, D_MODEL), lambda i, j: (0, 0)),
                  pl.BlockSpec((1, D_MODEL, tf), lambda i, j: (layer, 0, j)),
                  pl.BlockSpec((1, tf, D_MODEL), lambda i, j: (layer, j, 0)),
                  pl.BlockSpec((1, D_MODEL), lambda i, j: (0, 0))],
        out_specs=pl.BlockSpec((tm, D_MODEL), lambda i, j: (i, 0)),
        out_shape=jax.ShapeDtypeStruct((n, D_MODEL), F32),
        scratch_shapes=[pltpu.VMEM((tm, D_MODEL), BF16), pltpu.VMEM((tm, D_MODEL), F32)],
        compiler_params=_cparams(("arbitrary", "arbitrary")),
        name="ffn",
    )(x, mod4, g, w1, w2, final_g)


SQ = 16


def _page_specs(pages_per_step, row_block):
    def in_map(p):
        return lambda b, s, pt: (pt[b, s * pages_per_step + p], row_block, 0)
    return [pl.BlockSpec((1, 256, PAGE_SIZE), in_map(p)) for p in range(pages_per_step)]


def _decode_cmp_kernel(pt_ref, *refs, pages_per_step, past, n_pick):
    pages = refs[:pages_per_step]
    (q_ref, sm_ref, ovl_ref, wc_ref, w1_ref, pe_ref, b1_ref, w2_ref, b2_ref, o_ref, sel_ref, x_scr) = refs[pages_per_step:]
    s_idx = pl.program_id(1)
    for p in range(pages_per_step):
        row0 = pl.multiple_of((s_idx * pages_per_step + p) * PAGE_SIZE, PAGE_SIZE)
        for kind in range(2):
            x_scr[kind, pl.ds(row0, PAGE_SIZE), :] = pages[p][0, 128 * kind:128 * (kind + 1), :].T

    @pl.when(s_idx == pl.num_programs(1) - 1)
    def _():
        n_sub = past // CMP_STRIDE
        nc = n_sub - 1
        kv = [_compress_math(lambda p: x_scr[kind, pl.ds(p, n_sub, stride=CMP_STRIDE), :],
                             lambda p: wc_ref[kind, p], w1_ref[kind], pe_ref[kind], b1_ref[kind],
                             w2_ref[kind], b2_ref[kind], n_sub) for kind in range(2)]
        q = q_ref[0]
        gates = _sigmoid(sm_ref[0])
        qpos = past + lax.broadcasted_iota(jnp.int32, (SQ, 1), 0)
        n_i = lax.broadcasted_iota(jnp.int32, (1, n_sub), 1)
        valid = (n_i * CMP_STRIDE + (CMP_LEN - 1) <= qpos) & (n_i < nc)
        outs, imps = [], []
        for g in range(NSA_KV):
            kc = kv[0][g].astype(BF16)
            vc = kv[1][g].astype(BF16)
            psum = jnp.zeros((SQ, n_sub), F32)
            for r in range(NSA_GROUP):
                hd = g * NSA_GROUP + r
                s = _dot_nt(q[:, NSA_DIM * hd:NSA_DIM * (hd + 1)], kc)
                s = jnp.where(valid, s, NEG)
                m = jnp.max(s, axis=1, keepdims=True)
                p = jnp.where(valid, jnp.exp2(s - m), 0.0)
                d = jnp.sum(p, axis=1, keepdims=True)
                p = p / jnp.where(d > 0, d, 1.0)
                psum = psum + p
                outs.append(_dot(p.astype(BF16), vc) * gates[:, 3 * hd:3 * hd + 1])
            psum = jnp.concatenate([psum, jnp.zeros((128 - SQ, n_sub), F32)], axis=0)
            imps.append(sum(_dot_nt(ovl_ref[...], part) for part in _split3(psum)))
        qpos_r = past + lax.broadcasted_iota(jnp.int32, (1, 128), 1)
        for g, selb in enumerate(_select_blocks(imps, qpos_r, n_pick)):
            sel_ref[0, g] = selb.T[0:SQ]
        o_ref[0] = jnp.concatenate(outs, axis=1)


def _decode_cmp(cache_t, page_table, qn, sm, ovl, cw, pages_per_step, n_pick):
    bsz, n_pages = page_table.shape
    past = n_pages * PAGE_SIZE
    nb = ovl.shape[0]
    full = lambda a: pl.BlockSpec(a.shape, lambda b, s, pt: (0,) * a.ndim)
    seq = lambda a: pl.BlockSpec((1,) + a.shape[1:], lambda b, s, pt: (b,) + (0,) * (a.ndim - 1))
    ws = [cw['wc'], cw['w1cat'], cw['pe'], cw['b1'], cw['w2'], cw['b2']]
    return pl.pallas_call(
        functools.partial(_decode_cmp_kernel, pages_per_step=pages_per_step, past=past, n_pick=n_pick),
        grid_spec=pltpu.PrefetchScalarGridSpec(
            num_scalar_prefetch=1,
            grid=(bsz, n_pages // pages_per_step),
            in_specs=_page_specs(pages_per_step, 0) + [seq(qn), seq(sm), full(ovl)] + [full(a) for a in ws],
            out_specs=[pl.BlockSpec((1, SQ, NSA_WIDTH), lambda b, s, pt: (b, 0, 0)),
                       pl.BlockSpec((1, NSA_KV, SQ, nb), lambda b, s, pt: (b, 0, 0, 0))],
            scratch_shapes=[pltpu.VMEM((2, past, 128), F32)]),
        out_shape=[jax.ShapeDtypeStruct((bsz, SQ, NSA_WIDTH), F32),
                   jax.ShapeDtypeStruct((bsz, NSA_KV, SQ, nb), F32)],
        compiler_params=_cparams(("arbitrary", "arbitrary")),
        name="decode_compress_select",
    )(page_table, *([cache_t] * pages_per_step), qn, sm, ovl, *ws)


def _decode_sw_kernel(pt_ref, *refs, pages_per_step, past, n_new):
    pages = refs[:pages_per_step]
    (et_ref, q_ref, sel_ref, kvn_ref, win_ref, sm_ref, o_ref, qa_scr, m_scr, l_scr, acc_scr) = refs[pages_per_step:]
    s_idx = pl.program_id(1)
    nb = sel_ref.shape[3]
    rows_n = NSA_HEADS * SQ
    wlen = win_ref.shape[2]

    @pl.when(s_idx == 0)
    def _():
        q = q_ref[0]
        qa_scr[...] = jnp.zeros(qa_scr.shape, F32)
        for hd in range(NSA_HEADS):
            g = hd // NSA_GROUP
            rows = slice(hd * SQ, (hd + 1) * SQ)
            qa_scr[rows, NSA_DIM * g:NSA_DIM * (g + 1)] = q[:, NSA_DIM * hd:NSA_DIM * (hd + 1)].astype(F32)
            qa_scr[rows, 128:128 + nb] = sel_ref[0, g]
        m_scr[...] = jnp.full(m_scr.shape, -3e38, F32)
        l_scr[...] = jnp.zeros(l_scr.shape, F32)
        acc_scr[...] = jnp.zeros(acc_scr.shape, F32)

    qa = qa_scr[...].astype(BF16)

    def online(s, pv):
        m_prev = m_scr[...]
        m_new = jnp.maximum(m_prev, jnp.max(s, axis=1, keepdims=True))
        alpha = jnp.exp2(m_prev - m_new)
        p = jnp.exp2(s - m_new[:, 0:1])
        l_scr[...] = alpha * l_scr[...] + jnp.sum(p, axis=1, keepdims=True)
        acc_scr[...] = alpha * acc_scr[...] + pv(p.astype(BF16))
        m_scr[...] = m_new

    n_pair = pages_per_step // 2
    s_parts = []
    for pair in range(n_pair):
        pa, pb = pages[2 * pair], pages[2 * pair + 1]
        kt = jnp.concatenate([pa[0, 0:128, :], pb[0, 0:128, :]], axis=1).astype(BF16)
        off = pl.multiple_of((s_idx * pages_per_step + 2 * pair) * PAGE_SIZE, 2 * PAGE_SIZE)
        rhs = jnp.concatenate([kt, et_ref[:, pl.ds(off, 2 * PAGE_SIZE)]], axis=0)
        s_parts.append(_dot(qa, rhs))

    def step_pv(p):
        out = None
        for pair in range(n_pair):
            pa, pb = pages[2 * pair], pages[2 * pair + 1]
            vt = jnp.concatenate([pa[0, 128:256, :], pb[0, 128:256, :]], axis=1).astype(BF16)
            d = _dot_nt(p[:, 2 * PAGE_SIZE * pair:2 * PAGE_SIZE * (pair + 1)], vt)
            out = d if out is None else out + d
        return out

    online(jnp.concatenate(s_parts, axis=1), step_pv)

    @pl.when(s_idx == pl.num_programs(1) - 1)
    def _():
        tok = lax.broadcasted_iota(jnp.int32, (rows_n, 1), 0) & (SQ - 1)
        new_i = lax.broadcasted_iota(jnp.int32, (1, SQ), 1)
        new_ok = (new_i <= tok) & (new_i < n_new)
        kvn = kvn_ref[0]
        q2 = qa[:, 0:128]
        online(jnp.where(new_ok, _dot_nt(q2, kvn[:, 256:384]), NEG), lambda p: _dot(p, kvn[:, 384:512]))
        o_s = acc_scr[...] / l_scr[...]
        s_w = _dot(q2, win_ref[0, 0:128, :].astype(BF16))
        w_i = lax.broadcasted_iota(jnp.int32, (1, wlen), 1)
        s_w = jnp.where(w_i >= tok + (wlen - WINDOW), s_w, NEG)
        s_n = jnp.where(new_ok, _dot_nt(q2, kvn[:, 512:640]), NEG)
        m_w = jnp.maximum(jnp.max(s_w, axis=1, keepdims=True), jnp.max(s_n, axis=1, keepdims=True))
        p_w = jnp.exp2(s_w - m_w)
        p_n = jnp.exp2(s_n - m_w)
        l_w = jnp.sum(p_w, axis=1, keepdims=True) + jnp.sum(p_n, axis=1, keepdims=True)
        o_w = (_dot_nt(p_w.astype(BF16), win_ref[0, 128:256, :].astype(BF16))
               + _dot(p_n.astype(BF16), kvn[:, 640:768])) / l_w
        gates = _sigmoid(sm_ref[0])
        g_s = jnp.concatenate([gates[:, 3 * hd + 1:3 * hd + 2] for hd in range(NSA_HEADS)], axis=0)
        g_w = jnp.concatenate([gates[:, 3 * hd + 2:3 * hd + 3] for hd in range(NSA_HEADS)], axis=0)
        o = o_s * g_s + o_w * g_w
        half = rows_n // 2
        o_ref[0, 0:half, :] = o[0:half, 0:NSA_DIM]
        o_ref[0, half:, :] = o[half:, NSA_DIM:]


def _decode_sw(cache_t, page_table, et, qn, selb, kvn, win_t, sm, pages_per_step, n_new):
    bsz, n_pages = page_table.shape
    past = n_pages * PAGE_SIZE
    nb = et.shape[0]
    rows_n = NSA_HEADS * SQ
    full = lambda a: pl.BlockSpec(a.shape, lambda b, s, pt: (0,) * a.ndim)
    seq = lambda a: pl.BlockSpec((1,) + a.shape[1:], lambda b, s, pt: (b,) + (0,) * (a.ndim - 1))
    return pl.pallas_call(
        functools.partial(_decode_sw_kernel, pages_per_step=pages_per_step, past=past, n_new=n_new),
        grid_spec=pltpu.PrefetchScalarGridSpec(
            num_scalar_prefetch=1,
            grid=(bsz, n_pages // pages_per_step),
            in_specs=_page_specs(pages_per_step, 1) + [full(et), seq(qn), seq(selb), seq(kvn), seq(win_t), seq(sm)],
            out_specs=pl.BlockSpec((1, rows_n, NSA_DIM), lambda b, s, pt: (b, 0, 0)),
            scratch_shapes=[pltpu.VMEM((rows_n, 128 + nb), F32),
                            pltpu.VMEM((rows_n, 128), F32),
                            pltpu.VMEM((rows_n, 128), F32),
                            pltpu.VMEM((rows_n, 128), F32)]),
        out_shape=jax.ShapeDtypeStruct((bsz, rows_n, NSA_DIM), F32),
        compiler_params=_cparams(("arbitrary", "arbitrary")),
        name="decode_select_window",
    )(page_table, *([cache_t] * pages_per_step), et, qn, selb, kvn, win_t, sm)


def _selection_constants(n_keys):
    nb = n_keys // SEL_BLOCK
    n_sub = n_keys // CMP_STRIDE
    i = np.arange(n_sub)[None, :] * CMP_STRIDE
    j = np.arange(nb)[:, None] * SEL_BLOCK
    ovl_t = ((i < j + SEL_BLOCK) & (i + CMP_LEN > j) & (np.arange(n_sub)[None, :] < n_sub - 1))
    et = (np.arange(n_keys)[None, :] // SEL_BLOCK) == np.arange(nb)[:, None]
    return jnp.asarray(ovl_t, BF16), jnp.asarray(et, BF16)


def _ab_layer_prompt(x, mod4, tps, W, l, bsz, t, cfg):
    n = bsz * t
    j = l // 2
    tm, lc = cfg['tm'], cfg['lc']
    qk, v_m, o_pre, qn, kvr, win, sm, ka, kw, vt, smt = _ab_in(x, mod4, tps, W['norm_g0'][l], W['ab_w_in'][j],
                                                          W['ab_b_in'][j], tm, seq_shape=(bsz, t))
    seq = lambda a: a.reshape(bsz, t, a.shape[1])
    conv0 = jnp.zeros((bsz, CONV_W - 1, 2 * ML_WIDTH), F32)
    c0 = jnp.zeros((bsz, ML_HEADS, ML_DIM, ML_DIM), F32)
    n0 = jnp.zeros((bsz, ML_HEADS, ML_DIM), F32)
    m0 = jnp.zeros((bsz, ML_HEADS), F32)
    hm, c1, n1, m1 = _mlstm(seq(qk), seq(v_m), seq(o_pre), seq(sm), smt, W['ml_conv_w'][j], W['ml_conv_b'][j],
                            W['ml_f_bias'][j], W['ml_out_g'][j], conv0, c0, n0, m0, lc, gr_block=_SM_I // 8)
    conv_new = seq(qk)[:, -(CONV_W - 1):]
    kv_rows = kvr.reshape(bsz, t, 4, NSA_KV, NSA_DIM)
    win_rows = win.reshape(bsz, t, 2, NSA_KV, NSA_DIM)
    kvc, kvct = _compress(seq(kvr), W['cmp'][j])
    ovl_t, _ = _selection_constants(t)
    o_cmp_t, selb = _nsa_cmp(seq(qn), kvc, kvct, smt, ovl_t, cfg['tq_cmp'], N_SEL, t // CMP_STRIDE - 1)
    o_nsa = _nsa_sw(seq(qn), seq(ka), seq(kw), vt, selb, smt, o_cmp_t, cfg['tq'], cfg['tk'], cfg['tkw'])
    x = _ab_out(x, mod4, tps, hm.reshape(n, ML_WIDTH), [o_nsa.reshape(n, NSA_WIDTH)], W['ab_w_out'][j], tm)
    return x, (conv_new, c1, n1, m1, kv_rows, win_rows[:, -min(WINDOW, t):])


def _ab_layer_decode(x, mod4, W, l, bsz, t, cfg, st, page_table):
    n = bsz * t
    j = l // 2
    tm, lc = cfg['tm'], cfg['lc']
    conv0, c0, n0, m0, (cache_t, page0), win_buf, win_t = st
    page_table = page_table + page0
    qk, v_m, o_pre, qn, kvr, win, sm, kvb = _ab_in(x, mod4, 1, W['norm_g0'][l], W['ab_w_in'][j], W['ab_b_in'][j], tm)
    seq = lambda a: a.reshape(bsz, t, a.shape[1])
    pad_t = lambda a, tp: jnp.pad(a, ((0, 0), (0, tp - t), (0, 0)))
    sm3 = seq(sm)
    gr = sm3[:, :, _SM_I:_SM_I + 8].transpose(0, 2, 1)
    gr = jnp.concatenate([jnp.pad(gr[:, :4], ((0, 0), (0, 0), (0, lc - t)), constant_values=NEG),
                          jnp.pad(gr[:, 4:], ((0, 0), (0, 0), (0, lc - t)), constant_values=-NEG)], axis=1)
    sm_pad = jnp.zeros((bsz, lc - t, 128), F32).at[:, :, _SM_I:_SM_I + 4].set(NEG).at[:, :, _SM_F:_SM_F + 4].set(-NEG)
    hm, c1, n1, m1 = _mlstm(pad_t(seq(qk), lc), pad_t(seq(v_m), lc), pad_t(seq(o_pre), lc),
                            jnp.concatenate([sm3, sm_pad], axis=1), gr, W['ml_conv_w'][j], W['ml_conv_b'][j],
                            W['ml_f_bias'][j], W['ml_out_g'][j], conv0, c0, n0, m0, lc)
    hm = hm[:, :t].reshape(n, ML_WIDTH)
    conv_new = jnp.concatenate([conv0, seq(qk)], axis=1)[:, -(CONV_W - 1):]
    kv_rows = kvr.reshape(bsz, t, 4, NSA_KV, NSA_DIM)
    win_rows = win.reshape(bsz, t, 2, NSA_KV, NSA_DIM)
    past = page_table.shape[1] * PAGE_SIZE
    assert (past + t) // CMP_STRIDE == past // CMP_STRIDE and t <= min(SEL_BLOCK, SQ) and past % SEL_BLOCK == 0
    ovl_t, et = _selection_constants(past)
    qn3, sm3q, kvn = pad_t(seq(qn), SQ), pad_t(sm3, SQ), pad_t(seq(kvb), SQ)
    pps = cfg['pages_per_step']
    o_cmp, selb = _decode_cmp(cache_t, page_table, qn3, sm3q, ovl_t, W['cmp'][j], pps, N_SEL - 1)
    o_sw = _decode_sw(cache_t, page_table, et, qn3, selb, kvn, win_t, sm3q, pps, t)
    o_cmp = o_cmp[:, :t].reshape(n, NSA_WIDTH)
    o_sw = o_sw.reshape(bsz, NSA_HEADS, SQ, NSA_DIM).transpose(0, 2, 1, 3)[:, :t].reshape(n, NSA_WIDTH)
    x = _ab_out(x, mod4, 1, hm, [o_cmp, o_sw], W['ab_w_out'][j], tm)
    win_new = jnp.concatenate([win_buf, win_rows], axis=1)[:, -win_buf.shape[1]:]
    return x, (conv_new, c1, n1, m1, kv_rows, win_new)


def _trunk(x3, mods, W, state, page_table, cfg):
    bsz, t, _ = x3.shape
    n = bsz * t
    tm = cfg['tm']
    x = x3.reshape(n, D_MODEL)
    depth = mods.shape[0]
    ab_new, cl_new = [], []
    if t % tm == 0:
        tps = t // tm
        to_mod4 = lambda m: m.reshape(bsz, 6, 1, D_MODEL)
    else:
        assert n == tm
        tps = 1
        to_mod4 = lambda m: jnp.repeat(m.reshape(bsz, 6, D_MODEL), t, axis=0).reshape(n, 6, D_MODEL).transpose(1, 0, 2)[None]
    lch = min(GM_CHUNK, t)
    for l in range(depth):
        mod4 = to_mod4(mods[l])
        j = l // 2
        if l % 2 == 0:
            if state is None:
                x, new = _ab_layer_prompt(x, mod4, tps, W, l, bsz, t, cfg)
            else:
                x, new = _ab_layer_decode(x, mod4, W, l, bsz, t, cfg, tuple(a[j] for a in state), page_table)
            ab_new.append(new)
        else:
            ws = W['cl_ws'][j][:, :lch, :lch]
            bs = W['cl_bs'][j][:, :lch]
            if lch < GM_CHUNK:
                rep = GM_CHUNK // lch
                ws = jnp.einsum('ab,gts->gatbs', jnp.eye(rep, dtype=F32), ws).reshape(GM_GROUPS, GM_CHUNK, GM_CHUNK)
                bs = jnp.tile(bs, (1, rep))
            x, v = _cl_mixer(x, mod4, tps, W['norm_g0'][l], W['cl_w_in'][j], W['cl_b_in'][j], W['cl_v_g'][j],
                             ws, bs.T, W['cl_w_out'][j], tm)
            cl_new.append(v.reshape(bsz, t, -1))
        x = _ffn(x, mod4, max(t // cfg['tm_ffn'], 1), W['norm_g1'][l], W['ffn_w1'], W['ffn_w2'], l, W['final_g'],
                 l == depth - 1, cfg['tm_ffn'], cfg['tf'])
    return x.reshape(bsz, t, D_MODEL), ab_new, cl_new


def kernel(x_prompt, x_sample, c_prompt, c_sample, state_mlstm_conv, state_mlstm_C, state_mlstm_n,
           state_mlstm_m, cache_nsa_kv, state_nsa_win, page_table, ada_w, ada_b, norm_g, ab_w_in, ab_b_in,
           ml_conv_w, ml_conv_b, ml_f_bias, ml_out_g, phi_pe, phi_w1, phi_b1, phi_w2, phi_b2, ab_w_out,
           cl_w_in, cl_b_in, cl_v_g, cl_ws, cl_bs, cl_w_out, ffn_w1, ffn_w2, final_g):
    depth = ada_w.shape[0]
    n_ab = ab_w_in.shape[0]
    bp, bs_ = c_prompt.shape[0], c_sample.shape[0]
    rows = bp + bs_
    rows_pad = -(-rows // 8) * 8
    c_all = jnp.concatenate([c_prompt, c_sample, jnp.zeros((rows_pad - rows, D_MODEL), F32)], axis=0)
    mods = _ada_mod(c_all, ada_w, ada_b)
    mods_p = mods[:, :bp]
    mods_s = mods[:, bp:rows]
    w_in = jnp.concatenate([ab_w_in[:, :, 0:2048], ab_w_in[:, :, 2056:3336], ab_w_in[:, :, 3336:3360],
                            ab_w_in[:, :, 2048:2056], jnp.zeros((n_ab, D_MODEL, 96), F32)], axis=2).astype(BF16)
    b_in = jnp.concatenate([ab_b_in[:, 0:2048], ab_b_in[:, 2056:3336], ab_b_in[:, 3336:3360],
                            ab_b_in[:, 2048:2056], jnp.zeros((n_ab, 96), F32)], axis=1)[:, None, :]
    half = CMP_STRIDE * NSA_DIM
    blk = phi_w1.reshape(n_ab, 2, 2, CMP_STRIDE, NSA_DIM, NSA_DIM).transpose(0, 1, 3, 4, 2, 5)
    blk = blk.reshape(n_ab, 2, CMP_STRIDE, NSA_DIM, 2 * NSA_DIM)
    zero = jnp.zeros_like(blk)
    wc = jnp.concatenate([jnp.concatenate([blk, zero], axis=4), jnp.concatenate([zero, blk], axis=4)], axis=3)
    pe = jnp.pad(phi_pe.reshape(n_ab, 2, 2, half), ((0, 0), (0, 0), (0, 6), (0, 0)))
    cmp_w = [dict(wc=wc[j].astype(BF16),
                  w1cat=jnp.concatenate([phi_w1[j, :, :half], phi_w1[j, :, half:]], axis=2).astype(BF16),
                  pe=pe[j], b1=phi_b1[j][:, None, :], w2=phi_w2[j].astype(BF16), b2=phi_b2[j][:, None, :])
             for j in range(n_ab)]
    W = dict(
        norm_g0=norm_g[:, 0][:, None, :], norm_g1=norm_g[:, 1][:, None, :],
        ab_w_in=w_in, ab_b_in=b_in,
        ml_conv_w=ml_conv_w, ml_conv_b=ml_conv_b[:, None, :], ml_f_bias=ml_f_bias, ml_out_g=ml_out_g[:, None, :],
        cmp=cmp_w,
        ab_w_out=ab_w_out.astype(BF16),
        cl_w_in=cl_w_in.astype(BF16), cl_b_in=cl_b_in[:, None, :], cl_v_g=cl_v_g[:, None, :],
        cl_ws=cl_ws, cl_bs=cl_bs, cl_w_out=cl_w_out.astype(BF16),
        ffn_w1=ffn_w1.astype(BF16), ffn_w2=ffn_w2.astype(BF16), final_g=final_g[None, :])
    cfg_p = dict(tm=256, tm_ffn=512, tf=2048, lc=512, tq_cmp=128, tq=256, tk=512, tkw=256)
    y_prompt, ab_p, _ = _trunk(x_prompt, mods_p, W, None, None, cfg_p)
    n_s = x_sample.shape[0] * x_sample.shape[1]
    cfg_s = dict(tm=n_s, tm_ffn=n_s, tf=512, lc=128, pages_per_step=32)
    n_pool = cache_nsa_kv.shape[1]
    cache_t = cache_nsa_kv.transpose(0, 1, 3, 4, 5, 2).reshape(n_ab * n_pool, 4 * NSA_KV * NSA_DIM, PAGE_SIZE)
    cache_t = [(cache_t, j * n_pool) for j in range(n_ab)]
    wlen = state_nsa_win.shape[2]
    win_t = state_nsa_win.transpose(0, 1, 3, 4, 5, 2).reshape(n_ab, bs_, 2 * NSA_KV * NSA_DIM, wlen)
    state = (state_mlstm_conv, state_mlstm_C, state_mlstm_n, state_mlstm_m, cache_t, state_nsa_win, win_t)
    y_sample, ab_s, cl_s = _trunk(x_sample, mods_s, W, state, page_table, cfg_s)
    p_out = [jnp.stack(a) for a in zip(*ab_p)]
    s_out = [jnp.stack(a) for a in zip(*ab_s)]
    return (y_prompt, y_sample, *p_out, *s_out, jnp.stack(cl_s))
```
